```python
import math
import jax
import jax.numpy as jnp
from jax import lax
import numpy as np


D_MODEL = 4096
BATCH = 4
SEQ = 2048
DEPTH = 2

MEM_LEN = 256
MLA_HEADS = 16
MLA_NOPE = 128
MLA_ROPE = 64
MLA_V = 128
MLA_Q_RANK = 1024
MLA_KV_RANK = 512
SB_HEADS = 16
SB_DIM = 128
X_HEADS = 4
X_DIM = 128
FFN_HIDDEN = -(-(8 * D_MODEL) // (3 * 256)) * 256
Q_BLOCK = 128
ROPE_THETA = 10000.0
EPS = 1e-6
MLA_QK = MLA_NOPE + MLA_ROPE
SB_WIDTH = SB_HEADS * SB_DIM
MLA_WIDTH = MLA_HEADS * MLA_V
MIX_WIDTH = MLA_WIDTH + SB_WIDTH
IN_COLS = MLA_Q_RANK + MLA_KV_RANK + MLA_ROPE + 3 * SB_WIDTH

kernel_name = "hymba_mla_stickbreaking_hybrid"


def rmsnorm(x, g):
    xf = x.astype(jnp.float32)
    y = xf * lax.rsqrt(jnp.mean(xf * xf, axis=-1, keepdims=True) + EPS)
    return (y * g.astype(jnp.float32)).astype(x.dtype)


def rope(x, positions):
    half = x.shape[-1] // 2
    inv_freq = ROPE_THETA ** (-jnp.arange(half, dtype=jnp.float32) / half)
    ang = positions.astype(jnp.float32)[..., None] * inv_freq
    ang = ang.reshape(ang.shape[:2] + (1,) * (x.ndim - 3) + (half,))
    cos, sin = jnp.cos(ang), jnp.sin(ang)
    xf = x.astype(jnp.float32)
    x1, x2 = xf[..., :half], xf[..., half:]
    return jnp.concatenate([x1 * cos - x2 * sin, x1 * sin + x2 * cos], axis=-1).astype(x.dtype)


def mla_attention(cq, ckv, k_rope, positions, w_q_up, w_kv_up, g_q_lat, g_kv_lat, g_q, g_k):
    B, S = cq.shape[:2]
    q = (rmsnorm(cq, g_q_lat) @ w_q_up).reshape(B, S, MLA_HEADS, MLA_QK)
    kv = (rmsnorm(ckv, g_kv_lat) @ w_kv_up).reshape(B, S, MLA_HEADS, MLA_NOPE + MLA_V)
    k_nope, v = kv[..., :MLA_NOPE], kv[..., MLA_NOPE:]
    q_nope = rmsnorm(q[..., :MLA_NOPE], g_q[:MLA_NOPE])
    q_rope = rope(rmsnorm(q[..., MLA_NOPE:], g_q[MLA_NOPE:]), positions)
    k_nope = rmsnorm(k_nope, g_k[:MLA_NOPE])
    k_rope = rope(rmsnorm(k_rope, g_k[MLA_NOPE:]), positions)
    scale = 1.0 / math.sqrt(MLA_QK)
    outs = []
    for i in range(S // Q_BLOCK):
        lo, hi = i * Q_BLOCK, (i + 1) * Q_BLOCK
        s = (jnp.einsum('bqhd,bkhd->bhqk', q_nope[:, lo:hi], k_nope[:, :hi])
             + jnp.einsum('bqhr,bkr->bhqk', q_rope[:, lo:hi], k_rope[:, :hi])).astype(jnp.float32) * scale
        causal = jnp.arange(hi)[None, :] <= (lo + jnp.arange(Q_BLOCK))[:, None]
        p = jax.nn.softmax(jnp.where(causal, s, -jnp.inf), axis=-1)
        outs.append(jnp.einsum('bhqk,bkhd->bqhd', p.astype(v.dtype), v[:, :hi]))
    return jnp.concatenate(outs, axis=1).reshape(B, S, MLA_WIDTH)


def stick_breaking_attention(q, k, v):
    B, S = q.shape[:2]
    scale = 1.0 / math.sqrt(SB_DIM)
    outs = []
    for i in range(S // Q_BLOCK):
        lo, hi = i * Q_BLOCK, (i + 1) * Q_BLOCK
        z = jnp.einsum('bqhd,bkhd->bhqk', q[:, lo:hi], k[:, :hi]).astype(jnp.float32) * scale
        strict = jnp.arange(hi)[None, :] < (lo + jnp.arange(Q_BLOCK))[:, None]
        log_beta = jax.nn.log_sigmoid(z)
        log_keep = jnp.where(strict, jax.nn.log_sigmoid(-z), 0.0)
        log_rest = lax.cumsum(log_keep, axis=3, reverse=True) - log_keep
        a = jnp.where(strict, jnp.exp(log_beta + log_rest), 0.0)
        outs.append(jnp.einsum('bhqk,bkhd->bqhd', a.astype(v.dtype), v[:, :hi]))
    return jnp.concatenate(outs, axis=1).reshape(B, S, SB_WIDTH)


def memory_cross_attention(h, mem, w_xq, w_xkv, g_mem, g_xq, g_xk, w_xo):
    B, S = h.shape[:2]
    M = mem.shape[1]
    q = rmsnorm((h @ w_xq).reshape(B, S, X_HEADS, X_DIM), g_xq)
    kv = (rmsnorm(mem, g_mem) @ w_xkv).reshape(B, M, X_HEADS, 2 * X_DIM)
    k = rmsnorm(kv[..., :X_DIM], g_xk)
    v = kv[..., X_DIM:]
    s = jnp.einsum('bqhd,bmhd->bhqm', q, k).astype(jnp.float32) * (1.0 / math.sqrt(X_DIM))
    p = jax.nn.softmax(s, axis=-1)
    o = jnp.einsum('bhqm,bmhd->bqhd', p.astype(v.dtype), v).reshape(B, S, X_HEADS * X_DIM)
    return o @ w_xo


def _w(k, shape, fan_in):
    return jax.random.normal(k, shape, jnp.float32) * (fan_in ** -0.5)


def _g(k, shape):
    return 1.0 + 0.02 * jax.random.normal(k, shape, jnp.float32)


def setup_inputs(seed: int = 0) -> dict:
    key = jax.random.key(seed)
    ks = jax.random.split(key, 26)
    L = DEPTH
    x = jax.random.normal(ks[0], (BATCH, SEQ, D_MODEL), jnp.float32)
    mem = jax.random.normal(ks[1], (BATCH, MEM_LEN, D_MODEL), jnp.float32)
    offsets = jax.random.randint(ks[2], (BATCH, 1), 0, 1024, dtype=jnp.int32)
    positions = (jnp.arange(SEQ, dtype=jnp.int32)[None, :] + offsets).astype(jnp.int32)
    return {
        "x": x,
        "mem": mem,
        "positions": positions,
        "g_attn": _g(ks[3], (L, D_MODEL)),
        "w_in": _w(ks[4], (L, D_MODEL, IN_COLS), D_MODEL),
        "g_q_lat": _g(ks[5], (L, MLA_Q_RANK)),
        "g_kv_lat": _g(ks[6], (L, MLA_KV_RANK)),
        "w_q_up": _w(ks[7], (L, MLA_Q_RANK, MLA_HEADS * MLA_QK), MLA_Q_RANK),
        "w_kv_up": _w(ks[8], (L, MLA_KV_RANK, MLA_HEADS * (MLA_NOPE + MLA_V)), MLA_KV_RANK),
        "g_mla_q": _g(ks[9], (L, MLA_QK)),
        "g_mla_k": _g(ks[10], (L, MLA_QK)),
        "g_mla_out": _g(ks[11], (L, MLA_WIDTH)),
        "g_sb_out": _g(ks[12], (L, SB_WIDTH)),
        "w_out": _w(ks[13], (L, MIX_WIDTH, D_MODEL), MIX_WIDTH),
        "g_cross": _g(ks[14], (L, D_MODEL)),
        "g_mem": _g(ks[15], (L, D_MODEL)),
        "w_xq": _w(ks[16], (L, D_MODEL, X_HEADS * X_DIM), D_MODEL),
        "w_xkv": _w(ks[17], (L, D_MODEL, 2 * X_HEADS * X_DIM), D_MODEL),
        "g_xq": _g(ks[18], (L, X_DIM)),
        "g_xk": _g(ks[19], (L, X_DIM)),
        "w_xo": _w(ks[20], (L, X_HEADS * X_DIM, D_MODEL), X_HEADS * X_DIM),
        "g_ffn": _g(ks[21], (L, D_MODEL)),
        "w_gate": _w(ks[22], (L, D_MODEL, FFN_HIDDEN), D_MODEL),
        "w_up": _w(ks[23], (L, D_MODEL, FFN_HIDDEN), D_MODEL),
        "w_down": _w(ks[24], (L, FFN_HIDDEN, D_MODEL), FFN_HIDDEN),
    }


def reference(x, mem, positions, g_attn, w_in, g_q_lat, g_kv_lat, w_q_up, w_kv_up,
              g_mla_q, g_mla_k, g_mla_out, g_sb_out, w_out, g_cross, g_mem, w_xq, w_xkv,
              g_xq, g_xk, w_xo, g_ffn, w_gate, w_up, w_down):
    B, S = x.shape[:2]
    splits = np.cumsum([MLA_Q_RANK, MLA_KV_RANK, MLA_ROPE, SB_WIDTH, SB_WIDTH]).tolist()
    for l in range(DEPTH):
        n = rmsnorm(x, g_attn[l])
        proj = n @ w_in[l]
        cq, ckv, k_rope, q_sb, k_sb, v_sb = jnp.split(proj, splits, axis=-1)
        o_mla = mla_attention(cq, ckv, k_rope, positions, w_q_up[l], w_kv_up[l],
                              g_q_lat[l], g_kv_lat[l], g_mla_q[l], g_mla_k[l])
        o_sb = stick_breaking_attention(q_sb.reshape(B, S, SB_HEADS, SB_DIM),
                                        k_sb.reshape(B, S, SB_HEADS, SB_DIM),
                                        v_sb.reshape(B, S, SB_HEADS, SB_DIM))
        mixed = jnp.concatenate([rmsnorm(o_mla, g_mla_out[l]), rmsnorm(o_sb, g_sb_out[l])], axis=-1)
        x = x + mixed @ w_out[l]
        x = x + memory_cross_attention(rmsnorm(x, g_cross[l]), mem, w_xq[l], w_xkv[l],
                                       g_mem[l], g_xq[l], g_xk[l], w_xo[l])
        h = rmsnorm(x, g_ffn[l])
        x = x + (jax.nn.silu(h @ w_gate[l]) * (h @ w_up[l])) @ w_down[l]
    return x
```

```python
import functools
import math

import jax
import jax.numpy as jnp
from jax import lax
from jax.experimental import pallas as pl
from jax.experimental.pallas import tpu as pltpu

MLA_HEADS = 16
MLA_NOPE = 128
MLA_ROPE = 64
MLA_V = 128
SB_HEADS = 16
SB_DIM = 128
X_HEADS = 4
X_DIM = 128
ROPE_THETA = 10000.0
EPS = 1e-6

LANE = 128
QK_PAD = 2 * LANE
VMEM_LIMIT_BYTES = 56 * 1024 * 1024
ATTN_TILE = 256
BF16 = jnp.bfloat16
F32 = jnp.float32


def _tile(dim, pref, align):
    if dim <= pref:
        return dim
    t = (pref // align) * align
    while t >= align:
        if dim % t == 0:
            return t
        t -= align
    return dim


def _params(ndims):
    return pltpu.CompilerParams(dimension_semantics=("arbitrary",) * ndims,
                                vmem_limit_bytes=VMEM_LIMIT_BYTES)


def _rms(y, width):
    ms = jnp.sum(y * y, axis=-1, keepdims=True) * (1.0 / width)
    return y * lax.rsqrt(ms + EPS)


def _rope(r, cos, sin):
    half = MLA_ROPE // 2
    lane = lax.broadcasted_iota(jnp.int32, r.shape, 1)
    from_below = pltpu.roll(r, half, 1)
    from_above = pltpu.roll(r, LANE - half, 1)
    return r * cos + jnp.where(lane < half, -from_above, from_below) * sin


def _rmsnorm_body(x_ref, g_ref, o_ref):
    x = x_ref[...]
    o_ref[...] = (_rms(x, x.shape[-1]) * g_ref[...]).astype(o_ref.dtype)


def _rmsnorm(x, g, name):
    m, d = x.shape
    tm = _tile(m, 256, 8)
    return pl.pallas_call(
        _rmsnorm_body, grid=(m // tm,),
        in_specs=[pl.BlockSpec((tm, d), lambda i: (i, 0)), pl.BlockSpec((1, d), lambda i: (0, 0))],
        out_specs=pl.BlockSpec((tm, d), lambda i: (i, 0)),
        out_shape=jax.ShapeDtypeStruct((m, d), BF16),
        compiler_params=_params(1), name=name)(x, g.reshape(1, d))


def _mixnorm_body(a_ref, b_ref, ga_ref, gb_ref, o_ref):
    wa = a_ref.shape[-1]
    a, b = a_ref[...], b_ref[...]
    o_ref[:, :wa] = (_rms(a, wa) * ga_ref[...]).astype(o_ref.dtype)
    o_ref[:, wa:] = (_rms(b, b.shape[-1]) * gb_ref[...]).astype(o_ref.dtype)


def _mixnorm(a, b, ga, gb):
    m, wa = a.shape
    wb = b.shape[1]
    tm = _tile(m, 256, 8)
    return pl.pallas_call(
        _mixnorm_body, grid=(m // tm,),
        in_specs=[pl.BlockSpec((tm, wa), lambda i: (i, 0)), pl.BlockSpec((tm, wb), lambda i: (i, 0)),
                  pl.BlockSpec((1, wa), lambda i: (0, 0)), pl.BlockSpec((1, wb), lambda i: (0, 0))],
        out_specs=pl.BlockSpec((tm, wa + wb), lambda i: (i, 0)),
        out_shape=jax.ShapeDtypeStruct((m, wa + wb), BF16),
        compiler_params=_params(1), name="mixnorm")(a, b, ga.reshape(1, wa), gb.reshape(1, wb))


def _rope_table_body(pos_ref, freq_ref, cos_ref, sin_ref):
    ang = pos_ref[...] * freq_ref[...]
    live = lax.broadcasted_iota(jnp.int32, ang.shape, 1) < MLA_ROPE
    cos_ref[...] = jnp.where(live, jnp.cos(ang), 0.0)
    sin_ref[...] = jnp.where(live, jnp.sin(ang), 0.0)


def _rope_tables(positions):
    t = positions.size
    half = MLA_ROPE // 2
    inv_freq = ROPE_THETA ** (-jnp.arange(half, dtype=F32) / half)
    freq = jnp.concatenate([inv_freq, inv_freq, jnp.zeros((LANE - MLA_ROPE,), F32)]).reshape(1, LANE)
    pos = positions.astype(F32).reshape(t, 1)
    tm = _tile(t, 512, 8)
    return pl.pallas_call(
        _rope_table_body, grid=(t // tm,),
        in_specs=[pl.BlockSpec((tm, 1), lambda i: (i, 0)), pl.BlockSpec((1, LANE), lambda i: (0, 0))],
        out_specs=[pl.BlockSpec((tm, LANE), lambda i: (i, 0))] * 2,
        out_shape=[jax.ShapeDtypeStruct((t, LANE), F32)] * 2,
        compiler_params=_params(1), name="rope_tables")(pos, freq)


def _mm_body(a_ref, w_ref, *rest, n_extra, epilogue):
    extras, outs = rest[:n_extra], rest[n_extra:]
    y = jnp.dot(a_ref[...], w_ref[...], preferred_element_type=F32)
    epilogue(y, extras, outs)


def _matmul(a, w, *, tm, tn, epilogue, out_shape, out_specs, extras=(), extra_specs=(), name):
    m, k = a.shape
    n = w.shape[1]
    body = functools.partial(_mm_body, n_extra=len(extras), epilogue=epilogue)
    return pl.pallas_call(
        body, grid=(m // tm, n // tn),
        in_specs=[pl.BlockSpec((tm, k), lambda i, j: (i, 0)),
                  pl.BlockSpec((k, tn), lambda i, j: (0, j)), *extra_specs],
        out_specs=out_specs, out_shape=out_shape,
        compiler_params=_params(2), name=name)(a, w, *extras)


def _row_spec(tm, width):
    return pl.BlockSpec((tm, width), lambda i, j: (i, 0))


def _const_spec(width):
    return pl.BlockSpec((1, width), lambda i, j: (0, 0))


def _tile_spec(tm, tn):
    return pl.BlockSpec((tm, tn), lambda i, j: (i, j))


def _store_epilogue(y, extras, outs):
    outs[0][...] = y.astype(outs[0].dtype)


def _residual_epilogue(y, extras, outs):
    outs[0][...] = extras[0][...] + y


def _matmul_plain(a, w, out_dtype, name, tm=1024, tn=512):
    m, n = a.shape[0], w.shape[1]
    tm, tn = _tile(m, tm, 8), _tile(n, tn, LANE)
    return _matmul(a, w, tm=tm, tn=tn, epilogue=_store_epilogue,
                   out_shape=jax.ShapeDtypeStruct((m, n), out_dtype),
                   out_specs=_tile_spec(tm, tn), name=name)


def _matmul_residual(a, w, res, name, tm=1024, tn=512):
    m, n = a.shape[0], w.shape[1]
    tm, tn = _tile(m, tm, 8), _tile(n, tn, LANE)
    return _matmul(a, w, tm=tm, tn=tn, epilogue=_residual_epilogue,
                   extras=(res,), extra_specs=(_tile_spec(tm, tn),),
                   out_shape=jax.ShapeDtypeStruct((m, n), F32),
                   out_specs=_tile_spec(tm, tn), name=name)


def _latent_epilogue(y, extras, outs, *, q_rank, kv_rank):
    gq_ref, gkv_ref, gkr_ref, cos_ref, sin_ref = extras
    cq_ref, ckv_ref, kr_ref = outs
    cq_ref[...] = (_rms(y[:, :q_rank], q_rank) * gq_ref[...]).astype(cq_ref.dtype)
    ckv = y[:, q_rank:q_rank + kv_rank]
    ckv_ref[...] = (_rms(ckv, kv_rank) * gkv_ref[...]).astype(ckv_ref.dtype)
    kr = _rms(y[:, q_rank + kv_rank:], MLA_ROPE) * gkr_ref[...]
    kr_ref[...] = _rope(kr, cos_ref[...], sin_ref[...]).astype(kr_ref.dtype)


def _latent_proj(a, w_lat, g_q_lat, g_kv_lat, g_k_rope, cos, sin):
    m = a.shape[0]
    q_rank, kv_rank = g_q_lat.shape[0], g_kv_lat.shape[0]
    n = w_lat.shape[1]
    tm = _tile(m, 512, 8)
    return _matmul(
        a, w_lat, tm=tm, tn=n,
        epilogue=functools.partial(_latent_epilogue, q_rank=q_rank, kv_rank=kv_rank),
        extras=(g_q_lat.reshape(1, -1), g_kv_lat.reshape(1, -1), g_k_rope, cos, sin),
        extra_specs=(_const_spec(q_rank), _const_spec(kv_rank), _const_spec(LANE),
                     _row_spec(tm, LANE), _row_spec(tm, LANE)),
        out_shape=[jax.ShapeDtypeStruct((m, q_rank), BF16), jax.ShapeDtypeStruct((m, kv_rank), BF16),
                   jax.ShapeDtypeStruct((m, LANE), BF16)],
        out_specs=[_row_spec(tm, q_rank), _row_spec(tm, kv_rank), _row_spec(tm, LANE)],
        name="latent_proj")


def _sb_proj_epilogue(y, extras, outs, *, q_tiles, scale):
    j = pl.program_id(1)
    outs[0][...] = (y * jnp.where(j < q_tiles, scale, 1.0)).astype(outs[0].dtype)


def _sb_proj(a, w_sb):
    m, n = a.shape[0], w_sb.shape[1]
    tm, tn = _tile(m, 1024, 8), _tile(n // 3, 512, LANE)
    return _matmul(
        a, w_sb, tm=tm, tn=tn,
        epilogue=functools.partial(_sb_proj_epilogue, q_tiles=(n // 3) // tn, scale=1.0 / math.sqrt(SB_DIM)),
        out_shape=jax.ShapeDtypeStruct((m, n), BF16), out_specs=_tile_spec(tm, tn), name="sb_proj")


def _q_up_epilogue(y, extras, outs, *, heads, scale):
    gn_ref, gr_ref, cos_ref, sin_ref = extras
    o_ref = outs[0]
    cos, sin = cos_ref[...], sin_ref[...]
    for h in range(heads):
        lo = h * QK_PAD
        nope = _rms(y[:, lo:lo + MLA_NOPE], MLA_NOPE) * gn_ref[...]
        o_ref[:, lo:lo + MLA_NOPE] = (nope * scale).astype(o_ref.dtype)
        r = _rms(y[:, lo + MLA_NOPE:lo + QK_PAD], MLA_ROPE) * gr_ref[...]
        o_ref[:, lo + MLA_NOPE:lo + QK_PAD] = (_rope(r, cos, sin) * scale).astype(o_ref.dtype)


def _q_up(cq, w_q, g_nope, g_rope, cos, sin):
    m, n = cq.shape[0], w_q.shape[1]
    tm, heads = _tile(m, 1024, 8), 4
    tn = heads * QK_PAD
    return _matmul(
        cq, w_q, tm=tm, tn=tn,
        epilogue=functools.partial(_q_up_epilogue, heads=heads, scale=1.0 / math.sqrt(MLA_NOPE + MLA_ROPE)),
        extras=(g_nope, g_rope, cos, sin),
        extra_specs=(_const_spec(LANE), _const_spec(LANE), _row_spec(tm, LANE), _row_spec(tm, LANE)),
        out_shape=jax.ShapeDtypeStruct((m, n), BF16), out_specs=_tile_spec(tm, tn), name="q_up")


def _k_up_epilogue(y, extras, outs, *, heads):
    gn_ref, kr_ref = extras
    o_ref = outs[0]
    for h in range(heads):
        k = _rms(y[:, h * MLA_NOPE:(h + 1) * MLA_NOPE], MLA_NOPE) * gn_ref[...]
        o_ref[:, h * QK_PAD:h * QK_PAD + MLA_NOPE] = k.astype(o_ref.dtype)
        o_ref[:, h * QK_PAD + MLA_NOPE:(h + 1) * QK_PAD] = kr_ref[...]


def _k_up(ckv, w_k, g_nope, k_rope):
    m, n = ckv.shape[0], w_k.shape[1]
    tm, heads = _tile(m, 1024, 8), 4
    tn = heads * MLA_NOPE
    return _matmul(
        ckv, w_k, tm=tm, tn=tn, epilogue=functools.partial(_k_up_epilogue, heads=heads),
        extras=(g_nope, k_rope), extra_specs=(_const_spec(LANE), _row_spec(tm, LANE)),
        out_shape=jax.ShapeDtypeStruct((m, (n // MLA_NOPE) * QK_PAD), BF16),
        out_specs=_tile_spec(tm, heads * QK_PAD), name="k_up")


def _headnorm_epilogue(y, extras, outs, *, heads, scale):
    g_ref = extras[0]
    for h in range(heads):
        sl = slice(h * X_DIM, (h + 1) * X_DIM)
        outs[0][:, sl] = (_rms(y[:, sl], X_DIM) * g_ref[...] * scale).astype(outs[0].dtype)


def _xq_proj(a, w_xq, g_xq):
    m, n = a.shape[0], w_xq.shape[1]
    tm = _tile(m, 1024, 8)
    return _matmul(
        a, w_xq, tm=tm, tn=n,
        epilogue=functools.partial(_headnorm_epilogue, heads=n // X_DIM, scale=1.0 / math.sqrt(X_DIM)),
        extras=(g_xq.reshape(1, X_DIM),), extra_specs=(_const_spec(X_DIM),),
        out_shape=jax.ShapeDtypeStruct((m, n), BF16), out_specs=_row_spec(tm, n), name="xq_proj")


def _xkv_epilogue(y, extras, outs, *, heads):
    g_ref = extras[0]
    k_ref, v_ref = outs
    width = heads * X_DIM
    for h in range(heads):
        sl = slice(h * X_DIM, (h + 1) * X_DIM)
        k_ref[:, sl] = (_rms(y[:, sl], X_DIM) * g_ref[...]).astype(k_ref.dtype)
    v_ref[...] = y[:, width:].astype(v_ref.dtype)


def _xkv_proj(a, w_xkv, g_xk):
    m, n = a.shape[0], w_xkv.shape[1]
    tm, width = _tile(m, 512, 8), n // 2
    return _matmul(
        a, w_xkv, tm=tm, tn=n, epilogue=functools.partial(_xkv_epilogue, heads=width // X_DIM),
        extras=(g_xk.reshape(1, X_DIM),), extra_specs=(_const_spec(X_DIM),),
        out_shape=[jax.ShapeDtypeStruct((m, width), BF16)] * 2,
        out_specs=[_row_spec(tm, width)] * 2, name="xkv_proj")


def _swiglu_body(a_ref, wg_ref, wu_ref, o_ref):
    a = a_ref[...]
    g = jnp.dot(a, wg_ref[...], preferred_element_type=F32)
    u = jnp.dot(a, wu_ref[...], preferred_element_type=F32)
    o_ref[...] = (g / (1.0 + jnp.exp(-g)) * u).astype(o_ref.dtype)


def _swiglu(a, w_gate, w_up):
    m, k = a.shape
    n = w_gate.shape[1]
    tm, tn = _tile(m, 1024, 8), _tile(n, 256, LANE)
    return pl.pallas_call(
        _swiglu_body, grid=(m // tm, n // tn),
        in_specs=[pl.BlockSpec((tm, k), lambda i, j: (i, 0)),
                  pl.BlockSpec((k, tn), lambda i, j: (0, j)), pl.BlockSpec((k, tn), lambda i, j: (0, j))],
        out_specs=_tile_spec(tm, tn), out_shape=jax.ShapeDtypeStruct((m, n), BF16),
        compiler_params=_params(2), name="swiglu")(a, w_gate, w_up)


def _dot_nt(a, b):
    return lax.dot_general(a, b, (((1,), (1,)), ((), ())), preferred_element_type=F32)


def _mla_attn_body(q_ref, k_ref, v_ref, o_ref, *, tile):
    n_tiles = q_ref.shape[0] // tile
    row = lax.broadcasted_iota(jnp.int32, (tile, tile), 0)
    col = lax.broadcasted_iota(jnp.int32, (tile, tile), 1)

    def q_tile(qi, _):
        q = q_ref[pl.ds(pl.multiple_of(qi * tile, tile), tile), :]

        def kv_tile(kb, carry, diagonal):
            m, l, acc = carry
            rows = pl.ds(pl.multiple_of(kb * tile, tile), tile)
            s = _dot_nt(q, k_ref[rows, :])
            if diagonal:
                s = jnp.where(col <= row, s, -jnp.inf)
            m_new = jnp.maximum(m, jnp.max(s, axis=1, keepdims=True))
            alpha = jnp.exp(m - m_new)
            p = jnp.exp(s - m_new)
            l = alpha * l + jnp.sum(p, axis=1, keepdims=True)
            acc = alpha * acc + jnp.dot(p.astype(BF16), v_ref[rows, :], preferred_element_type=F32)
            return m_new, l, acc

        init = (jnp.full((tile, 1), -jnp.inf, F32), jnp.zeros((tile, 1), F32),
                jnp.zeros((tile, v_ref.shape[1]), F32))
        carry = lax.fori_loop(0, qi, functools.partial(kv_tile, diagonal=False), init)
        _, l, acc = kv_tile(qi, carry, True)
        o_ref[pl.ds(pl.multiple_of(qi * tile, tile), tile), :] = acc / l
        return 0

    lax.fori_loop(0, n_tiles, q_tile, 0)


def _mla_attention(q, k, v, batch):
    t = q.shape[0]
    seq = t // batch
    heads = q.shape[1] // QK_PAD
    tile = _tile(seq, ATTN_TILE, 8)
    return pl.pallas_call(
        functools.partial(_mla_attn_body, tile=tile), grid=(batch, heads),
        in_specs=[pl.BlockSpec((seq, QK_PAD), lambda b, h: (b, h)),
                  pl.BlockSpec((seq, QK_PAD), lambda b, h: (b, h)),
                  pl.BlockSpec((seq, MLA_V), lambda b, h: (b, h))],
        out_specs=pl.BlockSpec((seq, MLA_V), lambda b, h: (b, h)),
        out_shape=jax.ShapeDtypeStruct((t, heads * MLA_V), F32),
        compiler_params=_params(2), name="mla_attention")(q, k, v)


def _sb_attn_body(q_ref, k_ref, v_ref, o_ref, *, tile):
    n_tiles = q_ref.shape[0] // tile
    row = lax.broadcasted_iota(jnp.int32, (tile, tile), 0)
    col = lax.broadcasted_iota(jnp.int32, (tile, tile), 1)
    strict = col < row
    suffix = jnp.where(row > col, 1.0, 0.0).astype(BF16)

    def q_tile(qi, _):
        q = q_ref[pl.ds(pl.multiple_of(qi * tile, tile), tile), :]

        def kv_tile(kb, carry, diagonal):
            run, acc = carry
            rows = pl.ds(pl.multiple_of(kb * tile, tile), tile)
            z = _dot_nt(q, k_ref[rows, :])
            soft = jnp.log1p(jnp.exp(-jnp.abs(z)))
            log_beta = jnp.minimum(z, 0.0) - soft
            log_keep = log_beta - z
            if diagonal:
                log_keep = jnp.where(strict, log_keep, 0.0)
            hi = log_keep.astype(BF16)
            lo = (log_keep - hi.astype(F32)).astype(BF16)
            rest = (jnp.dot(hi, suffix, preferred_element_type=F32)
                    + jnp.dot(lo, suffix, preferred_element_type=F32))
            a = jnp.exp(log_beta + rest + run)
            if diagonal:
                a = jnp.where(strict, a, 0.0)
            acc = acc + jnp.dot(a.astype(BF16), v_ref[rows, :], preferred_element_type=F32)
            run = run + jnp.sum(log_keep, axis=1, keepdims=True)
            return run, acc

        init = (jnp.zeros((tile, 1), F32), jnp.zeros((tile, v_ref.shape[1]), F32))
        carry = kv_tile(qi, init, True)

        def left(step, carry):
            return kv_tile(qi - 1 - step, carry, False)

        _, acc = lax.fori_loop(0, qi, left, carry)
        o_ref[pl.ds(pl.multiple_of(qi * tile, tile), tile), :] = acc
        return 0

    lax.fori_loop(0, n_tiles, q_tile, 0)


def _sb_attention(qkv, batch):
    t = qkv.shape[0]
    seq = t // batch
    heads = qkv.shape[1] // (3 * SB_DIM)
    tile = _tile(seq, ATTN_TILE, 8)
    return pl.pallas_call(
        functools.partial(_sb_attn_body, tile=tile), grid=(batch, heads),
        in_specs=[pl.BlockSpec((seq, SB_DIM), lambda b, h: (b, h)),
                  pl.BlockSpec((seq, SB_DIM), lambda b, h: (b, heads + h)),
                  pl.BlockSpec((seq, SB_DIM), lambda b, h: (b, 2 * heads + h))],
        out_specs=pl.BlockSpec((seq, SB_DIM), lambda b, h: (b, h)),
        out_shape=jax.ShapeDtypeStruct((t, heads * SB_DIM), F32),
        compiler_params=_params(2), name="sb_attention")(qkv, qkv, qkv)


def _cross_attn_body(q_ref, k_ref, v_ref, o_ref, *, heads):
    for h in range(heads):
        sl = slice(h * X_DIM, (h + 1) * X_DIM)
        s = _dot_nt(q_ref[:, sl], k_ref[:, sl])
        p = jnp.exp(s - jnp.max(s, axis=1, keepdims=True))
        o = jnp.dot(p.astype(BF16), v_ref[:, sl], preferred_element_type=F32)
        o_ref[:, sl] = (o / jnp.sum(p, axis=1, keepdims=True)).astype(o_ref.dtype)


def _cross_attention(q, k, v, batch):
    t, width = q.shape
    seq, mem_len = t // batch, k.shape[0] // batch
    tq = _tile(seq, 512, 8)
    n_q = seq // tq
    return pl.pallas_call(
        functools.partial(_cross_attn_body, heads=width // X_DIM), grid=(batch, n_q),
        in_specs=[pl.BlockSpec((tq, width), lambda b, i: (b * n_q + i, 0)),
                  pl.BlockSpec((mem_len, width), lambda b, i: (b, 0)),
                  pl.BlockSpec((mem_len, width), lambda b, i: (b, 0))],
        out_specs=pl.BlockSpec((tq, width), lambda b, i: (b * n_q + i, 0)),
        out_shape=jax.ShapeDtypeStruct((t, width), BF16),
        compiler_params=_params(2), name="cross_attention")(q, k, v)


def _pad_lanes(g, width):
    return jnp.pad(g, (0, width - g.shape[0])).reshape(1, width)


def _layer_weights(w_in, w_q_up, w_kv_up, w_out, w_xq, w_xkv, w_xo, w_gate, w_up, w_down, q_rank, kv_rank):
    lat = q_rank + kv_rank + MLA_ROPE
    lat_pad = -(-lat // LANE) * LANE
    d = w_in.shape[0]
    w_lat = jnp.pad(w_in[:, :lat], ((0, 0), (0, lat_pad - lat)))
    w_q = jnp.pad(w_q_up.reshape(q_rank, MLA_HEADS, MLA_NOPE + MLA_ROPE),
                  ((0, 0), (0, 0), (0, QK_PAD - MLA_NOPE - MLA_ROPE))).reshape(q_rank, MLA_HEADS * QK_PAD)
    w_kv = w_kv_up.reshape(kv_rank, MLA_HEADS, MLA_NOPE + MLA_V)
    w_x = w_xkv.reshape(d, X_HEADS, 2 * X_DIM)
    w_xkv2 = jnp.concatenate([w_x[:, :, :X_DIM].reshape(d, -1), w_x[:, :, X_DIM:].reshape(d, -1)], axis=1)
    ws = dict(lat=w_lat, sb=w_in[:, lat:], q=w_q,
              k=w_kv[:, :, :MLA_NOPE].reshape(kv_rank, -1), v=w_kv[:, :, MLA_NOPE:].reshape(kv_rank, -1),
              out=w_out, xq=w_xq, xkv=w_xkv2, xo=w_xo, gate=w_gate, up=w_up, down=w_down)
    return {name: w.astype(BF16) for name, w in ws.items()}


def kernel(x, mem, positions, g_attn, w_in, g_q_lat, g_kv_lat, w_q_up, w_kv_up, g_mla_q, g_mla_k,
           g_mla_out, g_sb_out, w_out, g_cross, g_mem, w_xq, w_xkv, g_xq, g_xk, w_xo, g_ffn,
           w_gate, w_up, w_down):
    batch, seq, d = x.shape
    depth = w_in.shape[0]
    q_rank, kv_rank = g_q_lat.shape[1], g_kv_lat.shape[1]
    x = x.reshape(batch * seq, d)
    mem2 = mem.reshape(-1, d)
    cos, sin = _rope_tables(positions)
    for l in range(depth):
        w = _layer_weights(w_in[l], w_q_up[l], w_kv_up[l], w_out[l], w_xq[l], w_xkv[l], w_xo[l],
                           w_gate[l], w_up[l], w_down[l], q_rank, kv_rank)
        n = _rmsnorm(x, g_attn[l], "norm_attn")
        cq, ckv, k_rope = _latent_proj(n, w["lat"], g_q_lat[l], g_kv_lat[l],
                                       _pad_lanes(g_mla_k[l, MLA_NOPE:], LANE), cos, sin)
        q = _q_up(cq, w["q"], g_mla_q[l, :MLA_NOPE].reshape(1, LANE),
                  _pad_lanes(g_mla_q[l, MLA_NOPE:], LANE), cos, sin)
        k = _k_up(ckv, w["k"], g_mla_k[l, :MLA_NOPE].reshape(1, LANE), k_rope)
        v = _matmul_plain(ckv, w["v"], BF16, "v_up")
        o_mla = _mla_attention(q, k, v, batch)
        o_sb = _sb_attention(_sb_proj(n, w["sb"]), batch)
        mixed = _mixnorm(o_mla, o_sb, g_mla_out[l], g_sb_out[l])
        x = _matmul_residual(mixed, w["out"], x, "out_proj")
        h = _rmsnorm(x, g_cross[l], "norm_cross")
        xq = _xq_proj(h, w["xq"], g_xq[l])
        xk, xv = _xkv_proj(_rmsnorm(mem2, g_mem[l], "norm_mem"), w["xkv"], g_xk[l])
        x = _matmul_residual(_cross_attention(xq, xk, xv, batch), w["xo"], x, "cross_out")
        h = _rmsnorm(x, g_ffn[l], "norm_ffn")
        x = _matmul_residual(_swiglu(h, w["gate"], w["up"]), w["down"], x, "ffn_down", tm=512)
    return x.reshape(batch, seq, d)
```

```python
import functools
import math

import jax
import jax.numpy as jnp
from jax import lax
from jax.experimental import pallas as pl
from jax.experimental.pallas import tpu as pltpu

MLA_HEADS = 16
MLA_NOPE = 128
MLA_ROPE = 64
MLA_V = 128
SB_HEADS = 16
SB_DIM = 128
X_HEADS = 4
X_DIM = 128
ROPE_THETA = 10000.0
EPS = 1e-6

LANE = 128
QK_PAD = 2 * LANE
VMEM_LIMIT_BYTES = 56 * 1024 * 1024
ATTN_TILE = 256
LOG2_E = math.log2(math.e)
BF16 = jnp.bfloat16
F32 = jnp.float32


def _tile(dim, pref, align):
    if dim <= pref:
        return dim
    t = (pref // align) * align
    while t >= align:
        if dim % t == 0:
            return t
        t -= align
    return dim


def _params(ndims):
    return pltpu.CompilerParams(dimension_semantics=("arbitrary",) * ndims,
                                vmem_limit_bytes=VMEM_LIMIT_BYTES)


def _rms(y, width):
    ms = jnp.sum(y * y, axis=-1, keepdims=True) * (1.0 / width)
    return y * lax.rsqrt(ms + EPS)


def _rope(r, cos, sin):
    half = MLA_ROPE // 2
    lane = lax.broadcasted_iota(jnp.int32, r.shape, 1)
    from_below = pltpu.roll(r, half, 1)
    from_above = pltpu.roll(r, LANE - half, 1)
    return r * cos + jnp.where(lane < half, -from_above, from_below) * sin


def _rmsnorm_body(x_ref, g_ref, o_ref):
    x = x_ref[...]
    o_ref[...] = (_rms(x, x.shape[-1]) * g_ref[...]).astype(o_ref.dtype)


def _rmsnorm(x, g, name):
    m, d = x.shape
    tm = _tile(m, 256, 8)
    return pl.pallas_call(
        _rmsnorm_body, grid=(m // tm,),
        in_specs=[pl.BlockSpec((tm, d), lambda i: (i, 0)), pl.BlockSpec((1, d), lambda i: (0, 0))],
        out_specs=pl.BlockSpec((tm, d), lambda i: (i, 0)),
        out_shape=jax.ShapeDtypeStruct((m, d), BF16),
        compiler_params=_params(1), name=name)(x, g.reshape(1, d))


def _mixnorm_body(a_ref, b_ref, ga_ref, gb_ref, o_ref):
    wa = a_ref.shape[-1]
    a, b = a_ref[...], b_ref[...]
    o_ref[:, :wa] = (_rms(a, wa) * ga_ref[...]).astype(o_ref.dtype)
    o_ref[:, wa:] = (_rms(b, b.shape[-1]) * gb_ref[...]).astype(o_ref.dtype)


def _mixnorm(a, b, ga, gb):
    m, wa = a.shape
    wb = b.shape[1]
    tm = _tile(m, 256, 8)
    return pl.pallas_call(
        _mixnorm_body, grid=(m // tm,),
        in_specs=[pl.BlockSpec((tm, wa), lambda i: (i, 0)), pl.BlockSpec((tm, wb), lambda i: (i, 0)),
                  pl.BlockSpec((1, wa), lambda i: (0, 0)), pl.BlockSpec((1, wb), lambda i: (0, 0))],
        out_specs=pl.BlockSpec((tm, wa + wb), lambda i: (i, 0)),
        out_shape=jax.ShapeDtypeStruct((m, wa + wb), BF16),
        compiler_params=_params(1), name="mixnorm")(a, b, ga.reshape(1, wa), gb.reshape(1, wb))


def _rope_table_body(pos_ref, freq_ref, cos_ref, sin_ref):
    ang = pos_ref[...] * freq_ref[...]
    live = lax.broadcasted_iota(jnp.int32, ang.shape, 1) < MLA_ROPE
    cos_ref[...] = jnp.where(live, jnp.cos(ang), 0.0)
    sin_ref[...] = jnp.where(live, jnp.sin(ang), 0.0)


def _rope_tables(positions):
    t = positions.size
    half = MLA_ROPE // 2
    inv_freq = ROPE_THETA ** (-jnp.arange(half, dtype=F32) / half)
    freq = jnp.concatenate([inv_freq, inv_freq, jnp.zeros((LANE - MLA_ROPE,), F32)]).reshape(1, LANE)
    pos = positions.astype(F32).reshape(t, 1)
    tm = _tile(t, 512, 8)
    return pl.pallas_call(
        _rope_table_body, grid=(t // tm,),
        in_specs=[pl.BlockSpec((tm, 1), lambda i: (i, 0)), pl.BlockSpec((1, LANE), lambda i: (0, 0))],
        out_specs=[pl.BlockSpec((tm, LANE), lambda i: (i, 0))] * 2,
        out_shape=[jax.ShapeDtypeStruct((t, LANE), F32)] * 2,
        compiler_params=_params(1), name="rope_tables")(pos, freq)


def _mm_body(a_ref, w_ref, *rest, n_extra, epilogue):
    extras, outs = rest[:n_extra], rest[n_extra:]
    y = jnp.dot(a_ref[...], w_ref[...].astype(BF16), preferred_element_type=F32)
    epilogue(y, extras, outs)


def _matmul(a, w, *, tm, tn, epilogue, out_shape, out_specs, extras=(), extra_specs=(), name):
    m, k = a.shape
    n = w.shape[1]
    body = functools.partial(_mm_body, n_extra=len(extras), epilogue=epilogue)
    return pl.pallas_call(
        body, grid=(m // tm, n // tn),
        in_specs=[pl.BlockSpec((tm, k), lambda i, j: (i, 0)),
                  pl.BlockSpec((k, tn), lambda i, j: (0, j)), *extra_specs],
        out_specs=out_specs, out_shape=out_shape,
        compiler_params=_params(2), name=name)(a, w, *extras)


def _row_spec(tm, width):
    return pl.BlockSpec((tm, width), lambda i, j: (i, 0))


def _const_spec(width):
    return pl.BlockSpec((1, width), lambda i, j: (0, 0))


def _tile_spec(tm, tn):
    return pl.BlockSpec((tm, tn), lambda i, j: (i, j))


def _store_epilogue(y, extras, outs):
    outs[0][...] = y.astype(outs[0].dtype)


def _residual_epilogue(y, extras, outs):
    outs[0][...] = extras[0][...] + y


def _matmul_plain(a, w, out_dtype, name, tm=1024, tn=512):
    m, n = a.shape[0], w.shape[1]
    tm, tn = _tile(m, tm, 8), _tile(n, tn, LANE)
    return _matmul(a, w, tm=tm, tn=tn, epilogue=_store_epilogue,
                   out_shape=jax.ShapeDtypeStruct((m, n), out_dtype),
                   out_specs=_tile_spec(tm, tn), name=name)


def _matmul_residual(a, w, res, name, tm=1024, tn=512):
    m, n = a.shape[0], w.shape[1]
    tm, tn = _tile(m, tm, 8), _tile(n, tn, LANE)
    return _matmul(a, w, tm=tm, tn=tn, epilogue=_residual_epilogue,
                   extras=(res,), extra_specs=(_tile_spec(tm, tn),),
                   out_shape=jax.ShapeDtypeStruct((m, n), F32),
                   out_specs=_tile_spec(tm, tn), name=name)


def _latent_epilogue(y, extras, outs, *, q_rank, kv_rank):
    gq_ref, gkv_ref, gkr_ref, cos_ref, sin_ref = extras
    cq_ref, ckv_ref, kr_ref = outs
    cq_ref[...] = (_rms(y[:, :q_rank], q_rank) * gq_ref[...]).astype(cq_ref.dtype)
    ckv = y[:, q_rank:q_rank + kv_rank]
    ckv_ref[...] = (_rms(ckv, kv_rank) * gkv_ref[...]).astype(ckv_ref.dtype)
    kr = _rms(y[:, q_rank + kv_rank:], MLA_ROPE) * gkr_ref[...]
    kr_ref[...] = _rope(kr, cos_ref[...], sin_ref[...]).astype(kr_ref.dtype)


def _latent_proj(a, w_lat, g_q_lat, g_kv_lat, g_k_rope, cos, sin):
    m = a.shape[0]
    q_rank, kv_rank = g_q_lat.shape[0], g_kv_lat.shape[0]
    n = w_lat.shape[1]
    tm = _tile(m, 512, 8)
    return _matmul(
        a, w_lat, tm=tm, tn=n,
        epilogue=functools.partial(_latent_epilogue, q_rank=q_rank, kv_rank=kv_rank),
        extras=(g_q_lat.reshape(1, -1), g_kv_lat.reshape(1, -1), g_k_rope, cos, sin),
        extra_specs=(_const_spec(q_rank), _const_spec(kv_rank), _const_spec(LANE),
                     _row_spec(tm, LANE), _row_spec(tm, LANE)),
        out_shape=[jax.ShapeDtypeStruct((m, q_rank), BF16), jax.ShapeDtypeStruct((m, kv_rank), BF16),
                   jax.ShapeDtypeStruct((m, LANE), BF16)],
        out_specs=[_row_spec(tm, q_rank), _row_spec(tm, kv_rank), _row_spec(tm, LANE)],
        name="latent_proj")


def _sb_proj_epilogue(y, extras, outs, *, q_tiles, scale):
    j = pl.program_id(1)
    outs[0][...] = (y * jnp.where(j < q_tiles, scale, 1.0)).astype(outs[0].dtype)


def _sb_proj(a, w_sb):
    m, n = a.shape[0], w_sb.shape[1]
    tm, tn = _tile(m, 1024, 8), _tile(n // 3, 512, LANE)
    return _matmul(
        a, w_sb, tm=tm, tn=tn,
        epilogue=functools.partial(_sb_proj_epilogue, q_tiles=(n // 3) // tn, scale=LOG2_E / math.sqrt(SB_DIM)),
        out_shape=jax.ShapeDtypeStruct((m, n), BF16), out_specs=_tile_spec(tm, tn), name="sb_proj")


def _q_up_epilogue(y, extras, outs, *, heads, scale):
    gn_ref, gr_ref, cos_ref, sin_ref = extras
    o_ref = outs[0]
    cos, sin = cos_ref[...], sin_ref[...]
    for h in range(heads):
        lo = h * QK_PAD
        nope = _rms(y[:, lo:lo + MLA_NOPE], MLA_NOPE) * gn_ref[...]
        o_ref[:, lo:lo + MLA_NOPE] = (nope * scale).astype(o_ref.dtype)
        r = _rms(y[:, lo + MLA_NOPE:lo + QK_PAD], MLA_ROPE) * gr_ref[...]
        o_ref[:, lo + MLA_NOPE:lo + QK_PAD] = (_rope(r, cos, sin) * scale).astype(o_ref.dtype)


def _q_up(cq, w_q, g_nope, g_rope, cos, sin):
    m, n = cq.shape[0], w_q.shape[1]
    tm, heads = _tile(m, 1024, 8), 4
    tn = heads * QK_PAD
    return _matmul(
        cq, w_q, tm=tm, tn=tn,
        epilogue=functools.partial(_q_up_epilogue, heads=heads, scale=LOG2_E / math.sqrt(MLA_NOPE + MLA_ROPE)),
        extras=(g_nope, g_rope, cos, sin),
        extra_specs=(_const_spec(LANE), _const_spec(LANE), _row_spec(tm, LANE), _row_spec(tm, LANE)),
        out_shape=jax.ShapeDtypeStruct((m, n), BF16), out_specs=_tile_spec(tm, tn), name="q_up")


def _k_up_epilogue(y, extras, outs, *, heads):
    gn_ref, kr_ref = extras
    o_ref = outs[0]
    for h in range(heads):
        k = _rms(y[:, h * MLA_NOPE:(h + 1) * MLA_NOPE], MLA_NOPE) * gn_ref[...]
        o_ref[:, h * QK_PAD:h * QK_PAD + MLA_NOPE] = k.astype(o_ref.dtype)
        o_ref[:, h * QK_PAD + MLA_NOPE:(h + 1) * QK_PAD] = kr_ref[...]


def _k_up(ckv, w_k, g_nope, k_rope):
    m, n = ckv.shape[0], w_k.shape[1]
    tm, heads = _tile(m, 1024, 8), 4
    tn = heads * MLA_NOPE
    return _matmul(
        ckv, w_k, tm=tm, tn=tn, epilogue=functools.partial(_k_up_epilogue, heads=heads),
        extras=(g_nope, k_rope), extra_specs=(_const_spec(LANE), _row_spec(tm, LANE)),
        out_shape=jax.ShapeDtypeStruct((m, (n // MLA_NOPE) * QK_PAD), BF16),
        out_specs=_tile_spec(tm, heads * QK_PAD), name="k_up")


def _headnorm_epilogue(y, extras, outs, *, heads, scale):
    g_ref = extras[0]
    for h in range(heads):
        sl = slice(h * X_DIM, (h + 1) * X_DIM)
        outs[0][:, sl] = (_rms(y[:, sl], X_DIM) * g_ref[...] * scale).astype(outs[0].dtype)


def _xq_proj(a, w_xq, g_xq):
    m, n = a.shape[0], w_xq.shape[1]
    tm = _tile(m, 1024, 8)
    return _matmul(
        a, w_xq, tm=tm, tn=n,
        epilogue=functools.partial(_headnorm_epilogue, heads=n // X_DIM, scale=1.0 / math.sqrt(X_DIM)),
        extras=(g_xq.reshape(1, X_DIM),), extra_specs=(_const_spec(X_DIM),),
        out_shape=jax.ShapeDtypeStruct((m, n), BF16), out_specs=_row_spec(tm, n), name="xq_proj")


def _xkv_epilogue(y, extras, outs, *, heads):
    g_ref = extras[0]
    k_ref, v_ref = outs
    width = heads * X_DIM
    for h in range(heads):
        sl = slice(h * X_DIM, (h + 1) * X_DIM)
        k_ref[:, sl] = (_rms(y[:, sl], X_DIM) * g_ref[...]).astype(k_ref.dtype)
    v_ref[...] = y[:, width:].astype(v_ref.dtype)


def _xkv_proj(a, w_xkv, g_xk):
    m, n = a.shape[0], w_xkv.shape[1]
    tm, width = _tile(m, 512, 8), n // 2
    return _matmul(
        a, w_xkv, tm=tm, tn=n, epilogue=functools.partial(_xkv_epilogue, heads=width // X_DIM),
        extras=(g_xk.reshape(1, X_DIM),), extra_specs=(_const_spec(X_DIM),),
        out_shape=[jax.ShapeDtypeStruct((m, width), BF16)] * 2,
        out_specs=[_row_spec(tm, width)] * 2, name="xkv_proj")


def _swiglu_body(a_ref, wg_ref, wu_ref, o_ref):
    a = a_ref[...]
    g = jnp.dot(a, wg_ref[...].astype(BF16), preferred_element_type=F32)
    u = jnp.dot(a, wu_ref[...].astype(BF16), preferred_element_type=F32)
    o_ref[...] = (g / (1.0 + jnp.exp(-g)) * u).astype(o_ref.dtype)


def _swiglu(a, w_gate, w_up):
    m, k = a.shape
    n = w_gate.shape[1]
    tm, tn = _tile(m, 1024, 8), _tile(n, 256, LANE)
    return pl.pallas_call(
        _swiglu_body, grid=(m // tm, n // tn),
        in_specs=[pl.BlockSpec((tm, k), lambda i, j: (i, 0)),
                  pl.BlockSpec((k, tn), lambda i, j: (0, j)), pl.BlockSpec((k, tn), lambda i, j: (0, j))],
        out_specs=_tile_spec(tm, tn), out_shape=jax.ShapeDtypeStruct((m, n), BF16),
        compiler_params=_params(2), name="swiglu")(a, w_gate, w_up)


def _dot_nt(a, b):
    return lax.dot_general(a, b, (((1,), (1,)), ((), ())), preferred_element_type=F32)


def _mla_attn_body(q_ref, k_ref, v_ref, o_ref, *, tile):
    n_tiles = q_ref.shape[0] // tile
    row = lax.broadcasted_iota(jnp.int32, (tile, tile), 0)
    col = lax.broadcasted_iota(jnp.int32, (tile, tile), 1)
    for qi in range(n_tiles):
        lo, hi = qi * tile, (qi + 1) * tile
        q = q_ref[lo:hi, :]
        s_diag = jnp.where(col <= row, _dot_nt(q, k_ref[lo:hi, :]), -jnp.inf)
        m = jnp.max(s_diag, axis=1, keepdims=True)
        if qi:
            s_off = _dot_nt(q, k_ref[0:lo, :])
            m = jnp.maximum(m, jnp.max(s_off, axis=1, keepdims=True))
        p_diag = jnp.exp2(s_diag - m)
        l = jnp.sum(p_diag, axis=1, keepdims=True)
        acc = jnp.dot(p_diag.astype(BF16), v_ref[lo:hi, :], preferred_element_type=F32)
        if qi:
            p_off = jnp.exp2(s_off - m)
            l = l + jnp.sum(p_off, axis=1, keepdims=True)
            acc = acc + jnp.dot(p_off.astype(BF16), v_ref[0:lo, :], preferred_element_type=F32)
        o_ref[lo:hi, :] = acc / l


def _mla_attention(q, k, v, batch):
    t = q.shape[0]
    seq = t // batch
    heads = q.shape[1] // QK_PAD
    tile = _tile(seq, ATTN_TILE, 8)
    return pl.pallas_call(
        functools.partial(_mla_attn_body, tile=tile), grid=(batch, heads),
        in_specs=[pl.BlockSpec((seq, QK_PAD), lambda b, h: (b, h)),
                  pl.BlockSpec((seq, QK_PAD), lambda b, h: (b, h)),
                  pl.BlockSpec((seq, MLA_V), lambda b, h: (b, h))],
        out_specs=pl.BlockSpec((seq, MLA_V), lambda b, h: (b, h)),
        out_shape=jax.ShapeDtypeStruct((t, heads * MLA_V), F32),
        compiler_params=_params(2), name="mla_attention")(q, k, v)


def _sb_attn_body(q_ref, k_ref, v_ref, o_ref, *, tile):
    n_tiles = q_ref.shape[0] // tile
    row = lax.broadcasted_iota(jnp.int32, (tile, tile), 0)
    col = lax.broadcasted_iota(jnp.int32, (tile, tile), 1)
    strict = col < row
    ones_below = jnp.where(row > col, 1.0, 0.0).astype(BF16)
    suffix = jnp.concatenate([ones_below, ones_below], axis=0)

    def logs(z):
        log_beta = jnp.minimum(z, 0.0) - jnp.log2(1.0 + jnp.exp2(-jnp.abs(z)))
        return log_beta, log_beta - z

    def suffix_sums(log_keep):
        hi = log_keep.astype(BF16)
        lo = (log_keep - hi.astype(F32)).astype(BF16)
        return jnp.dot(jnp.concatenate([hi, lo], axis=1), suffix, preferred_element_type=F32)

    for qi in range(n_tiles):
        lo_row, hi_row = qi * tile, (qi + 1) * tile
        q = q_ref[lo_row:hi_row, :]
        log_beta, log_keep = logs(_dot_nt(q, k_ref[lo_row:hi_row, :]))
        log_keep = jnp.where(strict, log_keep, 0.0)
        a = jnp.where(strict, jnp.exp2(log_beta + suffix_sums(log_keep)), 0.0)
        acc = jnp.dot(a.astype(BF16), v_ref[lo_row:hi_row, :], preferred_element_type=F32)
        if qi:
            z = jnp.concatenate([_dot_nt(q, k_ref[c * tile:(c + 1) * tile, :]) for c in range(qi)], axis=0)
            log_beta, keep = logs(z)
            totals = jnp.sum(keep, axis=1, keepdims=True)
            run = jnp.sum(log_keep, axis=1, keepdims=True)
            carries = [None] * qi
            for c in reversed(range(qi)):
                carries[c] = run
                run = run + totals[c * tile:(c + 1) * tile]
            a = jnp.exp2(log_beta + suffix_sums(keep) + jnp.concatenate(carries, axis=0)).astype(BF16)
            a = jnp.concatenate([a[c * tile:(c + 1) * tile] for c in range(qi)], axis=1)
            acc = acc + jnp.dot(a, v_ref[0:lo_row, :], preferred_element_type=F32)
        o_ref[lo_row:hi_row, :] = acc


def _sb_attention(qkv, batch):
    t = qkv.shape[0]
    seq = t // batch
    heads = qkv.shape[1] // (3 * SB_DIM)
    tile = _tile(seq, ATTN_TILE, 8)
    return pl.pallas_call(
        functools.partial(_sb_attn_body, tile=tile), grid=(batch, heads),
        in_specs=[pl.BlockSpec((seq, SB_DIM), lambda b, h: (b, h)),
                  pl.BlockSpec((seq, SB_DIM), lambda b, h: (b, heads + h)),
                  pl.BlockSpec((seq, SB_DIM), lambda b, h: (b, 2 * heads + h))],
        out_specs=pl.BlockSpec((seq, SB_DIM), lambda b, h: (b, h)),
        out_shape=jax.ShapeDtypeStruct((t, heads * SB_DIM), F32),
        compiler_params=_params(2), name="sb_attention")(qkv, qkv, qkv)


def _cross_attn_body(q_ref, k_ref, v_ref, o_ref, *, heads):
    for h in range(heads):
        sl = slice(h * X_DIM, (h + 1) * X_DIM)
        s = _dot_nt(q_ref[:, sl], k_ref[:, sl])
        p = jnp.exp(s - jnp.max(s, axis=1, keepdims=True))
        o = jnp.dot(p.astype(BF16), v_ref[:, sl], preferred_element_type=F32)
        o_ref[:, sl] = (o / jnp.sum(p, axis=1, keepdims=True)).astype(o_ref.dtype)


def _cross_attention(q, k, v, batch):
    t, width = q.shape
    seq, mem_len = t // batch, k.shape[0] // batch
    tq = _tile(seq, 512, 8)
    n_q = seq // tq
    return pl.pallas_call(
        functools.partial(_cross_attn_body, heads=width // X_DIM), grid=(batch, n_q),
        in_specs=[pl.BlockSpec((tq, width), lambda b, i: (b * n_q + i, 0)),
                  pl.BlockSpec((mem_len, width), lambda b, i: (b, 0)),
                  pl.BlockSpec((mem_len, width), lambda b, i: (b, 0))],
        out_specs=pl.BlockSpec((tq, width), lambda b, i: (b * n_q + i, 0)),
        out_shape=jax.ShapeDtypeStruct((t, width), BF16),
        compiler_params=_params(2), name="cross_attention")(q, k, v)


def _pad_lanes(g, width):
    return jnp.pad(g, (0, width - g.shape[0])).reshape(1, width)


def _layer_weights(w_in, w_q_up, w_kv_up, w_xkv, w_down, q_rank, kv_rank):
    lat = q_rank + kv_rank + MLA_ROPE
    lat_pad = -(-lat // LANE) * LANE
    d = w_in.shape[0]
    w_lat = jnp.pad(w_in[:, :lat], ((0, 0), (0, lat_pad - lat)))
    w_q = jnp.pad(w_q_up.reshape(q_rank, MLA_HEADS, MLA_NOPE + MLA_ROPE),
                  ((0, 0), (0, 0), (0, QK_PAD - MLA_NOPE - MLA_ROPE))).reshape(q_rank, MLA_HEADS * QK_PAD)
    w_kv = w_kv_up.reshape(kv_rank, MLA_HEADS, MLA_NOPE + MLA_V)
    w_x = w_xkv.reshape(d, X_HEADS, 2 * X_DIM)
    w_xkv2 = jnp.concatenate([w_x[:, :, :X_DIM].reshape(d, -1), w_x[:, :, X_DIM:].reshape(d, -1)], axis=1)
    ws = dict(lat=w_lat, sb=w_in[:, lat:], q=w_q,
              k=w_kv[:, :, :MLA_NOPE].reshape(kv_rank, -1), v=w_kv[:, :, MLA_NOPE:].reshape(kv_rank, -1),
              xkv=w_xkv2, down=w_down)
    return {name: w.astype(BF16) for name, w in ws.items()}


def kernel(x, mem, positions, g_attn, w_in, g_q_lat, g_kv_lat, w_q_up, w_kv_up, g_mla_q, g_mla_k,
           g_mla_out, g_sb_out, w_out, g_cross, g_mem, w_xq, w_xkv, g_xq, g_xk, w_xo, g_ffn,
           w_gate, w_up, w_down):
    batch, seq, d = x.shape
    depth = w_in.shape[0]
    q_rank, kv_rank = g_q_lat.shape[1], g_kv_lat.shape[1]
    x = x.reshape(batch * seq, d)
    mem2 = mem.reshape(-1, d)
    cos, sin = _rope_tables(positions)
    for l in range(depth):
        w = _layer_weights(w_in[l], w_q_up[l], w_kv_up[l], w_xkv[l], w_down[l], q_rank, kv_rank)
        n = _rmsnorm(x, g_attn[l], "norm_attn")
        cq, ckv, k_rope = _latent_proj(n, w["lat"], g_q_lat[l], g_kv_lat[l],
                                       _pad_lanes(g_mla_k[l, MLA_NOPE:], LANE), cos, sin)
        q = _q_up(cq, w["q"], g_mla_q[l, :MLA_NOPE].reshape(1, LANE),
                  _pad_lanes(g_mla_q[l, MLA_NOPE:], LANE), cos, sin)
        k = _k_up(ckv, w["k"], g_mla_k[l, :MLA_NOPE].reshape(1, LANE), k_rope)
        v = _matmul_plain(ckv, w["v"], BF16, "v_up")
        o_mla = _mla_attention(q, k, v, batch)
        o_sb = _sb_attention(_sb_proj(n, w["sb"]), batch)
        mixed = _mixnorm(o_mla, o_sb, g_mla_out[l], g_sb_out[l])
        x = _matmul_residual(mixed, w_out[l], x, "out_proj")
        h = _rmsnorm(x, g_cross[l], "norm_cross")
        xq = _xq_proj(h, w_xq[l], g_xq[l])
        xk, xv = _xkv_proj(_rmsnorm(mem2, g_mem[l], "norm_mem"), w["xkv"], g_xk[l])
        x = _matmul_residual(_cross_attention(xq, xk, xv, batch), w_xo[l], x, "cross_out")
        h = _rmsnorm(x, g_ffn[l], "norm_ffn")
        x = _matmul_residual(_swiglu(h, w_gate[l], w_up[l]), w["down"], x, "ffn_down", tm=512)
    return x.reshape(batch, seq, d)
```

```python
import functools
import math

import jax
import jax.numpy as jnp
from jax import lax
from jax.experimental import pallas as pl
from jax.experimental.pallas import tpu as pltpu

MLA_HEADS = 16
MLA_NOPE = 128
MLA_ROPE = 64
MLA_V = 128
SB_HEADS = 16
SB_DIM = 128
X_HEADS = 4
X_DIM = 128
ROPE_THETA = 10000.0
EPS = 1e-6

LANE = 128
QK_PAD = 2 * LANE
VMEM_LIMIT_BYTES = 56 * 1024 * 1024
ATTN_TILE = 256
LOG2_E = math.log2(math.e)
BF16 = jnp.bfloat16
F32 = jnp.float32


def _tile(dim, pref, align):
    if dim <= pref:
        return dim
    t = (pref // align) * align
    while t >= align:
        if dim % t == 0:
            return t
        t -= align
    return dim


def _params(ndims):
    return pltpu.CompilerParams(dimension_semantics=("arbitrary",) * ndims,
                                vmem_limit_bytes=VMEM_LIMIT_BYTES)


def _rms(y, width):
    ms = jnp.sum(y * y, axis=-1, keepdims=True) * (1.0 / width)
    return y * lax.rsqrt(ms + EPS)


def _rope(r, cos, sin):
    return r * cos + pltpu.roll(r, MLA_ROPE // 2, 1) * sin


def _rmsnorm_body(x_ref, g_ref, o_ref):
    x = x_ref[...]
    o_ref[...] = (_rms(x, x.shape[-1]) * g_ref[...]).astype(o_ref.dtype)


def _rmsnorm(x, g, name):
    m, d = x.shape
    tm = _tile(m, 256, 8)
    return pl.pallas_call(
        _rmsnorm_body, grid=(m // tm,),
        in_specs=[pl.BlockSpec((tm, d), lambda i: (i, 0)), pl.BlockSpec((1, d), lambda i: (0, 0))],
        out_specs=pl.BlockSpec((tm, d), lambda i: (i, 0)),
        out_shape=jax.ShapeDtypeStruct((m, d), BF16),
        compiler_params=_params(1), name=name)(x, g.reshape(1, d))


def _mixnorm_body(a_ref, b_ref, ga_ref, gb_ref, o_ref):
    wa = a_ref.shape[-1]
    a, b = a_ref[...], b_ref[...]
    o_ref[:, :wa] = (_rms(a, wa) * ga_ref[...]).astype(o_ref.dtype)
    o_ref[:, wa:] = (_rms(b, b.shape[-1]) * gb_ref[...]).astype(o_ref.dtype)


def _mixnorm(a, b, ga, gb):
    m, wa = a.shape
    wb = b.shape[1]
    tm = _tile(m, 256, 8)
    return pl.pallas_call(
        _mixnorm_body, grid=(m // tm,),
        in_specs=[pl.BlockSpec((tm, wa), lambda i: (i, 0)), pl.BlockSpec((tm, wb), lambda i: (i, 0)),
                  pl.BlockSpec((1, wa), lambda i: (0, 0)), pl.BlockSpec((1, wb), lambda i: (0, 0))],
        out_specs=pl.BlockSpec((tm, wa + wb), lambda i: (i, 0)),
        out_shape=jax.ShapeDtypeStruct((m, wa + wb), BF16),
        compiler_params=_params(1), name="mixnorm")(a, b, ga.reshape(1, wa), gb.reshape(1, wb))


def _rope_table_body(pos_ref, freq_ref, cos_ref, sin_ref):
    ang = pos_ref[...] * freq_ref[...]
    lane = lax.broadcasted_iota(jnp.int32, ang.shape, 1)
    sin = jnp.sin(ang)
    cos_ref[...] = jnp.where(lane < MLA_ROPE, jnp.cos(ang), 0.0)
    sin_ref[...] = jnp.where(lane < MLA_ROPE // 2, -sin, jnp.where(lane < MLA_ROPE, sin, 0.0))


def _rope_tables(positions):
    t = positions.size
    half = MLA_ROPE // 2
    inv_freq = ROPE_THETA ** (-jnp.arange(half, dtype=F32) / half)
    freq = jnp.concatenate([inv_freq, inv_freq, jnp.zeros((LANE - MLA_ROPE,), F32)]).reshape(1, LANE)
    pos = positions.astype(F32).reshape(t, 1)
    tm = _tile(t, 512, 8)
    return pl.pallas_call(
        _rope_table_body, grid=(t // tm,),
        in_specs=[pl.BlockSpec((tm, 1), lambda i: (i, 0)), pl.BlockSpec((1, LANE), lambda i: (0, 0))],
        out_specs=[pl.BlockSpec((tm, LANE), lambda i: (i, 0))] * 2,
        out_shape=[jax.ShapeDtypeStruct((t, LANE), F32)] * 2,
        compiler_params=_params(1), name="rope_tables")(pos, freq)


def _weight_spec(layer, k, tn):
    return pl.BlockSpec((None, k, tn), lambda i, j: (layer, 0, j))


def _mm_body(a_ref, w_ref, *rest, n_extra, epilogue):
    extras, outs = rest[:n_extra], rest[n_extra:]
    y = jnp.dot(a_ref[...], w_ref[...].astype(BF16), preferred_element_type=F32)
    epilogue(y, extras, outs)


def _matmul(a, w, layer, *, tm, tn, epilogue, out_shape, out_specs, extras=(), extra_specs=(), name):
    m, k = a.shape
    n = w.shape[2]
    body = functools.partial(_mm_body, n_extra=len(extras), epilogue=epilogue)
    return pl.pallas_call(
        body, grid=(m // tm, n // tn),
        in_specs=[pl.BlockSpec((tm, k), lambda i, j: (i, 0)), _weight_spec(layer, k, tn), *extra_specs],
        out_specs=out_specs, out_shape=out_shape,
        compiler_params=_params(2), name=name)(a, w, *extras)


def _row_spec(tm, width):
    return pl.BlockSpec((tm, width), lambda i, j: (i, 0))


def _const_spec(width):
    return pl.BlockSpec((1, width), lambda i, j: (0, 0))


def _tile_spec(tm, tn):
    return pl.BlockSpec((tm, tn), lambda i, j: (i, j))


def _store_epilogue(y, extras, outs):
    outs[0][...] = y.astype(outs[0].dtype)


def _residual_epilogue(y, extras, outs):
    outs[0][...] = extras[0][...] + y


def _matmul_plain(a, w, layer, out_dtype, name, tm=1024, tn=512):
    m, n = a.shape[0], w.shape[2]
    tm, tn = _tile(m, tm, 8), _tile(n, tn, LANE)
    return _matmul(a, w, layer, tm=tm, tn=tn, epilogue=_store_epilogue,
                   out_shape=jax.ShapeDtypeStruct((m, n), out_dtype),
                   out_specs=_tile_spec(tm, tn), name=name)


def _matmul_residual(a, w, layer, res, name, tm=1024, tn=512):
    m, n = a.shape[0], w.shape[2]
    tm, tn = _tile(m, tm, 8), _tile(n, tn, LANE)
    return _matmul(a, w, layer, tm=tm, tn=tn, epilogue=_residual_epilogue,
                   extras=(res,), extra_specs=(_tile_spec(tm, tn),),
                   out_shape=jax.ShapeDtypeStruct((m, n), F32),
                   out_specs=_tile_spec(tm, tn), name=name)


def _latent_epilogue(y, extras, outs, *, q_rank, kv_rank):
    gq_ref, gkv_ref, gkr_ref, cos_ref, sin_ref = extras
    cq_ref, ckv_ref, kr_ref = outs
    cq_ref[...] = (_rms(y[:, :q_rank], q_rank) * gq_ref[...]).astype(cq_ref.dtype)
    ckv = y[:, q_rank:q_rank + kv_rank]
    ckv_ref[...] = (_rms(ckv, kv_rank) * gkv_ref[...]).astype(ckv_ref.dtype)
    kr = _rms(y[:, q_rank + kv_rank:], 2 * MLA_ROPE) * gkr_ref[...]
    kr_ref[...] = _rope(kr, cos_ref[...], sin_ref[...]).astype(kr_ref.dtype)


def _latent_proj(a, w_lat, layer, g_q_lat, g_kv_lat, g_k_rope, cos, sin):
    m = a.shape[0]
    q_rank, kv_rank = g_q_lat.shape[0], g_kv_lat.shape[0]
    n = w_lat.shape[2]
    tm = _tile(m, 512, 8)
    return _matmul(
        a, w_lat, layer, tm=tm, tn=n,
        epilogue=functools.partial(_latent_epilogue, q_rank=q_rank, kv_rank=kv_rank),
        extras=(g_q_lat.reshape(1, -1), g_kv_lat.reshape(1, -1), g_k_rope, cos, sin),
        extra_specs=(_const_spec(q_rank), _const_spec(kv_rank), _const_spec(LANE),
                     _row_spec(tm, LANE), _row_spec(tm, LANE)),
        out_shape=[jax.ShapeDtypeStruct((m, q_rank), BF16), jax.ShapeDtypeStruct((m, kv_rank), BF16),
                   jax.ShapeDtypeStruct((m, LANE), BF16)],
        out_specs=[_row_spec(tm, q_rank), _row_spec(tm, kv_rank), _row_spec(tm, LANE)],
        name="latent_proj")


def _sb_proj_epilogue(y, extras, outs, *, q_tiles, scale):
    j = pl.program_id(1)
    outs[0][...] = (y * jnp.where(j < q_tiles, scale, 1.0)).astype(outs[0].dtype)


def _sb_proj(a, w_sb, layer):
    m, n = a.shape[0], w_sb.shape[2]
    tm, tn = _tile(m, 1024, 8), _tile(n // 3, 512, LANE)
    return _matmul(
        a, w_sb, layer, tm=tm, tn=tn,
        epilogue=functools.partial(_sb_proj_epilogue, q_tiles=(n // 3) // tn, scale=LOG2_E / math.sqrt(SB_DIM)),
        out_shape=jax.ShapeDtypeStruct((m, n), BF16), out_specs=_tile_spec(tm, tn), name="sb_proj")


def _q_up_body(cq_ref, w_ref, gn_ref, gr_ref, cos_ref, sin_ref, o_ref, *, heads, scale):
    cq = cq_ref[...]
    cos, sin = cos_ref[...] * scale, sin_ref[...] * scale
    gn = gn_ref[...] * scale
    for h in range(heads):
        lo = h * QK_PAD
        y = jnp.dot(cq, w_ref[:, lo:lo + QK_PAD], preferred_element_type=F32)
        o_ref[:, lo:lo + MLA_NOPE] = (_rms(y[:, :MLA_NOPE], MLA_NOPE) * gn).astype(o_ref.dtype)
        r = _rms(y[:, MLA_NOPE:], 2 * MLA_ROPE) * gr_ref[...]
        o_ref[:, lo + MLA_NOPE:lo + QK_PAD] = _rope(r, cos, sin).astype(o_ref.dtype)


def _q_up(cq, w_q, layer, g_nope, g_rope, cos, sin):
    m, k = cq.shape
    n = w_q.shape[2]
    tm, heads = _tile(m, 1024, 8), 4
    tn = heads * QK_PAD
    body = functools.partial(_q_up_body, heads=heads, scale=LOG2_E / math.sqrt(MLA_NOPE + MLA_ROPE))
    return pl.pallas_call(
        body, grid=(m // tm, n // tn),
        in_specs=[pl.BlockSpec((tm, k), lambda i, j: (i, 0)), _weight_spec(layer, k, tn),
                  _const_spec(LANE), _const_spec(LANE), _row_spec(tm, LANE), _row_spec(tm, LANE)],
        out_specs=_tile_spec(tm, tn), out_shape=jax.ShapeDtypeStruct((m, n), BF16),
        compiler_params=_params(2), name="q_up")(cq, w_q, g_nope, g_rope, cos, sin)


def _k_up_epilogue(y, extras, outs, *, heads):
    gn_ref, kr_ref = extras
    o_ref = outs[0]
    for h in range(heads):
        k = _rms(y[:, h * MLA_NOPE:(h + 1) * MLA_NOPE], MLA_NOPE) * gn_ref[...]
        o_ref[:, h * QK_PAD:h * QK_PAD + MLA_NOPE] = k.astype(o_ref.dtype)
        o_ref[:, h * QK_PAD + MLA_NOPE:(h + 1) * QK_PAD] = kr_ref[...]


def _k_up(ckv, w_k, layer, g_nope, k_rope):
    m, n = ckv.shape[0], w_k.shape[2]
    tm, heads = _tile(m, 1024, 8), 4
    tn = heads * MLA_NOPE
    return _matmul(
        ckv, w_k, layer, tm=tm, tn=tn, epilogue=functools.partial(_k_up_epilogue, heads=heads),
        extras=(g_nope, k_rope), extra_specs=(_const_spec(LANE), _row_spec(tm, LANE)),
        out_shape=jax.ShapeDtypeStruct((m, (n // MLA_NOPE) * QK_PAD), BF16),
        out_specs=_tile_spec(tm, heads * QK_PAD), name="k_up")


def _headnorm_epilogue(y, extras, outs, *, heads, scale):
    g_ref = extras[0]
    for h in range(heads):
        sl = slice(h * X_DIM, (h + 1) * X_DIM)
        outs[0][:, sl] = (_rms(y[:, sl], X_DIM) * g_ref[...] * scale).astype(outs[0].dtype)


def _xq_proj(a, w_xq, layer, g_xq):
    m, n = a.shape[0], w_xq.shape[2]
    tm = _tile(m, 1024, 8)
    return _matmul(
        a, w_xq, layer, tm=tm, tn=n,
        epilogue=functools.partial(_headnorm_epilogue, heads=n // X_DIM, scale=1.0 / math.sqrt(X_DIM)),
        extras=(g_xq.reshape(1, X_DIM),), extra_specs=(_const_spec(X_DIM),),
        out_shape=jax.ShapeDtypeStruct((m, n), BF16), out_specs=_row_spec(tm, n), name="xq_proj")


def _xkv_epilogue(y, extras, outs, *, heads):
    g_ref = extras[0]
    k_ref, v_ref = outs
    width = heads * X_DIM
    for h in range(heads):
        sl = slice(h * X_DIM, (h + 1) * X_DIM)
        k_ref[:, sl] = (_rms(y[:, sl], X_DIM) * g_ref[...]).astype(k_ref.dtype)
    v_ref[...] = y[:, width:].astype(v_ref.dtype)


def _xkv_proj(a, w_xkv, layer, g_xk):
    m, n = a.shape[0], w_xkv.shape[2]
    tm, width = _tile(m, 512, 8), n // 2
    return _matmul(
        a, w_xkv, layer, tm=tm, tn=n, epilogue=functools.partial(_xkv_epilogue, heads=width // X_DIM),
        extras=(g_xk.reshape(1, X_DIM),), extra_specs=(_const_spec(X_DIM),),
        out_shape=[jax.ShapeDtypeStruct((m, width), BF16)] * 2,
        out_specs=[_row_spec(tm, width)] * 2, name="xkv_proj")


def _swiglu_body(a_ref, wg_ref, wu_ref, o_ref):
    a = a_ref[...]
    g = jnp.dot(a, wg_ref[...].astype(BF16), preferred_element_type=F32)
    u = jnp.dot(a, wu_ref[...].astype(BF16), preferred_element_type=F32)
    o_ref[...] = (g / (1.0 + jnp.exp(-g)) * u).astype(o_ref.dtype)


def _swiglu(a, w_gate, w_up, layer):
    m, k = a.shape
    n = w_gate.shape[2]
    tm, tn = _tile(m, 1024, 8), _tile(n, 256, LANE)
    return pl.pallas_call(
        _swiglu_body, grid=(m // tm, n // tn),
        in_specs=[pl.BlockSpec((tm, k), lambda i, j: (i, 0)),
                  _weight_spec(layer, k, tn), _weight_spec(layer, k, tn)],
        out_specs=_tile_spec(tm, tn), out_shape=jax.ShapeDtypeStruct((m, n), BF16),
        compiler_params=_params(2), name="swiglu")(a, w_gate, w_up)


def _dot_nt(a, b):
    return lax.dot_general(a, b, (((1,), (1,)), ((), ())), preferred_element_type=F32)


def _mla_attn_body(q_ref, k_ref, v_ref, o_ref, *, tile):
    n_tiles = q_ref.shape[0] // tile
    row = lax.broadcasted_iota(jnp.int32, (tile, tile), 0)
    col = lax.broadcasted_iota(jnp.int32, (tile, tile), 1)
    for qi in range(n_tiles):
        lo, hi = qi * tile, (qi + 1) * tile
        q = q_ref[lo:hi, :]
        s_diag = jnp.where(col <= row, _dot_nt(q, k_ref[lo:hi, :]), -jnp.inf)
        m = jnp.max(s_diag, axis=1, keepdims=True)
        if qi:
            s_off = _dot_nt(q, k_ref[0:lo, :])
            m = jnp.maximum(m, jnp.max(s_off, axis=1, keepdims=True))
        p_diag = jnp.exp2(s_diag - m)
        l = jnp.sum(p_diag, axis=1, keepdims=True)
        acc = jnp.dot(p_diag.astype(BF16), v_ref[lo:hi, :], preferred_element_type=F32)
        if qi:
            p_off = jnp.exp2(s_off - m)
            l = l + jnp.sum(p_off, axis=1, keepdims=True)
            acc = acc + jnp.dot(p_off.astype(BF16), v_ref[0:lo, :], preferred_element_type=F32)
        o_ref[lo:hi, :] = acc / l


def _mla_attention(q, k, v, batch):
    t = q.shape[0]
    seq = t // batch
    heads = q.shape[1] // QK_PAD
    tile = _tile(seq, ATTN_TILE, 8)
    return pl.pallas_call(
        functools.partial(_mla_attn_body, tile=tile), grid=(batch, heads),
        in_specs=[pl.BlockSpec((seq, QK_PAD), lambda b, h: (b, h)),
                  pl.BlockSpec((seq, QK_PAD), lambda b, h: (b, h)),
                  pl.BlockSpec((seq, MLA_V), lambda b, h: (b, h))],
        out_specs=pl.BlockSpec((seq, MLA_V), lambda b, h: (b, h)),
        out_shape=jax.ShapeDtypeStruct((t, heads * MLA_V), F32),
        compiler_params=_params(2), name="mla_attention")(q, k, v)


def _sb_attn_body(q_ref, k_ref, v_ref, o_ref, *, tile):
    n_tiles = q_ref.shape[0] // tile
    row = lax.broadcasted_iota(jnp.int32, (tile, tile), 0)
    col = lax.broadcasted_iota(jnp.int32, (tile, tile), 1)
    strict = col < row
    ones_below = jnp.where(row > col, 1.0, 0.0).astype(BF16)

    def logs(z):
        log_beta = jnp.minimum(z, 0.0) - jnp.log2(1.0 + jnp.exp2(-jnp.abs(z)))
        return log_beta, log_beta - z

    def suffix_sums(log_keep):
        return jnp.dot(log_keep.astype(BF16), ones_below, preferred_element_type=F32)

    for qi in range(n_tiles):
        lo_row, hi_row = qi * tile, (qi + 1) * tile
        q = q_ref[lo_row:hi_row, :]
        log_beta, log_keep = logs(_dot_nt(q, k_ref[lo_row:hi_row, :]))
        log_keep = jnp.where(strict, log_keep, 0.0)
        a = jnp.where(strict, jnp.exp2(log_beta + suffix_sums(log_keep)), 0.0)
        acc = jnp.dot(a.astype(BF16), v_ref[lo_row:hi_row, :], preferred_element_type=F32)
        if qi:
            z = jnp.concatenate([_dot_nt(q, k_ref[c * tile:(c + 1) * tile, :]) for c in range(qi)], axis=0)
            log_beta, keep = logs(z)
            totals = jnp.sum(keep, axis=1, keepdims=True)
            run = jnp.sum(log_keep, axis=1, keepdims=True)
            carries = [None] * qi
            for c in reversed(range(qi)):
                carries[c] = run
                run = run + totals[c * tile:(c + 1) * tile]
            a = jnp.exp2(log_beta + suffix_sums(keep) + jnp.concatenate(carries, axis=0)).astype(BF16)
            a = jnp.concatenate([a[c * tile:(c + 1) * tile] for c in range(qi)], axis=1)
            acc = acc + jnp.dot(a, v_ref[0:lo_row, :], preferred_element_type=F32)
        o_ref[lo_row:hi_row, :] = acc


def _sb_attention(qkv, batch):
    t = qkv.shape[0]
    seq = t // batch
    heads = qkv.shape[1] // (3 * SB_DIM)
    tile = _tile(seq, ATTN_TILE, 8)
    return pl.pallas_call(
        functools.partial(_sb_attn_body, tile=tile), grid=(batch, heads),
        in_specs=[pl.BlockSpec((seq, SB_DIM), lambda b, h: (b, h)),
                  pl.BlockSpec((seq, SB_DIM), lambda b, h: (b, heads + h)),
                  pl.BlockSpec((seq, SB_DIM), lambda b, h: (b, 2 * heads + h))],
        out_specs=pl.BlockSpec((seq, SB_DIM), lambda b, h: (b, h)),
        out_shape=jax.ShapeDtypeStruct((t, heads * SB_DIM), F32),
        compiler_params=_params(2), name="sb_attention")(qkv, qkv, qkv)


def _cross_attn_body(q_ref, k_ref, v_ref, o_ref, *, heads):
    for h in range(heads):
        sl = slice(h * X_DIM, (h + 1) * X_DIM)
        s = _dot_nt(q_ref[:, sl], k_ref[:, sl])
        p = jnp.exp(s - jnp.max(s, axis=1, keepdims=True))
        o = jnp.dot(p.astype(BF16), v_ref[:, sl], preferred_element_type=F32)
        o_ref[:, sl] = (o / jnp.sum(p, axis=1, keepdims=True)).astype(o_ref.dtype)


def _cross_attention(q, k, v, batch):
    t, width = q.shape
    seq, mem_len = t // batch, k.shape[0] // batch
    tq = _tile(seq, 512, 8)
    n_q = seq // tq
    return pl.pallas_call(
        functools.partial(_cross_attn_body, heads=width // X_DIM), grid=(batch, n_q),
        in_specs=[pl.BlockSpec((tq, width), lambda b, i: (b * n_q + i, 0)),
                  pl.BlockSpec((mem_len, width), lambda b, i: (b, 0)),
                  pl.BlockSpec((mem_len, width), lambda b, i: (b, 0))],
        out_specs=pl.BlockSpec((tq, width), lambda b, i: (b * n_q + i, 0)),
        out_shape=jax.ShapeDtypeStruct((t, width), BF16),
        compiler_params=_params(2), name="cross_attention")(q, k, v)


def _twice(g):
    return jnp.tile(g, 2).reshape(1, 2 * g.shape[0])


def _relaid_weights(w_in, w_q_up, w_kv_up, w_xkv, w_down, q_rank, kv_rank):
    assert 2 * MLA_ROPE == LANE and (q_rank + kv_rank) % LANE == 0
    layers, d = w_in.shape[:2]
    lat = q_rank + kv_rank + MLA_ROPE
    w_q = w_q_up.reshape(layers, q_rank, MLA_HEADS, MLA_NOPE + MLA_ROPE)
    w_q = jnp.concatenate([w_q, w_q[..., MLA_NOPE:]], axis=-1)
    w_kv = w_kv_up.reshape(layers, kv_rank, MLA_HEADS, MLA_NOPE + MLA_V)
    w_x = w_xkv.reshape(layers, d, X_HEADS, 2 * X_DIM)
    ws = dict(lat=jnp.concatenate([w_in[:, :, :lat], w_in[:, :, lat - MLA_ROPE:lat]], axis=2),
              sb=w_in[:, :, lat:],
              q=w_q.reshape(layers, q_rank, MLA_HEADS * QK_PAD),
              k=w_kv[..., :MLA_NOPE].reshape(layers, kv_rank, -1),
              v=w_kv[..., MLA_NOPE:].reshape(layers, kv_rank, -1),
              xkv=jnp.concatenate([w_x[..., :X_DIM].reshape(layers, d, -1),
                                   w_x[..., X_DIM:].reshape(layers, d, -1)], axis=2),
              down=w_down)
    return {name: w.astype(BF16) for name, w in ws.items()}


def kernel(x, mem, positions, g_attn, w_in, g_q_lat, g_kv_lat, w_q_up, w_kv_up, g_mla_q, g_mla_k,
           g_mla_out, g_sb_out, w_out, g_cross, g_mem, w_xq, w_xkv, g_xq, g_xk, w_xo, g_ffn,
           w_gate, w_up, w_down):
    batch, seq, d = x.shape
    depth = w_in.shape[0]
    q_rank, kv_rank = g_q_lat.shape[1], g_kv_lat.shape[1]
    x = x.reshape(batch * seq, d)
    mem2 = mem.reshape(-1, d)
    cos, sin = _rope_tables(positions)
    w = _relaid_weights(w_in, w_q_up, w_kv_up, w_xkv, w_down, q_rank, kv_rank)
    for l in range(depth):
        n = _rmsnorm(x, g_attn[l], "norm_attn")
        cq, ckv, k_rope = _latent_proj(n, w["lat"], l, g_q_lat[l], g_kv_lat[l],
                                       _twice(g_mla_k[l, MLA_NOPE:]), cos, sin)
        q = _q_up(cq, w["q"], l, g_mla_q[l, :MLA_NOPE].reshape(1, LANE),
                  _twice(g_mla_q[l, MLA_NOPE:]), cos, sin)
        k = _k_up(ckv, w["k"], l, g_mla_k[l, :MLA_NOPE].reshape(1, LANE), k_rope)
        v = _matmul_plain(ckv, w["v"], l, BF16, "v_up")
        o_mla = _mla_attention(q, k, v, batch)
        o_sb = _sb_attention(_sb_proj(n, w["sb"], l), batch)
        mixed = _mixnorm(o_mla, o_sb, g_mla_out[l], g_sb_out[l])
        x = _matmul_residual(mixed, w_out, l, x, "out_proj")
        h = _rmsnorm(x, g_cross[l], "norm_cross")
        xq = _xq_proj(h, w_xq, l, g_xq[l])
        xk, xv = _xkv_proj(_rmsnorm(mem2, g_mem[l], "norm_mem"), w["xkv"], l, g_xk[l])
        x = _matmul_residual(_cross_attention(xq, xk, xv, batch), w_xo, l, x, "cross_out")
        h = _rmsnorm(x, g_ffn[l], "norm_ffn")
        x = _matmul_residual(_swiglu(h, w_gate, w_up, l), w["down"], l, x, "ffn_down", tm=512)
    return x.reshape(batch, seq, d)
```

```python
import functools
import math

import jax
import jax.numpy as jnp
from jax import lax
from jax.experimental import pallas as pl
from jax.experimental.pallas import tpu as pltpu

MLA_HEADS = 16
MLA_NOPE = 128
MLA_ROPE = 64
MLA_V = 128
SB_HEADS = 16
SB_DIM = 128
X_HEADS = 4
X_DIM = 128
ROPE_THETA = 10000.0
EPS = 1e-6

LANE = 128
QK_PAD = 2 * LANE
VMEM_LIMIT_BYTES = 56 * 1024 * 1024
ATTN_TILE = 256
LOG2_E = math.log2(math.e)
BF16 = jnp.bfloat16
F32 = jnp.float32


def _tile(dim, pref, align):
    if dim <= pref:
        return dim
    t = (pref // align) * align
    while t >= align:
        if dim % t == 0:
            return t
        t -= align
    return dim


def _params(ndims):
    return pltpu.CompilerParams(dimension_semantics=("arbitrary",) * ndims,
                                vmem_limit_bytes=VMEM_LIMIT_BYTES)


def _rms(y, width):
    ms = jnp.sum(y * y, axis=-1, keepdims=True) * (1.0 / width)
    return y * lax.rsqrt(ms + EPS)


def _rope(r, cos, sin):
    return r * cos + pltpu.roll(r, MLA_ROPE // 2, 1) * sin


def _rmsnorm_body(x_ref, g_ref, o_ref):
    x = x_ref[...]
    o_ref[...] = (_rms(x, x.shape[-1]) * g_ref[...]).astype(o_ref.dtype)


def _rmsnorm(x, g, name):
    m, d = x.shape
    tm = _tile(m, 256, 8)
    return pl.pallas_call(
        _rmsnorm_body, grid=(m // tm,),
        in_specs=[pl.BlockSpec((tm, d), lambda i: (i, 0)), pl.BlockSpec((1, d), lambda i: (0, 0))],
        out_specs=pl.BlockSpec((tm, d), lambda i: (i, 0)),
        out_shape=jax.ShapeDtypeStruct((m, d), BF16),
        compiler_params=_params(1), name=name)(x, g.reshape(1, d))


def _mixnorm_body(a_ref, b_ref, ga_ref, gb_ref, o_ref):
    wa = a_ref.shape[-1]
    a, b = a_ref[...], b_ref[...]
    o_ref[:, :wa] = (_rms(a, wa) * ga_ref[...]).astype(o_ref.dtype)
    o_ref[:, wa:] = (_rms(b, b.shape[-1]) * gb_ref[...]).astype(o_ref.dtype)


def _mixnorm(a, b, ga, gb):
    m, wa = a.shape
    wb = b.shape[1]
    tm = _tile(m, 256, 8)
    return pl.pallas_call(
        _mixnorm_body, grid=(m // tm,),
        in_specs=[pl.BlockSpec((tm, wa), lambda i: (i, 0)), pl.BlockSpec((tm, wb), lambda i: (i, 0)),
                  pl.BlockSpec((1, wa), lambda i: (0, 0)), pl.BlockSpec((1, wb), lambda i: (0, 0))],
        out_specs=pl.BlockSpec((tm, wa + wb), lambda i: (i, 0)),
        out_shape=jax.ShapeDtypeStruct((m, wa + wb), BF16),
        compiler_params=_params(1), name="mixnorm")(a, b, ga.reshape(1, wa), gb.reshape(1, wb))


def _rope_table_body(pos_ref, freq_ref, cos_ref, sin_ref):
    ang = pos_ref[...] * freq_ref[...]
    lane = lax.broadcasted_iota(jnp.int32, ang.shape, 1)
    sin = jnp.sin(ang)
    cos_ref[...] = jnp.where(lane < MLA_ROPE, jnp.cos(ang), 0.0)
    sin_ref[...] = jnp.where(lane < MLA_ROPE // 2, -sin, jnp.where(lane < MLA_ROPE, sin, 0.0))


def _rope_tables(positions):
    t = positions.size
    half = MLA_ROPE // 2
    inv_freq = ROPE_THETA ** (-jnp.arange(half, dtype=F32) / half)
    freq = jnp.concatenate([inv_freq, inv_freq, jnp.zeros((LANE - MLA_ROPE,), F32)]).reshape(1, LANE)
    pos = positions.astype(F32).reshape(t, 1)
    tm = _tile(t, 512, 8)
    return pl.pallas_call(
        _rope_table_body, grid=(t // tm,),
        in_specs=[pl.BlockSpec((tm, 1), lambda i: (i, 0)), pl.BlockSpec((1, LANE), lambda i: (0, 0))],
        out_specs=[pl.BlockSpec((tm, LANE), lambda i: (i, 0))] * 2,
        out_shape=[jax.ShapeDtypeStruct((t, LANE), F32)] * 2,
        compiler_params=_params(1), name="rope_tables")(pos, freq)


def _weight_spec(layer, k, tn):
    return pl.BlockSpec((None, k, tn), lambda i, j: (layer, 0, j))


def _mm_body(a_ref, w_ref, *rest, n_extra, epilogue):
    extras, outs = rest[:n_extra], rest[n_extra:]
    y = jnp.dot(a_ref[...], w_ref[...].astype(BF16), preferred_element_type=F32)
    epilogue(y, extras, outs)


def _matmul(a, w, layer, *, tm, tn, epilogue, out_shape, out_specs, extras=(), extra_specs=(), name):
    m, k = a.shape
    n = w.shape[2]
    body = functools.partial(_mm_body, n_extra=len(extras), epilogue=epilogue)
    return pl.pallas_call(
        body, grid=(m // tm, n // tn),
        in_specs=[pl.BlockSpec((tm, k), lambda i, j: (i, 0)), _weight_spec(layer, k, tn), *extra_specs],
        out_specs=out_specs, out_shape=out_shape,
        compiler_params=_params(2), name=name)(a, w, *extras)


def _row_spec(tm, width):
    return pl.BlockSpec((tm, width), lambda i, j: (i, 0))


def _const_spec(width):
    return pl.BlockSpec((1, width), lambda i, j: (0, 0))


def _tile_spec(tm, tn):
    return pl.BlockSpec((tm, tn), lambda i, j: (i, j))


def _store_epilogue(y, extras, outs):
    outs[0][...] = y.astype(outs[0].dtype)


def _residual_epilogue(y, extras, outs):
    outs[0][...] = extras[0][...] + y


def _matmul_plain(a, w, layer, out_dtype, name, tm=1024, tn=512):
    m, n = a.shape[0], w.shape[2]
    tm, tn = _tile(m, tm, 8), _tile(n, tn, LANE)
    return _matmul(a, w, layer, tm=tm, tn=tn, epilogue=_store_epilogue,
                   out_shape=jax.ShapeDtypeStruct((m, n), out_dtype),
                   out_specs=_tile_spec(tm, tn), name=name)


def _matmul_residual(a, w, layer, res, name, tm=1024, tn=512):
    m, n = a.shape[0], w.shape[2]
    tm, tn = _tile(m, tm, 8), _tile(n, tn, LANE)
    return _matmul(a, w, layer, tm=tm, tn=tn, epilogue=_residual_epilogue,
                   extras=(res,), extra_specs=(_tile_spec(tm, tn),),
                   out_shape=jax.ShapeDtypeStruct((m, n), F32),
                   out_specs=_tile_spec(tm, tn), name=name)


def _latent_epilogue(y, extras, outs, *, q_rank, kv_rank):
    gq_ref, gkv_ref, gkr_ref, cos_ref, sin_ref = extras
    cq_ref, ckv_ref, kr_ref = outs
    cq_ref[...] = (_rms(y[:, :q_rank], q_rank) * gq_ref[...]).astype(cq_ref.dtype)
    ckv = y[:, q_rank:q_rank + kv_rank]
    ckv_ref[...] = (_rms(ckv, kv_rank) * gkv_ref[...]).astype(ckv_ref.dtype)
    kr = _rms(y[:, q_rank + kv_rank:], 2 * MLA_ROPE) * gkr_ref[...]
    kr_ref[...] = _rope(kr, cos_ref[...], sin_ref[...]).astype(kr_ref.dtype)


def _norm_latent_body(x_ref, g_ref, w_ref, gq_ref, gkv_ref, gkr_ref, cos_ref, sin_ref,
                      n_ref, cq_ref, ckv_ref, kr_ref, *, q_rank, kv_rank):
    x = x_ref[...]
    n = (_rms(x, x.shape[-1]) * g_ref[...]).astype(BF16)
    n_ref[...] = n
    y = jnp.dot(n, w_ref[...], preferred_element_type=F32)
    _latent_epilogue(y, (gq_ref, gkv_ref, gkr_ref, cos_ref, sin_ref), (cq_ref, ckv_ref, kr_ref),
                     q_rank=q_rank, kv_rank=kv_rank)


def _norm_latent_proj(x, g_attn, w_lat, layer, g_q_lat, g_kv_lat, g_k_rope, cos, sin):
    m, d = x.shape
    q_rank, kv_rank = g_q_lat.shape[0], g_kv_lat.shape[0]
    n = w_lat.shape[2]
    tm = _tile(m, 256, 8)
    row = lambda width: pl.BlockSpec((tm, width), lambda i: (i, 0))
    const = lambda width: pl.BlockSpec((1, width), lambda i: (0, 0))
    return pl.pallas_call(
        functools.partial(_norm_latent_body, q_rank=q_rank, kv_rank=kv_rank), grid=(m // tm,),
        in_specs=[row(d), const(d), pl.BlockSpec((None, d, n), lambda i: (layer, 0, 0)),
                  const(q_rank), const(kv_rank), const(LANE), row(LANE), row(LANE)],
        out_specs=[row(d), row(q_rank), row(kv_rank), row(LANE)],
        out_shape=[jax.ShapeDtypeStruct((m, d), BF16), jax.ShapeDtypeStruct((m, q_rank), BF16),
                   jax.ShapeDtypeStruct((m, kv_rank), BF16), jax.ShapeDtypeStruct((m, LANE), BF16)],
        compiler_params=_params(1), name="norm_latent_proj")(
            x, g_attn.reshape(1, d), w_lat, g_q_lat.reshape(1, -1), g_kv_lat.reshape(1, -1), g_k_rope, cos, sin)


def _sb_proj_epilogue(y, extras, outs, *, q_tiles, scale):
    j = pl.program_id(1)
    outs[0][...] = (y * jnp.where(j < q_tiles, scale, 1.0)).astype(outs[0].dtype)


def _sb_proj(a, w_sb, layer):
    m, n = a.shape[0], w_sb.shape[2]
    tm, tn = _tile(m, 1024, 8), _tile(n // 3, 512, LANE)
    return _matmul(
        a, w_sb, layer, tm=tm, tn=tn,
        epilogue=functools.partial(_sb_proj_epilogue, q_tiles=(n // 3) // tn, scale=LOG2_E / math.sqrt(SB_DIM)),
        out_shape=jax.ShapeDtypeStruct((m, n), BF16), out_specs=_tile_spec(tm, tn), name="sb_proj")


def _q_up_body(cq_ref, w_ref, gn_ref, gr_ref, cos_ref, sin_ref, o_ref, *, heads, scale):
    cq = cq_ref[...]
    cos, sin = cos_ref[...] * scale, sin_ref[...] * scale
    gn = gn_ref[...] * scale
    for h in range(heads):
        lo = h * QK_PAD
        y = jnp.dot(cq, w_ref[:, lo:lo + QK_PAD], preferred_element_type=F32)
        o_ref[:, lo:lo + MLA_NOPE] = (_rms(y[:, :MLA_NOPE], MLA_NOPE) * gn).astype(o_ref.dtype)
        r = _rms(y[:, MLA_NOPE:], 2 * MLA_ROPE) * gr_ref[...]
        o_ref[:, lo + MLA_NOPE:lo + QK_PAD] = _rope(r, cos, sin).astype(o_ref.dtype)


def _q_up(cq, w_q, layer, g_nope, g_rope, cos, sin):
    m, k = cq.shape
    n = w_q.shape[2]
    tm, heads = _tile(m, 1024, 8), 4
    tn = heads * QK_PAD
    body = functools.partial(_q_up_body, heads=heads, scale=LOG2_E / math.sqrt(MLA_NOPE + MLA_ROPE))
    return pl.pallas_call(
        body, grid=(m // tm, n // tn),
        in_specs=[pl.BlockSpec((tm, k), lambda i, j: (i, 0)), _weight_spec(layer, k, tn),
                  _const_spec(LANE), _const_spec(LANE), _row_spec(tm, LANE), _row_spec(tm, LANE)],
        out_specs=_tile_spec(tm, tn), out_shape=jax.ShapeDtypeStruct((m, n), BF16),
        compiler_params=_params(2), name="q_up")(cq, w_q, g_nope, g_rope, cos, sin)


def _k_up_epilogue(y, extras, outs, *, heads):
    gn_ref, kr_ref = extras
    o_ref = outs[0]
    for h in range(heads):
        k = _rms(y[:, h * MLA_NOPE:(h + 1) * MLA_NOPE], MLA_NOPE) * gn_ref[...]
        o_ref[:, h * QK_PAD:h * QK_PAD + MLA_NOPE] = k.astype(o_ref.dtype)
        o_ref[:, h * QK_PAD + MLA_NOPE:(h + 1) * QK_PAD] = kr_ref[...]


def _k_up(ckv, w_k, layer, g_nope, k_rope):
    m, n = ckv.shape[0], w_k.shape[2]
    tm, heads = _tile(m, 1024, 8), 4
    tn = heads * MLA_NOPE
    return _matmul(
        ckv, w_k, layer, tm=tm, tn=tn, epilogue=functools.partial(_k_up_epilogue, heads=heads),
        extras=(g_nope, k_rope), extra_specs=(_const_spec(LANE), _row_spec(tm, LANE)),
        out_shape=jax.ShapeDtypeStruct((m, (n // MLA_NOPE) * QK_PAD), BF16),
        out_specs=_tile_spec(tm, heads * QK_PAD), name="k_up")


def _headnorm_epilogue(y, extras, outs, *, heads, scale):
    g_ref = extras[0]
    for h in range(heads):
        sl = slice(h * X_DIM, (h + 1) * X_DIM)
        outs[0][:, sl] = (_rms(y[:, sl], X_DIM) * g_ref[...] * scale).astype(outs[0].dtype)


def _norm_xq_body(x_ref, g_ref, w_ref, gq_ref, o_ref, *, heads, scale):
    x = x_ref[...]
    xn = (_rms(x, x.shape[-1]) * g_ref[...]).astype(BF16)
    y = jnp.dot(xn, w_ref[...].astype(BF16), preferred_element_type=F32)
    _headnorm_epilogue(y, (gq_ref,), (o_ref,), heads=heads, scale=scale)


def _norm_xq_proj(x, g_cross, w_xq, layer, g_xq):
    m, d = x.shape
    n = w_xq.shape[2]
    tm = _tile(m, 512, 8)
    body = functools.partial(_norm_xq_body, heads=n // X_DIM, scale=1.0 / math.sqrt(X_DIM))
    return pl.pallas_call(
        body, grid=(m // tm,),
        in_specs=[pl.BlockSpec((tm, d), lambda i: (i, 0)), pl.BlockSpec((1, d), lambda i: (0, 0)),
                  pl.BlockSpec((None, d, n), lambda i: (layer, 0, 0)),
                  pl.BlockSpec((1, X_DIM), lambda i: (0, 0))],
        out_specs=pl.BlockSpec((tm, n), lambda i: (i, 0)),
        out_shape=jax.ShapeDtypeStruct((m, n), BF16),
        compiler_params=_params(1), name="norm_xq_proj")(x, g_cross.reshape(1, d), w_xq, g_xq.reshape(1, X_DIM))


def _out_norm_body(a_ref, w_ref, res_ref, g_ref, x_ref, h_ref):
    x = res_ref[...] + jnp.dot(a_ref[...], w_ref[...].astype(BF16), preferred_element_type=F32)
    x_ref[...] = x
    h_ref[...] = (_rms(x, x.shape[-1]) * g_ref[...]).astype(h_ref.dtype)


def _matmul_residual_norm(a, w, layer, res, g, name):
    m, k = a.shape
    n = w.shape[2]
    tm = _tile(m, 256, 8)
    return pl.pallas_call(
        _out_norm_body, grid=(m // tm,),
        in_specs=[pl.BlockSpec((tm, k), lambda i: (i, 0)), pl.BlockSpec((None, k, n), lambda i: (layer, 0, 0)),
                  pl.BlockSpec((tm, n), lambda i: (i, 0)), pl.BlockSpec((1, n), lambda i: (0, 0))],
        out_specs=[pl.BlockSpec((tm, n), lambda i: (i, 0))] * 2,
        out_shape=[jax.ShapeDtypeStruct((m, n), F32), jax.ShapeDtypeStruct((m, n), BF16)],
        compiler_params=_params(1), name=name)(a, w, res, g.reshape(1, n))


def _xkv_epilogue(y, extras, outs, *, heads):
    g_ref = extras[0]
    k_ref, v_ref = outs
    width = heads * X_DIM
    for h in range(heads):
        sl = slice(h * X_DIM, (h + 1) * X_DIM)
        k_ref[:, sl] = (_rms(y[:, sl], X_DIM) * g_ref[...]).astype(k_ref.dtype)
    v_ref[...] = y[:, width:].astype(v_ref.dtype)


def _xkv_proj(a, w_xkv, layer, g_xk):
    m, n = a.shape[0], w_xkv.shape[2]
    tm, width = _tile(m, 512, 8), n // 2
    return _matmul(
        a, w_xkv, layer, tm=tm, tn=n, epilogue=functools.partial(_xkv_epilogue, heads=width // X_DIM),
        extras=(g_xk.reshape(1, X_DIM),), extra_specs=(_const_spec(X_DIM),),
        out_shape=[jax.ShapeDtypeStruct((m, width), BF16)] * 2,
        out_specs=[_row_spec(tm, width)] * 2, name="xkv_proj")


def _swiglu_body(a_ref, wg_ref, wu_ref, o_ref):
    a = a_ref[...]
    g = jnp.dot(a, wg_ref[...].astype(BF16), preferred_element_type=F32)
    u = jnp.dot(a, wu_ref[...].astype(BF16), preferred_element_type=F32)
    o_ref[...] = (g / (1.0 + jnp.exp(-g)) * u).astype(o_ref.dtype)


def _swiglu(a, w_gate, w_up, layer):
    m, k = a.shape
    n = w_gate.shape[2]
    tm, tn = _tile(m, 1024, 8), _tile(n, 256, LANE)
    return pl.pallas_call(
        _swiglu_body, grid=(m // tm, n // tn),
        in_specs=[pl.BlockSpec((tm, k), lambda i, j: (i, 0)),
                  _weight_spec(layer, k, tn), _weight_spec(layer, k, tn)],
        out_specs=_tile_spec(tm, tn), out_shape=jax.ShapeDtypeStruct((m, n), BF16),
        compiler_params=_params(2), name="swiglu")(a, w_gate, w_up)


def _dot_nt(a, b):
    return lax.dot_general(a, b, (((1,), (1,)), ((), ())), preferred_element_type=F32)


def _mla_attn_body(q_ref, k_ref, v_ref, o_ref, *, tile):
    n_tiles = q_ref.shape[0] // tile
    row = lax.broadcasted_iota(jnp.int32, (tile, tile), 0)
    col = lax.broadcasted_iota(jnp.int32, (tile, tile), 1)
    for qi in range(n_tiles):
        lo, hi = qi * tile, (qi + 1) * tile
        q = q_ref[lo:hi, :]
        s_diag = jnp.where(col <= row, _dot_nt(q, k_ref[lo:hi, :]), -jnp.inf)
        m = jnp.max(s_diag, axis=1, keepdims=True)
        if qi:
            s_off = _dot_nt(q, k_ref[0:lo, :])
            m = jnp.maximum(m, jnp.max(s_off, axis=1, keepdims=True))
        p_diag = jnp.exp2(s_diag - m)
        l = jnp.sum(p_diag, axis=1, keepdims=True)
        acc = jnp.dot(p_diag.astype(BF16), v_ref[lo:hi, :], preferred_element_type=F32)
        if qi:
            p_off = jnp.exp2(s_off - m)
            l = l + jnp.sum(p_off, axis=1, keepdims=True)
            acc = acc + jnp.dot(p_off.astype(BF16), v_ref[0:lo, :], preferred_element_type=F32)
        o_ref[lo:hi, :] = acc / l


def _mla_attention(q, k, v, batch):
    t = q.shape[0]
    seq = t // batch
    heads = q.shape[1] // QK_PAD
    tile = _tile(seq, ATTN_TILE, 8)
    return pl.pallas_call(
        functools.partial(_mla_attn_body, tile=tile), grid=(batch, heads),
        in_specs=[pl.BlockSpec((seq, QK_PAD), lambda b, h: (b, h)),
                  pl.BlockSpec((seq, QK_PAD), lambda b, h: (b, h)),
                  pl.BlockSpec((seq, MLA_V), lambda b, h: (b, h))],
        out_specs=pl.BlockSpec((seq, MLA_V), lambda b, h: (b, h)),
        out_shape=jax.ShapeDtypeStruct((t, heads * MLA_V), F32),
        compiler_params=_params(2), name="mla_attention")(q, k, v)


def _sb_attn_body(q_ref, k_ref, v_ref, o_ref, *, tile):
    n_tiles = q_ref.shape[0] // tile
    row = lax.broadcasted_iota(jnp.int32, (tile, tile), 0)
    col = lax.broadcasted_iota(jnp.int32, (tile, tile), 1)
    strict = col < row
    ones_below = jnp.where(row > col, 1.0, 0.0).astype(BF16)

    def logs(z):
        log_beta = jnp.minimum(z, 0.0) - jnp.log2(1.0 + jnp.exp2(-jnp.abs(z)))
        return log_beta, log_beta - z

    def suffix_sums(log_keep):
        return jnp.dot(log_keep.astype(BF16), ones_below, preferred_element_type=F32)

    for qi in range(n_tiles):
        lo_row, hi_row = qi * tile, (qi + 1) * tile
        q = q_ref[lo_row:hi_row, :]
        log_beta, log_keep = logs(_dot_nt(q, k_ref[lo_row:hi_row, :]))
        log_keep = jnp.where(strict, log_keep, 0.0)
        a = jnp.where(strict, jnp.exp2(log_beta + suffix_sums(log_keep)), 0.0)
        acc = jnp.dot(a.astype(BF16), v_ref[lo_row:hi_row, :], preferred_element_type=F32)
        if qi:
            z = jnp.concatenate([_dot_nt(q, k_ref[c * tile:(c + 1) * tile, :]) for c in range(qi)], axis=0)
            log_beta, keep = logs(z)
            totals = jnp.sum(keep, axis=1, keepdims=True)
            run = jnp.sum(log_keep, axis=1, keepdims=True)
            carries = [None] * qi
            for c in reversed(range(qi)):
                carries[c] = run
                run = run + totals[c * tile:(c + 1) * tile]
            a = jnp.exp2(log_beta + suffix_sums(keep) + jnp.concatenate(carries, axis=0)).astype(BF16)
            a = jnp.concatenate([a[c * tile:(c + 1) * tile] for c in range(qi)], axis=1)
            acc = acc + jnp.dot(a, v_ref[0:lo_row, :], preferred_element_type=F32)
        o_ref[lo_row:hi_row, :] = acc


def _sb_attention(qkv, batch):
    t = qkv.shape[0]
    seq = t // batch
    heads = qkv.shape[1] // (3 * SB_DIM)
    tile = _tile(seq, ATTN_TILE, 8)
    return pl.pallas_call(
        functools.partial(_sb_attn_body, tile=tile), grid=(batch, heads),
        in_specs=[pl.BlockSpec((seq, SB_DIM), lambda b, h: (b, h)),
                  pl.BlockSpec((seq, SB_DIM), lambda b, h: (b, heads + h)),
                  pl.BlockSpec((seq, SB_DIM), lambda b, h: (b, 2 * heads + h))],
        out_specs=pl.BlockSpec((seq, SB_DIM), lambda b, h: (b, h)),
        out_shape=jax.ShapeDtypeStruct((t, heads * SB_DIM), F32),
        compiler_params=_params(2), name="sb_attention")(qkv, qkv, qkv)


def _cross_attn_body(q_ref, k_ref, v_ref, o_ref, *, heads):
    for h in range(heads):
        sl = slice(h * X_DIM, (h + 1) * X_DIM)
        s = _dot_nt(q_ref[:, sl], k_ref[:, sl])
        p = jnp.exp(s - jnp.max(s, axis=1, keepdims=True))
        o = jnp.dot(p.astype(BF16), v_ref[:, sl], preferred_element_type=F32)
        o_ref[:, sl] = (o / jnp.sum(p, axis=1, keepdims=True)).astype(o_ref.dtype)


def _cross_attention(q, k, v, batch):
    t, width = q.shape
    seq, mem_len = t // batch, k.shape[0] // batch
    tq = _tile(seq, 512, 8)
    n_q = seq // tq
    return pl.pallas_call(
        functools.partial(_cross_attn_body, heads=width // X_DIM), grid=(batch, n_q),
        in_specs=[pl.BlockSpec((tq, width), lambda b, i: (b * n_q + i, 0)),
                  pl.BlockSpec((mem_len, width), lambda b, i: (b, 0)),
                  pl.BlockSpec((mem_len, width), lambda b, i: (b, 0))],
        out_specs=pl.BlockSpec((tq, width), lambda b, i: (b * n_q + i, 0)),
        out_shape=jax.ShapeDtypeStruct((t, width), BF16),
        compiler_params=_params(2), name="cross_attention")(q, k, v)


def _split_w_in_body(w_ref, lat_ref, sb_ref, *, lat):
    aligned = lat - MLA_ROPE
    w = w_ref[...]
    lat_ref[:, :aligned] = w[:, :aligned].astype(lat_ref.dtype)
    group = w[:, aligned:aligned + LANE]
    lane = lax.broadcasted_iota(jnp.int32, group.shape, 1)
    twice = jnp.where(lane < MLA_ROPE, group, pltpu.roll(group, MLA_ROPE, 1))
    lat_ref[:, aligned:] = twice.astype(lat_ref.dtype)
    sb_ref[...] = w[:, lat:].astype(sb_ref.dtype)


def _split_w_in(w_in, lat):
    layers, d, cols = w_in.shape
    tk = _tile(d, 256, 8)
    lat_cols = lat + MLA_ROPE
    return pl.pallas_call(
        functools.partial(_split_w_in_body, lat=lat), grid=(layers, d // tk),
        in_specs=[pl.BlockSpec((None, tk, cols), lambda l, i: (l, i, 0))],
        out_specs=[pl.BlockSpec((None, tk, lat_cols), lambda l, i: (l, i, 0)),
                   pl.BlockSpec((None, tk, cols - lat), lambda l, i: (l, i, 0))],
        out_shape=[jax.ShapeDtypeStruct((layers, d, lat_cols), BF16),
                   jax.ShapeDtypeStruct((layers, d, cols - lat), BF16)],
        compiler_params=_params(2), name="split_w_in")(w_in)


def _twice(g):
    return jnp.tile(g, 2).reshape(1, 2 * g.shape[0])


def _relaid_weights(w_in, w_q_up, w_kv_up, w_xkv, w_down, q_rank, kv_rank):
    assert 2 * MLA_ROPE == LANE and (q_rank + kv_rank) % LANE == 0
    layers, d = w_in.shape[:2]
    lat = q_rank + kv_rank + MLA_ROPE
    w_q = w_q_up.reshape(layers, q_rank, MLA_HEADS, MLA_NOPE + MLA_ROPE)
    w_q = jnp.concatenate([w_q, w_q[..., MLA_NOPE:]], axis=-1)
    w_kv = w_kv_up.reshape(layers, kv_rank, MLA_HEADS, MLA_NOPE + MLA_V)
    w_x = w_xkv.reshape(layers, d, X_HEADS, 2 * X_DIM)
    w_lat, w_sb = _split_w_in(w_in, lat)
    ws = dict(lat=w_lat, sb=w_sb,
              q=w_q.reshape(layers, q_rank, MLA_HEADS * QK_PAD),
              k=w_kv[..., :MLA_NOPE].reshape(layers, kv_rank, -1),
              v=w_kv[..., MLA_NOPE:].reshape(layers, kv_rank, -1),
              xkv=jnp.concatenate([w_x[..., :X_DIM].reshape(layers, d, -1),
                                   w_x[..., X_DIM:].reshape(layers, d, -1)], axis=2),
              down=w_down)
    return {name: w.astype(BF16) for name, w in ws.items()}


def kernel(x, mem, positions, g_attn, w_in, g_q_lat, g_kv_lat, w_q_up, w_kv_up, g_mla_q, g_mla_k,
           g_mla_out, g_sb_out, w_out, g_cross, g_mem, w_xq, w_xkv, g_xq, g_xk, w_xo, g_ffn,
           w_gate, w_up, w_down):
    batch, seq, d = x.shape
    depth = w_in.shape[0]
    q_rank, kv_rank = g_q_lat.shape[1], g_kv_lat.shape[1]
    x = x.reshape(batch * seq, d)
    mem2 = mem.reshape(-1, d)
    cos, sin = _rope_tables(positions)
    w = _relaid_weights(w_in, w_q_up, w_kv_up, w_xkv, w_down, q_rank, kv_rank)
    for l in range(depth):
        n, cq, ckv, k_rope = _norm_latent_proj(x, g_attn[l], w["lat"], l, g_q_lat[l], g_kv_lat[l],
                                               _twice(g_mla_k[l, MLA_NOPE:]), cos, sin)
        q = _q_up(cq, w["q"], l, g_mla_q[l, :MLA_NOPE].reshape(1, LANE),
                  _twice(g_mla_q[l, MLA_NOPE:]), cos, sin)
        k = _k_up(ckv, w["k"], l, g_mla_k[l, :MLA_NOPE].reshape(1, LANE), k_rope)
        v = _matmul_plain(ckv, w["v"], l, BF16, "v_up")
        o_mla = _mla_attention(q, k, v, batch)
        o_sb = _sb_attention(_sb_proj(n, w["sb"], l), batch)
        mixed = _mixnorm(o_mla, o_sb, g_mla_out[l], g_sb_out[l])
        x = _matmul_residual(mixed, w_out, l, x, "out_proj")
        xq = _norm_xq_proj(x, g_cross[l], w_xq, l, g_xq[l])
        xk, xv = _xkv_proj(_rmsnorm(mem2, g_mem[l], "norm_mem"), w["xkv"], l, g_xk[l])
        x, h = _matmul_residual_norm(_cross_attention(xq, xk, xv, batch), w_xo, l, x, g_ffn[l], "cross_out")
        x = _matmul_residual(_swiglu(h, w_gate, w_up, l), w["down"], l, x, "ffn_down", tm=512)
    return x.reshape(batch, seq, d)
```

```python
import functools
import math

import jax
import jax.numpy as jnp
from jax import lax
from jax.experimental import pallas as pl
from jax.experimental.pallas import tpu as pltpu

MLA_HEADS = 16
MLA_NOPE = 128
MLA_ROPE = 64
MLA_V = 128
SB_HEADS = 16
SB_DIM = 128
X_HEADS = 4
X_DIM = 128
ROPE_THETA = 10000.0
EPS = 1e-6

LANE = 128
QK_PAD = 2 * LANE
VMEM_LIMIT_BYTES = 56 * 1024 * 1024
ATTN_TILE = 256
LOG2_E = math.log2(math.e)
BF16 = jnp.bfloat16
F32 = jnp.float32


def _tile(dim, pref, align):
    if dim <= pref:
        return dim
    t = (pref // align) * align
    while t >= align:
        if dim % t == 0:
            return t
        t -= align
    return dim


def _params(ndims):
    return pltpu.CompilerParams(dimension_semantics=("arbitrary",) * ndims,
                                vmem_limit_bytes=VMEM_LIMIT_BYTES)


def _rms(y, width):
    ms = jnp.sum(y * y, axis=-1, keepdims=True) * (1.0 / width)
    return y * lax.rsqrt(ms + EPS)


def _rope(r, cos, sin):
    return r * cos + pltpu.roll(r, MLA_ROPE // 2, 1) * sin


def _rmsnorm_body(x_ref, g_ref, o_ref):
    x = x_ref[...]
    o_ref[...] = (_rms(x, x.shape[-1]) * g_ref[...]).astype(o_ref.dtype)


def _rmsnorm(x, g, name):
    m, d = x.shape
    tm = _tile(m, 256, 8)
    return pl.pallas_call(
        _rmsnorm_body, grid=(m // tm,),
        in_specs=[pl.BlockSpec((tm, d), lambda i: (i, 0)), pl.BlockSpec((1, d), lambda i: (0, 0))],
        out_specs=pl.BlockSpec((tm, d), lambda i: (i, 0)),
        out_shape=jax.ShapeDtypeStruct((m, d), BF16),
        compiler_params=_params(1), name=name)(x, g.reshape(1, d))


def _mixnorm_body(a_ref, b_ref, ga_ref, gb_ref, o_ref):
    wa = a_ref.shape[-1]
    a, b = a_ref[...], b_ref[...]
    o_ref[:, :wa] = (_rms(a, wa) * ga_ref[...]).astype(o_ref.dtype)
    o_ref[:, wa:] = (_rms(b, b.shape[-1]) * gb_ref[...]).astype(o_ref.dtype)


def _mixnorm(a, b, ga, gb):
    m, wa = a.shape
    wb = b.shape[1]
    tm = _tile(m, 256, 8)
    return pl.pallas_call(
        _mixnorm_body, grid=(m // tm,),
        in_specs=[pl.BlockSpec((tm, wa), lambda i: (i, 0)), pl.BlockSpec((tm, wb), lambda i: (i, 0)),
                  pl.BlockSpec((1, wa), lambda i: (0, 0)), pl.BlockSpec((1, wb), lambda i: (0, 0))],
        out_specs=pl.BlockSpec((tm, wa + wb), lambda i: (i, 0)),
        out_shape=jax.ShapeDtypeStruct((m, wa + wb), BF16),
        compiler_params=_params(1), name="mixnorm")(a, b, ga.reshape(1, wa), gb.reshape(1, wb))


def _rope_table_body(pos_ref, freq_ref, cos_ref, sin_ref):
    ang = pos_ref[...] * freq_ref[...]
    lane = lax.broadcasted_iota(jnp.int32, ang.shape, 1)
    sin = jnp.sin(ang)
    cos_ref[...] = jnp.where(lane < MLA_ROPE, jnp.cos(ang), 0.0)
    sin_ref[...] = jnp.where(lane < MLA_ROPE // 2, -sin, jnp.where(lane < MLA_ROPE, sin, 0.0))


def _rope_tables(positions):
    t = positions.size
    half = MLA_ROPE // 2
    inv_freq = ROPE_THETA ** (-jnp.arange(half, dtype=F32) / half)
    freq = jnp.concatenate([inv_freq, inv_freq, jnp.zeros((LANE - MLA_ROPE,), F32)]).reshape(1, LANE)
    pos = positions.astype(F32).reshape(t, 1)
    tm = _tile(t, 512, 8)
    return pl.pallas_call(
        _rope_table_body, grid=(t // tm,),
        in_specs=[pl.BlockSpec((tm, 1), lambda i: (i, 0)), pl.BlockSpec((1, LANE), lambda i: (0, 0))],
        out_specs=[pl.BlockSpec((tm, LANE), lambda i: (i, 0))] * 2,
        out_shape=[jax.ShapeDtypeStruct((t, LANE), F32)] * 2,
        compiler_params=_params(1), name="rope_tables")(pos, freq)


def _weight_spec(layer, k, tn):
    return pl.BlockSpec((None, k, tn), lambda i, j: (layer, 0, j))


def _mm_body(a_ref, w_ref, *rest, n_extra, epilogue):
    extras, outs = rest[:n_extra], rest[n_extra:]
    y = jnp.dot(a_ref[...], w_ref[...].astype(BF16), preferred_element_type=F32)
    epilogue(y, extras, outs)


def _matmul(a, w, layer, *, tm, tn, epilogue, out_shape, out_specs, extras=(), extra_specs=(), name):
    m, k = a.shape
    n = w.shape[2]
    body = functools.partial(_mm_body, n_extra=len(extras), epilogue=epilogue)
    return pl.pallas_call(
        body, grid=(m // tm, n // tn),
        in_specs=[pl.BlockSpec((tm, k), lambda i, j: (i, 0)), _weight_spec(layer, k, tn), *extra_specs],
        out_specs=out_specs, out_shape=out_shape,
        compiler_params=_params(2), name=name)(a, w, *extras)


def _row_spec(tm, width):
    return pl.BlockSpec((tm, width), lambda i, j: (i, 0))


def _const_spec(width):
    return pl.BlockSpec((1, width), lambda i, j: (0, 0))


def _tile_spec(tm, tn):
    return pl.BlockSpec((tm, tn), lambda i, j: (i, j))


def _store_epilogue(y, extras, outs):
    outs[0][...] = y.astype(outs[0].dtype)


def _residual_epilogue(y, extras, outs):
    outs[0][...] = extras[0][...] + y


def _matmul_plain(a, w, layer, out_dtype, name, tm=1024, tn=512):
    m, n = a.shape[0], w.shape[2]
    tm, tn = _tile(m, tm, 8), _tile(n, tn, LANE)
    return _matmul(a, w, layer, tm=tm, tn=tn, epilogue=_store_epilogue,
                   out_shape=jax.ShapeDtypeStruct((m, n), out_dtype),
                   out_specs=_tile_spec(tm, tn), name=name)


def _matmul_residual(a, w, layer, res, name, tm=1024, tn=512):
    m, n = a.shape[0], w.shape[2]
    tm, tn = _tile(m, tm, 8), _tile(n, tn, LANE)
    return _matmul(a, w, layer, tm=tm, tn=tn, epilogue=_residual_epilogue,
                   extras=(res,), extra_specs=(_tile_spec(tm, tn),),
                   out_shape=jax.ShapeDtypeStruct((m, n), F32),
                   out_specs=_tile_spec(tm, tn), name=name)


def _latent_epilogue(y, extras, outs, *, q_rank, kv_rank):
    gq_ref, gkv_ref, gkr_ref, cos_ref, sin_ref = extras
    cq_ref, ckv_ref, kr_ref = outs
    cq_ref[...] = (_rms(y[:, :q_rank], q_rank) * gq_ref[...]).astype(cq_ref.dtype)
    ckv = y[:, q_rank:q_rank + kv_rank]
    ckv_ref[...] = (_rms(ckv, kv_rank) * gkv_ref[...]).astype(ckv_ref.dtype)
    group = y[:, q_rank + kv_rank:]
    lane = lax.broadcasted_iota(jnp.int32, group.shape, 1)
    twice = jnp.where(lane < MLA_ROPE, group, pltpu.roll(group, MLA_ROPE, 1))
    kr = _rms(twice, 2 * MLA_ROPE) * gkr_ref[...]
    kr_ref[...] = _rope(kr, cos_ref[...], sin_ref[...]).astype(kr_ref.dtype)


def _norm_latent_body(x_ref, g_ref, w_ref, gq_ref, gkv_ref, gkr_ref, cos_ref, sin_ref,
                      n_ref, cq_ref, ckv_ref, kr_ref, *, q_rank, kv_rank):
    x = x_ref[...]
    n = (_rms(x, x.shape[-1]) * g_ref[...]).astype(BF16)
    n_ref[...] = n
    y = _dot_nt(n, w_ref[...])
    _latent_epilogue(y, (gq_ref, gkv_ref, gkr_ref, cos_ref, sin_ref), (cq_ref, ckv_ref, kr_ref),
                     q_rank=q_rank, kv_rank=kv_rank)


def _norm_latent_proj(x, g_attn, w_lat_t, layer, g_q_lat, g_kv_lat, g_k_rope, cos, sin):
    m, d = x.shape
    q_rank, kv_rank = g_q_lat.shape[0], g_kv_lat.shape[0]
    n = w_lat_t.shape[1]
    tm = _tile(m, 256, 8)
    row = lambda width: pl.BlockSpec((tm, width), lambda i: (i, 0))
    const = lambda width: pl.BlockSpec((1, width), lambda i: (0, 0))
    return pl.pallas_call(
        functools.partial(_norm_latent_body, q_rank=q_rank, kv_rank=kv_rank), grid=(m // tm,),
        in_specs=[row(d), const(d), pl.BlockSpec((None, n, d), lambda i: (layer, 0, 0)),
                  const(q_rank), const(kv_rank), const(LANE), row(LANE), row(LANE)],
        out_specs=[row(d), row(q_rank), row(kv_rank), row(LANE)],
        out_shape=[jax.ShapeDtypeStruct((m, d), BF16), jax.ShapeDtypeStruct((m, q_rank), BF16),
                   jax.ShapeDtypeStruct((m, kv_rank), BF16), jax.ShapeDtypeStruct((m, LANE), BF16)],
        compiler_params=_params(1), name="norm_latent_proj")(
            x, g_attn.reshape(1, d), w_lat_t, g_q_lat.reshape(1, -1), g_kv_lat.reshape(1, -1), g_k_rope, cos, sin)


def _sb_proj_body(a_ref, w_ref, o_ref, *, q_tiles, scale):
    j = pl.program_id(1)
    y = _dot_nt(a_ref[...], w_ref[0].astype(BF16))
    o_ref[...] = (y * jnp.where(j < q_tiles, scale, 1.0)).astype(o_ref.dtype)


def _sb_proj(a, w_in_t, layer, first):
    m, k = a.shape
    n = w_in_t.shape[1] - first
    tm, tn = _tile(m, 1024, 8), _tile(n // 3, 512, LANE)
    body = functools.partial(_sb_proj_body, q_tiles=(n // 3) // tn, scale=LOG2_E / math.sqrt(SB_DIM))
    return pl.pallas_call(
        body, grid=(m // tm, n // tn),
        in_specs=[pl.BlockSpec((tm, k), lambda i, j: (i, 0)),
                  pl.BlockSpec((pl.Element(1), pl.Element(tn), pl.Element(k)),
                               lambda i, j: (layer, pl.multiple_of(first + j * tn, 8), 0))],
        out_specs=_tile_spec(tm, tn), out_shape=jax.ShapeDtypeStruct((m, n), BF16),
        compiler_params=_params(2), name="sb_proj")(a, w_in_t)


def _q_up_body(cq_ref, w_ref, gn_ref, gr_ref, cos_ref, sin_ref, o_ref, *, heads, scale):
    cq = cq_ref[...]
    cos, sin = cos_ref[...] * scale, sin_ref[...] * scale
    gn = gn_ref[...] * scale
    for h in range(heads):
        lo = h * QK_PAD
        y = jnp.dot(cq, w_ref[:, lo:lo + QK_PAD], preferred_element_type=F32)
        o_ref[:, lo:lo + MLA_NOPE] = (_rms(y[:, :MLA_NOPE], MLA_NOPE) * gn).astype(o_ref.dtype)
        r = _rms(y[:, MLA_NOPE:], 2 * MLA_ROPE) * gr_ref[...]
        o_ref[:, lo + MLA_NOPE:lo + QK_PAD] = _rope(r, cos, sin).astype(o_ref.dtype)


def _q_up(cq, w_q, layer, g_nope, g_rope, cos, sin):
    m, k = cq.shape
    n = w_q.shape[2]
    tm, heads = _tile(m, 1024, 8), 4
    tn = heads * QK_PAD
    body = functools.partial(_q_up_body, heads=heads, scale=LOG2_E / math.sqrt(MLA_NOPE + MLA_ROPE))
    return pl.pallas_call(
        body, grid=(m // tm, n // tn),
        in_specs=[pl.BlockSpec((tm, k), lambda i, j: (i, 0)), _weight_spec(layer, k, tn),
                  _const_spec(LANE), _const_spec(LANE), _row_spec(tm, LANE), _row_spec(tm, LANE)],
        out_specs=_tile_spec(tm, tn), out_shape=jax.ShapeDtypeStruct((m, n), BF16),
        compiler_params=_params(2), name="q_up")(cq, w_q, g_nope, g_rope, cos, sin)


def _k_up_epilogue(y, extras, outs, *, heads):
    gn_ref, kr_ref = extras
    o_ref = outs[0]
    for h in range(heads):
        k = _rms(y[:, h * MLA_NOPE:(h + 1) * MLA_NOPE], MLA_NOPE) * gn_ref[...]
        o_ref[:, h * QK_PAD:h * QK_PAD + MLA_NOPE] = k.astype(o_ref.dtype)
        o_ref[:, h * QK_PAD + MLA_NOPE:(h + 1) * QK_PAD] = kr_ref[...]


def _k_up(ckv, w_k, layer, g_nope, k_rope):
    m, n = ckv.shape[0], w_k.shape[2]
    tm, heads = _tile(m, 1024, 8), 4
    tn = heads * MLA_NOPE
    return _matmul(
        ckv, w_k, layer, tm=tm, tn=tn, epilogue=functools.partial(_k_up_epilogue, heads=heads),
        extras=(g_nope, k_rope), extra_specs=(_const_spec(LANE), _row_spec(tm, LANE)),
        out_shape=jax.ShapeDtypeStruct((m, (n // MLA_NOPE) * QK_PAD), BF16),
        out_specs=_tile_spec(tm, heads * QK_PAD), name="k_up")


def _headnorm_epilogue(y, extras, outs, *, heads, scale):
    g_ref = extras[0]
    for h in range(heads):
        sl = slice(h * X_DIM, (h + 1) * X_DIM)
        outs[0][:, sl] = (_rms(y[:, sl], X_DIM) * g_ref[...] * scale).astype(outs[0].dtype)


def _norm_xq_body(x_ref, g_ref, w_ref, gq_ref, o_ref, *, heads, scale):
    x = x_ref[...]
    xn = (_rms(x, x.shape[-1]) * g_ref[...]).astype(BF16)
    y = jnp.dot(xn, w_ref[...].astype(BF16), preferred_element_type=F32)
    _headnorm_epilogue(y, (gq_ref,), (o_ref,), heads=heads, scale=scale)


def _norm_xq_proj(x, g_cross, w_xq, layer, g_xq):
    m, d = x.shape
    n = w_xq.shape[2]
    tm = _tile(m, 512, 8)
    body = functools.partial(_norm_xq_body, heads=n // X_DIM, scale=1.0 / math.sqrt(X_DIM))
    return pl.pallas_call(
        body, grid=(m // tm,),
        in_specs=[pl.BlockSpec((tm, d), lambda i: (i, 0)), pl.BlockSpec((1, d), lambda i: (0, 0)),
                  pl.BlockSpec((None, d, n), lambda i: (layer, 0, 0)),
                  pl.BlockSpec((1, X_DIM), lambda i: (0, 0))],
        out_specs=pl.BlockSpec((tm, n), lambda i: (i, 0)),
        out_shape=jax.ShapeDtypeStruct((m, n), BF16),
        compiler_params=_params(1), name="norm_xq_proj")(x, g_cross.reshape(1, d), w_xq, g_xq.reshape(1, X_DIM))


def _out_norm_body(a_ref, w_ref, res_ref, g_ref, x_ref, h_ref):
    x = res_ref[...] + jnp.dot(a_ref[...], w_ref[...].astype(BF16), preferred_element_type=F32)
    x_ref[...] = x
    h_ref[...] = (_rms(x, x.shape[-1]) * g_ref[...]).astype(h_ref.dtype)


def _matmul_residual_norm(a, w, layer, res, g, name):
    m, k = a.shape
    n = w.shape[2]
    tm = _tile(m, 256, 8)
    return pl.pallas_call(
        _out_norm_body, grid=(m // tm,),
        in_specs=[pl.BlockSpec((tm, k), lambda i: (i, 0)), pl.BlockSpec((None, k, n), lambda i: (layer, 0, 0)),
                  pl.BlockSpec((tm, n), lambda i: (i, 0)), pl.BlockSpec((1, n), lambda i: (0, 0))],
        out_specs=[pl.BlockSpec((tm, n), lambda i: (i, 0))] * 2,
        out_shape=[jax.ShapeDtypeStruct((m, n), F32), jax.ShapeDtypeStruct((m, n), BF16)],
        compiler_params=_params(1), name=name)(a, w, res, g.reshape(1, n))


def _xkv_epilogue(y, extras, outs, *, heads):
    g_ref = extras[0]
    k_ref, v_ref = outs
    width = heads * X_DIM
    for h in range(heads):
        sl = slice(h * X_DIM, (h + 1) * X_DIM)
        k_ref[:, sl] = (_rms(y[:, sl], X_DIM) * g_ref[...]).astype(k_ref.dtype)
    v_ref[...] = y[:, width:].astype(v_ref.dtype)


def _xkv_proj(a, w_xkv, layer, g_xk):
    m, n = a.shape[0], w_xkv.shape[2]
    tm, width = _tile(m, 512, 8), n // 2
    return _matmul(
        a, w_xkv, layer, tm=tm, tn=n, epilogue=functools.partial(_xkv_epilogue, heads=width // X_DIM),
        extras=(g_xk.reshape(1, X_DIM),), extra_specs=(_const_spec(X_DIM),),
        out_shape=[jax.ShapeDtypeStruct((m, width), BF16)] * 2,
        out_specs=[_row_spec(tm, width)] * 2, name="xkv_proj")


def _swiglu_body(a_ref, wg_ref, wu_ref, o_ref):
    a = a_ref[...]
    g = jnp.dot(a, wg_ref[...].astype(BF16), preferred_element_type=F32)
    u = jnp.dot(a, wu_ref[...].astype(BF16), preferred_element_type=F32)
    o_ref[...] = (g / (1.0 + jnp.exp(-g)) * u).astype(o_ref.dtype)


def _swiglu(a, w_gate, w_up, layer):
    m, k = a.shape
    n = w_gate.shape[2]
    tm, tn = _tile(m, 1024, 8), _tile(n, 256, LANE)
    return pl.pallas_call(
        _swiglu_body, grid=(m // tm, n // tn),
        in_specs=[pl.BlockSpec((tm, k), lambda i, j: (i, 0)),
                  _weight_spec(layer, k, tn), _weight_spec(layer, k, tn)],
        out_specs=_tile_spec(tm, tn), out_shape=jax.ShapeDtypeStruct((m, n), BF16),
        compiler_params=_params(2), name="swiglu")(a, w_gate, w_up)


def _dot_nt(a, b):
    return lax.dot_general(a, b, (((1,), (1,)), ((), ())), preferred_element_type=F32)


def _mla_attn_body(q_ref, k_ref, v_ref, o_ref, *, tile):
    n_tiles = q_ref.shape[0] // tile
    row = lax.broadcasted_iota(jnp.int32, (tile, tile), 0)
    col = lax.broadcasted_iota(jnp.int32, (tile, tile), 1)
    for qi in range(n_tiles):
        lo, hi = qi * tile, (qi + 1) * tile
        q = q_ref[lo:hi, :]
        s_diag = jnp.where(col <= row, _dot_nt(q, k_ref[lo:hi, :]), -jnp.inf)
        m = jnp.max(s_diag, axis=1, keepdims=True)
        if qi:
            s_off = _dot_nt(q, k_ref[0:lo, :])
            m = jnp.maximum(m, jnp.max(s_off, axis=1, keepdims=True))
        p_diag = jnp.exp2(s_diag - m)
        l = jnp.sum(p_diag, axis=1, keepdims=True)
        acc = jnp.dot(p_diag.astype(BF16), v_ref[lo:hi, :], preferred_element_type=F32)
        if qi:
            p_off = jnp.exp2(s_off - m)
            l = l + jnp.sum(p_off, axis=1, keepdims=True)
            acc = acc + jnp.dot(p_off.astype(BF16), v_ref[0:lo, :], preferred_element_type=F32)
        o_ref[lo:hi, :] = acc / l


def _mla_attention(q, k, v, batch):
    t = q.shape[0]
    seq = t // batch
    heads = q.shape[1] // QK_PAD
    tile = _tile(seq, ATTN_TILE, 8)
    return pl.pallas_call(
        functools.partial(_mla_attn_body, tile=tile), grid=(batch, heads),
        in_specs=[pl.BlockSpec((seq, QK_PAD), lambda b, h: (b, h)),
                  pl.BlockSpec((seq, QK_PAD), lambda b, h: (b, h)),
                  pl.BlockSpec((seq, MLA_V), lambda b, h: (b, h))],
        out_specs=pl.BlockSpec((seq, MLA_V), lambda b, h: (b, h)),
        out_shape=jax.ShapeDtypeStruct((t, heads * MLA_V), F32),
        compiler_params=_params(2), name="mla_attention")(q, k, v)


def _sb_attn_body(q_ref, k_ref, v_ref, o_ref, *, tile):
    n_tiles = q_ref.shape[0] // tile
    row = lax.broadcasted_iota(jnp.int32, (tile, tile), 0)
    col = lax.broadcasted_iota(jnp.int32, (tile, tile), 1)
    strict = col < row
    ones_below = jnp.where(row > col, 1.0, 0.0).astype(BF16)

    def logs(z):
        log_beta = jnp.minimum(z, 0.0) - jnp.log2(1.0 + jnp.exp2(-jnp.abs(z)))
        return log_beta, log_beta - z

    def suffix_sums(log_keep):
        return jnp.dot(log_keep.astype(BF16), ones_below, preferred_element_type=F32)

    for qi in range(n_tiles):
        lo_row, hi_row = qi * tile, (qi + 1) * tile
        q = q_ref[lo_row:hi_row, :]
        log_beta, log_keep = logs(_dot_nt(q, k_ref[lo_row:hi_row, :]))
        log_keep = jnp.where(strict, log_keep, 0.0)
        a = jnp.where(strict, jnp.exp2(log_beta + suffix_sums(log_keep)), 0.0)
        acc = jnp.dot(a.astype(BF16), v_ref[lo_row:hi_row, :], preferred_element_type=F32)
        if qi:
            z = jnp.concatenate([_dot_nt(q, k_ref[c * tile:(c + 1) * tile, :]) for c in range(qi)], axis=0)
            log_beta, keep = logs(z)
            totals = jnp.sum(keep, axis=1, keepdims=True)
            run = jnp.sum(log_keep, axis=1, keepdims=True)
            carries = [None] * qi
            for c in reversed(range(qi)):
                carries[c] = run
                run = run + totals[c * tile:(c + 1) * tile]
            a = jnp.exp2(log_beta + suffix_sums(keep) + jnp.concatenate(carries, axis=0)).astype(BF16)
            a = jnp.concatenate([a[c * tile:(c + 1) * tile] for c in range(qi)], axis=1)
            acc = acc + jnp.dot(a, v_ref[0:lo_row, :], preferred_element_type=F32)
        o_ref[lo_row:hi_row, :] = acc


def _sb_attention(qkv, batch):
    t = qkv.shape[0]
    seq = t // batch
    heads = qkv.shape[1] // (3 * SB_DIM)
    tile = _tile(seq, ATTN_TILE, 8)
    return pl.pallas_call(
        functools.partial(_sb_attn_body, tile=tile), grid=(batch, heads),
        in_specs=[pl.BlockSpec((seq, SB_DIM), lambda b, h: (b, h)),
                  pl.BlockSpec((seq, SB_DIM), lambda b, h: (b, heads + h)),
                  pl.BlockSpec((seq, SB_DIM), lambda b, h: (b, 2 * heads + h))],
        out_specs=pl.BlockSpec((seq, SB_DIM), lambda b, h: (b, h)),
        out_shape=jax.ShapeDtypeStruct((t, heads * SB_DIM), F32),
        compiler_params=_params(2), name="sb_attention")(qkv, qkv, qkv)


def _cross_attn_body(q_ref, k_ref, v_ref, o_ref, *, heads):
    for h in range(heads):
        sl = slice(h * X_DIM, (h + 1) * X_DIM)
        s = _dot_nt(q_ref[:, sl], k_ref[:, sl])
        p = jnp.exp(s - jnp.max(s, axis=1, keepdims=True))
        o = jnp.dot(p.astype(BF16), v_ref[:, sl], preferred_element_type=F32)
        o_ref[:, sl] = (o / jnp.sum(p, axis=1, keepdims=True)).astype(o_ref.dtype)


def _cross_attention(q, k, v, batch):
    t, width = q.shape
    seq, mem_len = t // batch, k.shape[0] // batch
    tq = _tile(seq, 512, 8)
    n_q = seq // tq
    return pl.pallas_call(
        functools.partial(_cross_attn_body, heads=width // X_DIM), grid=(batch, n_q),
        in_specs=[pl.BlockSpec((tq, width), lambda b, i: (b * n_q + i, 0)),
                  pl.BlockSpec((mem_len, width), lambda b, i: (b, 0)),
                  pl.BlockSpec((mem_len, width), lambda b, i: (b, 0))],
        out_specs=pl.BlockSpec((tq, width), lambda b, i: (b * n_q + i, 0)),
        out_shape=jax.ShapeDtypeStruct((t, width), BF16),
        compiler_params=_params(2), name="cross_attention")(q, k, v)


def _twice(g):
    return jnp.tile(g, 2).reshape(1, 2 * g.shape[0])


def _relaid_weights(w_in_t, w_q_up, w_kv_up, w_xkv, w_down, q_rank, kv_rank):
    assert 2 * MLA_ROPE == LANE and (q_rank + kv_rank) % LANE == 0
    layers, d = w_in_t.shape[0], w_in_t.shape[2]
    w_q = w_q_up.reshape(layers, q_rank, MLA_HEADS, MLA_NOPE + MLA_ROPE)
    w_q = jnp.concatenate([w_q, w_q[..., MLA_NOPE:]], axis=-1)
    w_kv = w_kv_up.reshape(layers, kv_rank, MLA_HEADS, MLA_NOPE + MLA_V)
    w_x = w_xkv.reshape(layers, d, X_HEADS, 2 * X_DIM)
    ws = dict(lat_t=w_in_t[:, :q_rank + kv_rank + LANE],
              q=w_q.reshape(layers, q_rank, MLA_HEADS * QK_PAD),
              k=w_kv[..., :MLA_NOPE].reshape(layers, kv_rank, -1),
              v=w_kv[..., MLA_NOPE:].reshape(layers, kv_rank, -1),
              xkv=jnp.concatenate([w_x[..., :X_DIM].reshape(layers, d, -1),
                                   w_x[..., X_DIM:].reshape(layers, d, -1)], axis=2),
              down=w_down)
    return {name: w.astype(BF16) for name, w in ws.items()}


def kernel(x, mem, positions, g_attn, w_in, g_q_lat, g_kv_lat, w_q_up, w_kv_up, g_mla_q, g_mla_k,
           g_mla_out, g_sb_out, w_out, g_cross, g_mem, w_xq, w_xkv, g_xq, g_xk, w_xo, g_ffn,
           w_gate, w_up, w_down):
    batch, seq, d = x.shape
    depth = w_in.shape[0]
    q_rank, kv_rank = g_q_lat.shape[1], g_kv_lat.shape[1]
    x = x.reshape(batch * seq, d)
    mem2 = mem.reshape(-1, d)
    cos, sin = _rope_tables(positions)
    w_in_t = jnp.swapaxes(w_in, 1, 2)
    w = _relaid_weights(w_in_t, w_q_up, w_kv_up, w_xkv, w_down, q_rank, kv_rank)
    for l in range(depth):
        n, cq, ckv, k_rope = _norm_latent_proj(x, g_attn[l], w["lat_t"], l, g_q_lat[l], g_kv_lat[l],
                                               _twice(g_mla_k[l, MLA_NOPE:]), cos, sin)
        q = _q_up(cq, w["q"], l, g_mla_q[l, :MLA_NOPE].reshape(1, LANE),
                  _twice(g_mla_q[l, MLA_NOPE:]), cos, sin)
        k = _k_up(ckv, w["k"], l, g_mla_k[l, :MLA_NOPE].reshape(1, LANE), k_rope)
        v = _matmul_plain(ckv, w["v"], l, BF16, "v_up")
        o_mla = _mla_attention(q, k, v, batch)
        o_sb = _sb_attention(_sb_proj(n, w_in_t, l, q_rank + kv_rank + MLA_ROPE), batch)
        mixed = _mixnorm(o_mla, o_sb, g_mla_out[l], g_sb_out[l])
        x = _matmul_residual(mixed, w_out, l, x, "out_proj")
        xq = _norm_xq_proj(x, g_cross[l], w_xq, l, g_xq[l])
        xk, xv = _xkv_proj(_rmsnorm(mem2, g_mem[l], "norm_mem"), w["xkv"], l, g_xk[l])
        x, h = _matmul_residual_norm(_cross_attention(xq, xk, xv, batch), w_xo, l, x, g_ffn[l], "cross_out")
        x = _matmul_residual(_swiglu(h, w_gate, w_up, l), w["down"], l, x, "ffn_down", tm=512)
    return x.reshape(batch, seq, d)
```

```python
import functools
import math

import jax
import jax.numpy as jnp
from jax import lax
from jax.experimental import pallas as pl
from jax.experimental.pallas import tpu as pltpu

MLA_HEADS = 16
MLA_NOPE = 128
MLA_ROPE = 64
MLA_V = 128
SB_HEADS = 16
SB_DIM = 128
X_HEADS = 4
X_DIM = 128
ROPE_THETA = 10000.0
EPS = 1e-6

LANE = 128
QK_PAD = 2 * LANE
VMEM_LIMIT_BYTES = 56 * 1024 * 1024
ATTN_TILE = 256
LOG2_E = math.log2(math.e)
BF16 = jnp.bfloat16
F32 = jnp.float32


def _tile(dim, pref, align):
    if dim <= pref:
        return dim
    t = (pref // align) * align
    while t >= align:
        if dim % t == 0:
            return t
        t -= align
    return dim


def _params(ndims):
    return pltpu.CompilerParams(dimension_semantics=("arbitrary",) * ndims,
                                vmem_limit_bytes=VMEM_LIMIT_BYTES)


def _rms(y, width):
    ms = jnp.sum(y * y, axis=-1, keepdims=True) * (1.0 / width)
    return y * lax.rsqrt(ms + EPS)


def _rope(r, cos, sin):
    return r * cos + pltpu.roll(r, MLA_ROPE // 2, 1) * sin


def _rmsnorm_body(x_ref, g_ref, o_ref):
    x = x_ref[...]
    o_ref[...] = (_rms(x, x.shape[-1]) * g_ref[...]).astype(o_ref.dtype)


def _rmsnorm(x, g, name):
    m, d = x.shape
    tm = _tile(m, 256, 8)
    return pl.pallas_call(
        _rmsnorm_body, grid=(m // tm,),
        in_specs=[pl.BlockSpec((tm, d), lambda i: (i, 0)), pl.BlockSpec((1, d), lambda i: (0, 0))],
        out_specs=pl.BlockSpec((tm, d), lambda i: (i, 0)),
        out_shape=jax.ShapeDtypeStruct((m, d), BF16),
        compiler_params=_params(1), name=name)(x, g.reshape(1, d))


def _mixnorm_body(a_ref, b_ref, ga_ref, gb_ref, o_ref):
    wa = a_ref.shape[-1]
    a, b = a_ref[...], b_ref[...]
    o_ref[:, :wa] = (_rms(a, wa) * ga_ref[...]).astype(o_ref.dtype)
    o_ref[:, wa:] = (_rms(b, b.shape[-1]) * gb_ref[...]).astype(o_ref.dtype)


def _mixnorm(a, b, ga, gb):
    m, wa = a.shape
    wb = b.shape[1]
    tm = _tile(m, 256, 8)
    return pl.pallas_call(
        _mixnorm_body, grid=(m // tm,),
        in_specs=[pl.BlockSpec((tm, wa), lambda i: (i, 0)), pl.BlockSpec((tm, wb), lambda i: (i, 0)),
                  pl.BlockSpec((1, wa), lambda i: (0, 0)), pl.BlockSpec((1, wb), lambda i: (0, 0))],
        out_specs=pl.BlockSpec((tm, wa + wb), lambda i: (i, 0)),
        out_shape=jax.ShapeDtypeStruct((m, wa + wb), BF16),
        compiler_params=_params(1), name="mixnorm")(a, b, ga.reshape(1, wa), gb.reshape(1, wb))


def _rope_table_body(pos_ref, freq_ref, cos_ref, sin_ref):
    ang = pos_ref[...] * freq_ref[...]
    lane = lax.broadcasted_iota(jnp.int32, ang.shape, 1)
    sin = jnp.sin(ang)
    cos_ref[...] = jnp.where(lane < MLA_ROPE, jnp.cos(ang), 0.0)
    sin_ref[...] = jnp.where(lane < MLA_ROPE // 2, -sin, jnp.where(lane < MLA_ROPE, sin, 0.0))


def _rope_tables(positions):
    t = positions.size
    half = MLA_ROPE // 2
    inv_freq = ROPE_THETA ** (-jnp.arange(half, dtype=F32) / half)
    freq = jnp.concatenate([inv_freq, inv_freq, jnp.zeros((LANE - MLA_ROPE,), F32)]).reshape(1, LANE)
    pos = positions.astype(F32).reshape(t, 1)
    tm = _tile(t, 512, 8)
    return pl.pallas_call(
        _rope_table_body, grid=(t // tm,),
        in_specs=[pl.BlockSpec((tm, 1), lambda i: (i, 0)), pl.BlockSpec((1, LANE), lambda i: (0, 0))],
        out_specs=[pl.BlockSpec((tm, LANE), lambda i: (i, 0))] * 2,
        out_shape=[jax.ShapeDtypeStruct((t, LANE), F32)] * 2,
        compiler_params=_params(1), name="rope_tables")(pos, freq)


def _weight_spec(layer, k, tn):
    return pl.BlockSpec((None, k, tn), lambda i, j: (layer, 0, j))


def _mm_body(a_ref, w_ref, *rest, n_extra, epilogue):
    extras, outs = rest[:n_extra], rest[n_extra:]
    y = jnp.dot(a_ref[...], w_ref[...].astype(BF16), preferred_element_type=F32)
    epilogue(y, extras, outs)


def _matmul(a, w, layer, *, tm, tn, epilogue, out_shape, out_specs, extras=(), extra_specs=(), name):
    m, k = a.shape
    n = w.shape[2]
    body = functools.partial(_mm_body, n_extra=len(extras), epilogue=epilogue)
    return pl.pallas_call(
        body, grid=(m // tm, n // tn),
        in_specs=[pl.BlockSpec((tm, k), lambda i, j: (i, 0)), _weight_spec(layer, k, tn), *extra_specs],
        out_specs=out_specs, out_shape=out_shape,
        compiler_params=_params(2), name=name)(a, w, *extras)


def _row_spec(tm, width):
    return pl.BlockSpec((tm, width), lambda i, j: (i, 0))


def _const_spec(width):
    return pl.BlockSpec((1, width), lambda i, j: (0, 0))


def _tile_spec(tm, tn):
    return pl.BlockSpec((tm, tn), lambda i, j: (i, j))


def _store_epilogue(y, extras, outs):
    outs[0][...] = y.astype(outs[0].dtype)


def _residual_epilogue(y, extras, outs):
    outs[0][...] = extras[0][...] + y


def _matmul_plain(a, w, layer, out_dtype, name, tm=1024, tn=512):
    m, n = a.shape[0], w.shape[2]
    tm, tn = _tile(m, tm, 8), _tile(n, tn, LANE)
    return _matmul(a, w, layer, tm=tm, tn=tn, epilogue=_store_epilogue,
                   out_shape=jax.ShapeDtypeStruct((m, n), out_dtype),
                   out_specs=_tile_spec(tm, tn), name=name)


def _matmul_residual(a, w, layer, res, name, tm=1024, tn=512):
    m, n = a.shape[0], w.shape[2]
    tm, tn = _tile(m, tm, 8), _tile(n, tn, LANE)
    return _matmul(a, w, layer, tm=tm, tn=tn, epilogue=_residual_epilogue,
                   extras=(res,), extra_specs=(_tile_spec(tm, tn),),
                   out_shape=jax.ShapeDtypeStruct((m, n), F32),
                   out_specs=_tile_spec(tm, tn), name=name)


def _latent_epilogue(y, extras, outs, *, q_rank, kv_rank):
    gq_ref, gkv_ref, gkr_ref, cos_ref, sin_ref = extras
    cq_ref, ckv_ref, kr_ref = outs
    cq_ref[...] = (_rms(y[:, :q_rank], q_rank) * gq_ref[...]).astype(cq_ref.dtype)
    ckv = y[:, q_rank:q_rank + kv_rank]
    ckv_ref[...] = (_rms(ckv, kv_rank) * gkv_ref[...]).astype(ckv_ref.dtype)
    group = y[:, q_rank + kv_rank:]
    lane = lax.broadcasted_iota(jnp.int32, group.shape, 1)
    twice = jnp.where(lane < MLA_ROPE, group, pltpu.roll(group, MLA_ROPE, 1))
    kr = _rms(twice, 2 * MLA_ROPE) * gkr_ref[...]
    kr_ref[...] = _rope(kr, cos_ref[...], sin_ref[...]).astype(kr_ref.dtype)


def _norm_latent_body(x_ref, g_ref, w_ref, gq_ref, gkv_ref, gkr_ref, cos_ref, sin_ref,
                      n_ref, cq_ref, ckv_ref, kr_ref, *, q_rank, kv_rank):
    x = x_ref[...]
    n = (_rms(x, x.shape[-1]) * g_ref[...]).astype(BF16)
    n_ref[...] = n
    y = _dot_nt(n, w_ref[...])
    _latent_epilogue(y, (gq_ref, gkv_ref, gkr_ref, cos_ref, sin_ref), (cq_ref, ckv_ref, kr_ref),
                     q_rank=q_rank, kv_rank=kv_rank)


def _norm_latent_proj(x, g_attn, w_lat_t, layer, g_q_lat, g_kv_lat, g_k_rope, cos, sin):
    m, d = x.shape
    q_rank, kv_rank = g_q_lat.shape[0], g_kv_lat.shape[0]
    n = w_lat_t.shape[1]
    tm = _tile(m, 256, 8)
    row = lambda width: pl.BlockSpec((tm, width), lambda i: (i, 0))
    const = lambda width: pl.BlockSpec((1, width), lambda i: (0, 0))
    return pl.pallas_call(
        functools.partial(_norm_latent_body, q_rank=q_rank, kv_rank=kv_rank), grid=(m // tm,),
        in_specs=[row(d), const(d), pl.BlockSpec((None, n, d), lambda i: (layer, 0, 0)),
                  const(q_rank), const(kv_rank), const(LANE), row(LANE), row(LANE)],
        out_specs=[row(d), row(q_rank), row(kv_rank), row(LANE)],
        out_shape=[jax.ShapeDtypeStruct((m, d), BF16), jax.ShapeDtypeStruct((m, q_rank), BF16),
                   jax.ShapeDtypeStruct((m, kv_rank), BF16), jax.ShapeDtypeStruct((m, LANE), BF16)],
        compiler_params=_params(1), name="norm_latent_proj")(
            x, g_attn.reshape(1, d), w_lat_t, g_q_lat.reshape(1, -1), g_kv_lat.reshape(1, -1), g_k_rope, cos, sin)


def _sb_proj_body(a_ref, w_ref, o_ref, *, q_tiles, scale):
    j = pl.program_id(1)
    y = _dot_nt(a_ref[...], w_ref[0].astype(BF16))
    o_ref[...] = (y * jnp.where(j < q_tiles, scale, 1.0)).astype(o_ref.dtype)


def _sb_proj(a, w_in_t, layer, first):
    m, k = a.shape
    n = w_in_t.shape[1] - first
    tm, tn = _tile(m, 1024, 8), _tile(n // 3, 512, LANE)
    body = functools.partial(_sb_proj_body, q_tiles=(n // 3) // tn, scale=LOG2_E / math.sqrt(SB_DIM))
    return pl.pallas_call(
        body, grid=(m // tm, n // tn),
        in_specs=[pl.BlockSpec((tm, k), lambda i, j: (i, 0)),
                  pl.BlockSpec((pl.Element(1), pl.Element(tn), pl.Element(k)),
                               lambda i, j: (layer, pl.multiple_of(first + j * tn, 8), 0))],
        out_specs=_tile_spec(tm, tn), out_shape=jax.ShapeDtypeStruct((m, n), BF16),
        compiler_params=_params(2), name="sb_proj")(a, w_in_t)


def _q_up_body(cq_ref, w_ref, gn_ref, gr_ref, cos_ref, sin_ref, o_ref, *, heads, scale):
    cq = cq_ref[...]
    cos, sin = cos_ref[...] * scale, sin_ref[...] * scale
    gn = gn_ref[...] * scale
    for h in range(heads):
        lo = h * QK_PAD
        y = jnp.dot(cq, w_ref[:, lo:lo + QK_PAD], preferred_element_type=F32)
        o_ref[:, lo:lo + MLA_NOPE] = (_rms(y[:, :MLA_NOPE], MLA_NOPE) * gn).astype(o_ref.dtype)
        r = _rms(y[:, MLA_NOPE:], 2 * MLA_ROPE) * gr_ref[...]
        o_ref[:, lo + MLA_NOPE:lo + QK_PAD] = _rope(r, cos, sin).astype(o_ref.dtype)


def _q_up(cq, w_q, layer, g_nope, g_rope, cos, sin):
    m, k = cq.shape
    n = w_q.shape[2]
    tm, heads = _tile(m, 1024, 8), 4
    tn = heads * QK_PAD
    body = functools.partial(_q_up_body, heads=heads, scale=LOG2_E / math.sqrt(MLA_NOPE + MLA_ROPE))
    return pl.pallas_call(
        body, grid=(m // tm, n // tn),
        in_specs=[pl.BlockSpec((tm, k), lambda i, j: (i, 0)), _weight_spec(layer, k, tn),
                  _const_spec(LANE), _const_spec(LANE), _row_spec(tm, LANE), _row_spec(tm, LANE)],
        out_specs=_tile_spec(tm, tn), out_shape=jax.ShapeDtypeStruct((m, n), BF16),
        compiler_params=_params(2), name="q_up")(cq, w_q, g_nope, g_rope, cos, sin)


def _k_up_epilogue(y, extras, outs, *, heads):
    gn_ref, kr_ref = extras
    o_ref = outs[0]
    for h in range(heads):
        k = _rms(y[:, h * MLA_NOPE:(h + 1) * MLA_NOPE], MLA_NOPE) * gn_ref[...]
        o_ref[:, h * QK_PAD:h * QK_PAD + MLA_NOPE] = k.astype(o_ref.dtype)
        o_ref[:, h * QK_PAD + MLA_NOPE:(h + 1) * QK_PAD] = kr_ref[...]


def _k_up(ckv, w_k, layer, g_nope, k_rope):
    m, n = ckv.shape[0], w_k.shape[2]
    tm, heads = _tile(m, 1024, 8), 4
    tn = heads * MLA_NOPE
    return _matmul(
        ckv, w_k, layer, tm=tm, tn=tn, epilogue=functools.partial(_k_up_epilogue, heads=heads),
        extras=(g_nope, k_rope), extra_specs=(_const_spec(LANE), _row_spec(tm, LANE)),
        out_shape=jax.ShapeDtypeStruct((m, (n // MLA_NOPE) * QK_PAD), BF16),
        out_specs=_tile_spec(tm, heads * QK_PAD), name="k_up")


def _headnorm_epilogue(y, extras, outs, *, heads, scale):
    g_ref = extras[0]
    for h in range(heads):
        sl = slice(h * X_DIM, (h + 1) * X_DIM)
        outs[0][:, sl] = (_rms(y[:, sl], X_DIM) * g_ref[...] * scale).astype(outs[0].dtype)


def _norm_xq_body(x_ref, g_ref, w_ref, gq_ref, o_ref, *, heads, scale):
    x = x_ref[...]
    xn = (_rms(x, x.shape[-1]) * g_ref[...]).astype(BF16)
    y = jnp.dot(xn, w_ref[...].astype(BF16), preferred_element_type=F32)
    _headnorm_epilogue(y, (gq_ref,), (o_ref,), heads=heads, scale=scale)


def _norm_xq_proj(x, g_cross, w_xq, layer, g_xq):
    m, d = x.shape
    n = w_xq.shape[2]
    tm = _tile(m, 512, 8)
    body = functools.partial(_norm_xq_body, heads=n // X_DIM, scale=1.0 / math.sqrt(X_DIM))
    return pl.pallas_call(
        body, grid=(m // tm,),
        in_specs=[pl.BlockSpec((tm, d), lambda i: (i, 0)), pl.BlockSpec((1, d), lambda i: (0, 0)),
                  pl.BlockSpec((None, d, n), lambda i: (layer, 0, 0)),
                  pl.BlockSpec((1, X_DIM), lambda i: (0, 0))],
        out_specs=pl.BlockSpec((tm, n), lambda i: (i, 0)),
        out_shape=jax.ShapeDtypeStruct((m, n), BF16),
        compiler_params=_params(1), name="norm_xq_proj")(x, g_cross.reshape(1, d), w_xq, g_xq.reshape(1, X_DIM))


def _out_norm_body(a_ref, w_ref, res_ref, g_ref, x_ref, h_ref):
    x = res_ref[...] + jnp.dot(a_ref[...], w_ref[...].astype(BF16), preferred_element_type=F32)
    x_ref[...] = x
    h_ref[...] = (_rms(x, x.shape[-1]) * g_ref[...]).astype(h_ref.dtype)


def _matmul_residual_norm(a, w, layer, res, g, name):
    m, k = a.shape
    n = w.shape[2]
    tm = _tile(m, 256, 8)
    return pl.pallas_call(
        _out_norm_body, grid=(m // tm,),
        in_specs=[pl.BlockSpec((tm, k), lambda i: (i, 0)), pl.BlockSpec((None, k, n), lambda i: (layer, 0, 0)),
                  pl.BlockSpec((tm, n), lambda i: (i, 0)), pl.BlockSpec((1, n), lambda i: (0, 0))],
        out_specs=[pl.BlockSpec((tm, n), lambda i: (i, 0))] * 2,
        out_shape=[jax.ShapeDtypeStruct((m, n), F32), jax.ShapeDtypeStruct((m, n), BF16)],
        compiler_params=_params(1), name=name)(a, w, res, g.reshape(1, n))


def _xkv_epilogue(y, extras, outs, *, heads):
    g_ref = extras[0]
    k_ref, v_ref = outs
    width = heads * X_DIM
    for h in range(heads):
        sl = slice(h * X_DIM, (h + 1) * X_DIM)
        k_ref[:, sl] = (_rms(y[:, sl], X_DIM) * g_ref[...]).astype(k_ref.dtype)
    v_ref[...] = y[:, width:].astype(v_ref.dtype)


def _xkv_proj(a, w_xkv, layer, g_xk):
    m, n = a.shape[0], w_xkv.shape[2]
    tm, width = _tile(m, 512, 8), n // 2
    return _matmul(
        a, w_xkv, layer, tm=tm, tn=n, epilogue=functools.partial(_xkv_epilogue, heads=width // X_DIM),
        extras=(g_xk.reshape(1, X_DIM),), extra_specs=(_const_spec(X_DIM),),
        out_shape=[jax.ShapeDtypeStruct((m, width), BF16)] * 2,
        out_specs=[_row_spec(tm, width)] * 2, name="xkv_proj")


def _swiglu_body(a_ref, wg_ref, wu_ref, o_ref):
    a = a_ref[...]
    g = jnp.dot(a, wg_ref[...].astype(BF16), preferred_element_type=F32)
    u = jnp.dot(a, wu_ref[...].astype(BF16), preferred_element_type=F32)
    o_ref[...] = (g / (1.0 + jnp.exp(-g)) * u).astype(o_ref.dtype)


def _swiglu(a, w_gate, w_up, layer):
    m, k = a.shape
    n = w_gate.shape[2]
    tm, tn = _tile(m, 1024, 8), _tile(n, 256, LANE)
    return pl.pallas_call(
        _swiglu_body, grid=(m // tm, n // tn),
        in_specs=[pl.BlockSpec((tm, k), lambda i, j: (i, 0)),
                  _weight_spec(layer, k, tn), _weight_spec(layer, k, tn)],
        out_specs=_tile_spec(tm, tn), out_shape=jax.ShapeDtypeStruct((m, n), BF16),
        compiler_params=_params(2), name="swiglu")(a, w_gate, w_up)


def _dot_nt(a, b):
    return lax.dot_general(a, b, (((1,), (1,)), ((), ())), preferred_element_type=F32)


def _mla_attn_body(q_ref, k_ref, v_ref, o_ref, *, tile):
    n_tiles = q_ref.shape[0] // tile
    row = lax.broadcasted_iota(jnp.int32, (tile, tile), 0)
    col = lax.broadcasted_iota(jnp.int32, (tile, tile), 1)
    def scores(qi):
        lo, hi = qi * tile, (qi + 1) * tile
        q = q_ref[lo:hi, :]
        s_diag = jnp.where(col <= row, _dot_nt(q, k_ref[lo:hi, :]), -jnp.inf)
        return s_diag, (_dot_nt(q, k_ref[0:lo, :]) if qi else None)

    def probs(qi, s_diag, s_off):
        m = jnp.max(s_diag, axis=1, keepdims=True)
        if qi:
            m = jnp.maximum(m, jnp.max(s_off, axis=1, keepdims=True))
        p_diag = jnp.exp2(s_diag - m)
        l = jnp.sum(p_diag, axis=1, keepdims=True)
        if not qi:
            return p_diag.astype(BF16), None, l
        p_off = jnp.exp2(s_off - m)
        return p_diag.astype(BF16), p_off.astype(BF16), l + jnp.sum(p_off, axis=1, keepdims=True)

    def values(qi, p_diag, p_off, l):
        lo, hi = qi * tile, (qi + 1) * tile
        acc = jnp.dot(p_diag, v_ref[lo:hi, :], preferred_element_type=F32)
        if qi:
            acc = acc + jnp.dot(p_off, v_ref[0:lo, :], preferred_element_type=F32)
        o_ref[lo:hi, :] = acc / l

    nxt, prev = scores(0), None
    for qi in range(n_tiles):
        now = nxt
        if qi + 1 < n_tiles:
            nxt = scores(qi + 1)
        if qi:
            values(qi - 1, *prev)
        prev = probs(qi, *now)
    values(n_tiles - 1, *prev)


def _mla_attention(q, k, v, batch):
    t = q.shape[0]
    seq = t // batch
    heads = q.shape[1] // QK_PAD
    tile = _tile(seq, ATTN_TILE, 8)
    return pl.pallas_call(
        functools.partial(_mla_attn_body, tile=tile), grid=(batch, heads),
        in_specs=[pl.BlockSpec((seq, QK_PAD), lambda b, h: (b, h)),
                  pl.BlockSpec((seq, QK_PAD), lambda b, h: (b, h)),
                  pl.BlockSpec((seq, MLA_V), lambda b, h: (b, h))],
        out_specs=pl.BlockSpec((seq, MLA_V), lambda b, h: (b, h)),
        out_shape=jax.ShapeDtypeStruct((t, heads * MLA_V), F32),
        compiler_params=_params(2), name="mla_attention")(q, k, v)


def _sb_attn_body(q_ref, k_ref, v_ref, o_ref, *, tile):
    n_tiles = q_ref.shape[0] // tile
    row = lax.broadcasted_iota(jnp.int32, (tile, tile), 0)
    col = lax.broadcasted_iota(jnp.int32, (tile, tile), 1)
    strict = col < row
    ones_below = jnp.where(row > col, 1.0, 0.0).astype(BF16)

    def logs(z):
        log_beta = jnp.minimum(z, 0.0) - jnp.log2(1.0 + jnp.exp2(-jnp.abs(z)))
        return log_beta, log_beta - z

    def suffix_sums(log_keep):
        return jnp.dot(log_keep.astype(BF16), ones_below, preferred_element_type=F32)

    def logits(qi):
        q = q_ref[qi * tile:(qi + 1) * tile, :]
        z_diag = _dot_nt(q, k_ref[qi * tile:(qi + 1) * tile, :])
        if not qi:
            return z_diag, None
        return z_diag, jnp.concatenate(
            [_dot_nt(q, k_ref[c * tile:(c + 1) * tile, :]) for c in range(qi)], axis=0)

    def exponents(qi, z_diag, z):
        log_beta, log_keep = logs(z_diag)
        log_keep = jnp.where(strict, log_keep, 0.0)
        e_diag = log_beta + suffix_sums(log_keep)
        if not qi:
            return e_diag, None
        log_beta, keep = logs(z)
        totals = jnp.sum(keep, axis=1, keepdims=True)
        run = jnp.sum(log_keep, axis=1, keepdims=True)
        carries = [None] * qi
        for c in reversed(range(qi)):
            carries[c] = run
            run = run + totals[c * tile:(c + 1) * tile]
        return e_diag, log_beta + suffix_sums(keep) + jnp.concatenate(carries, axis=0)

    def weighted_values(qi, e_diag, e):
        lo_row, hi_row = qi * tile, (qi + 1) * tile
        a = jnp.where(strict, jnp.exp2(e_diag), 0.0)
        acc = jnp.dot(a.astype(BF16), v_ref[lo_row:hi_row, :], preferred_element_type=F32)
        if qi:
            a = jnp.exp2(e).astype(BF16)
            a = jnp.concatenate([a[c * tile:(c + 1) * tile] for c in range(qi)], axis=1)
            acc = acc + jnp.dot(a, v_ref[0:lo_row, :], preferred_element_type=F32)
        o_ref[lo_row:hi_row, :] = acc

    z_next = logits(0)
    for qi in range(n_tiles):
        z_now = z_next
        if qi + 1 < n_tiles:
            z_next = logits(qi + 1)
        weighted_values(qi, *exponents(qi, *z_now))


def _sb_attention(qkv, batch):
    t = qkv.shape[0]
    seq = t // batch
    heads = qkv.shape[1] // (3 * SB_DIM)
    tile = _tile(seq, ATTN_TILE, 8)
    return pl.pallas_call(
        functools.partial(_sb_attn_body, tile=tile), grid=(batch, heads),
        in_specs=[pl.BlockSpec((seq, SB_DIM), lambda b, h: (b, h)),
                  pl.BlockSpec((seq, SB_DIM), lambda b, h: (b, heads + h)),
                  pl.BlockSpec((seq, SB_DIM), lambda b, h: (b, 2 * heads + h))],
        out_specs=pl.BlockSpec((seq, SB_DIM), lambda b, h: (b, h)),
        out_shape=jax.ShapeDtypeStruct((t, heads * SB_DIM), F32),
        compiler_params=_params(2), name="sb_attention")(qkv, qkv, qkv)


def _cross_attn_body(q_ref, k_ref, v_ref, o_ref, *, heads):
    for h in range(heads):
        sl = slice(h * X_DIM, (h + 1) * X_DIM)
        s = _dot_nt(q_ref[:, sl], k_ref[:, sl])
        p = jnp.exp(s - jnp.max(s, axis=1, keepdims=True))
        o = jnp.dot(p.astype(BF16), v_ref[:, sl], preferred_element_type=F32)
        o_ref[:, sl] = (o / jnp.sum(p, axis=1, keepdims=True)).astype(o_ref.dtype)


def _cross_attention(q, k, v, batch):
    t, width = q.shape
    seq, mem_len = t // batch, k.shape[0] // batch
    tq = _tile(seq, 512, 8)
    n_q = seq // tq
    return pl.pallas_call(
        functools.partial(_cross_attn_body, heads=width // X_DIM), grid=(batch, n_q),
        in_specs=[pl.BlockSpec((tq, width), lambda b, i: (b * n_q + i, 0)),
                  pl.BlockSpec((mem_len, width), lambda b, i: (b, 0)),
                  pl.BlockSpec((mem_len, width), lambda b, i: (b, 0))],
        out_specs=pl.BlockSpec((tq, width), lambda b, i: (b * n_q + i, 0)),
        out_shape=jax.ShapeDtypeStruct((t, width), BF16),
        compiler_params=_params(2), name="cross_attention")(q, k, v)


def _twice(g):
    return jnp.tile(g, 2).reshape(1, 2 * g.shape[0])


def _relaid_weights(w_in_t, w_q_up, w_kv_up, w_xkv, w_down, q_rank, kv_rank):
    assert 2 * MLA_ROPE == LANE and (q_rank + kv_rank) % LANE == 0
    layers, d = w_in_t.shape[0], w_in_t.shape[2]
    w_q = w_q_up.reshape(layers, q_rank, MLA_HEADS, MLA_NOPE + MLA_ROPE)
    w_q = jnp.concatenate([w_q, w_q[..., MLA_NOPE:]], axis=-1)
    w_kv = w_kv_up.reshape(layers, kv_rank, MLA_HEADS, MLA_NOPE + MLA_V)
    w_x = w_xkv.reshape(layers, d, X_HEADS, 2 * X_DIM)
    ws = dict(lat_t=w_in_t[:, :q_rank + kv_rank + LANE],
              q=w_q.reshape(layers, q_rank, MLA_HEADS * QK_PAD),
              k=w_kv[..., :MLA_NOPE].reshape(layers, kv_rank, -1),
              v=w_kv[..., MLA_NOPE:].reshape(layers, kv_rank, -1),
              xkv=jnp.concatenate([w_x[..., :X_DIM].reshape(layers, d, -1),
                                   w_x[..., X_DIM:].reshape(layers, d, -1)], axis=2),
              down=w_down)
    return {name: w.astype(BF16) for name, w in ws.items()}


def kernel(x, mem, positions, g_attn, w_in, g_q_lat, g_kv_lat, w_q_up, w_kv_up, g_mla_q, g_mla_k,
           g_mla_out, g_sb_out, w_out, g_cross, g_mem, w_xq, w_xkv, g_xq, g_xk, w_xo, g_ffn,
           w_gate, w_up, w_down):
    batch, seq, d = x.shape
    depth = w_in.shape[0]
    q_rank, kv_rank = g_q_lat.shape[1], g_kv_lat.shape[1]
    x = x.reshape(batch * seq, d)
    mem2 = mem.reshape(-1, d)
    cos, sin = _rope_tables(positions)
    w_in_t = jnp.swapaxes(w_in, 1, 2)
    w = _relaid_weights(w_in_t, w_q_up, w_kv_up, w_xkv, w_down, q_rank, kv_rank)
    for l in range(depth):
        n, cq, ckv, k_rope = _norm_latent_proj(x, g_attn[l], w["lat_t"], l, g_q_lat[l], g_kv_lat[l],
                                               _twice(g_mla_k[l, MLA_NOPE:]), cos, sin)
        q = _q_up(cq, w["q"], l, g_mla_q[l, :MLA_NOPE].reshape(1, LANE),
                  _twice(g_mla_q[l, MLA_NOPE:]), cos, sin)
        k = _k_up(ckv, w["k"], l, g_mla_k[l, :MLA_NOPE].reshape(1, LANE), k_rope)
        v = _matmul_plain(ckv, w["v"], l, BF16, "v_up")
        o_mla = _mla_attention(q, k, v, batch)
        o_sb = _sb_attention(_sb_proj(n, w_in_t, l, q_rank + kv_rank + MLA_ROPE), batch)
        mixed = _mixnorm(o_mla, o_sb, g_mla_out[l], g_sb_out[l])
        x = _matmul_residual(mixed, w_out, l, x, "out_proj")
        xq = _norm_xq_proj(x, g_cross[l], w_xq, l, g_xq[l])
        xk, xv = _xkv_proj(_rmsnorm(mem2, g_mem[l], "norm_mem"), w["xkv"], l, g_xk[l])
        x, h = _matmul_residual_norm(_cross_attention(xq, xk, xv, batch), w_xo, l, x, g_ffn[l], "cross_out")
        x = _matmul_residual(_swiglu(h, w_gate, w_up, l), w["down"], l, x, "ffn_down", tm=512)
    return x.reshape(batch, seq, d)
```

```python
import functools
import math

import jax
import jax.numpy as jnp
from jax import lax
from jax.experimental import pallas as pl
from jax.experimental.pallas import tpu as pltpu

MLA_HEADS = 16
MLA_NOPE = 128
MLA_ROPE = 64
MLA_V = 128
SB_HEADS = 16
SB_DIM = 128
X_HEADS = 4
X_DIM = 128
ROPE_THETA = 10000.0
EPS = 1e-6

LANE = 128
QK_PAD = 2 * LANE
VMEM_LIMIT_BYTES = 56 * 1024 * 1024
ATTN_TILE = 256
LOG2_E = math.log2(math.e)
BF16 = jnp.bfloat16
F32 = jnp.float32


def _tile(dim, pref, align):
    if dim <= pref:
        return dim
    t = (pref // align) * align
    while t >= align:
        if dim % t == 0:
            return t
        t -= align
    return dim


def _params(ndims):
    return pltpu.CompilerParams(dimension_semantics=("arbitrary",) * ndims,
                                vmem_limit_bytes=VMEM_LIMIT_BYTES)


def _rms(y, width):
    ms = jnp.sum(y * y, axis=-1, keepdims=True) * (1.0 / width)
    return y * lax.rsqrt(ms + EPS)


def _rope(r, cos, sin):
    return r * cos + pltpu.roll(r, MLA_ROPE // 2, 1) * sin


def _rmsnorm_body(x_ref, g_ref, o_ref):
    x = x_ref[...]
    o_ref[...] = (_rms(x, x.shape[-1]) * g_ref[...]).astype(o_ref.dtype)


def _rmsnorm(x, g, name):
    m, d = x.shape
    tm = _tile(m, 256, 8)
    return pl.pallas_call(
        _rmsnorm_body, grid=(m // tm,),
        in_specs=[pl.BlockSpec((tm, d), lambda i: (i, 0)), pl.BlockSpec((1, d), lambda i: (0, 0))],
        out_specs=pl.BlockSpec((tm, d), lambda i: (i, 0)),
        out_shape=jax.ShapeDtypeStruct((m, d), BF16),
        compiler_params=_params(1), name=name)(x, g.reshape(1, d))


def _mixnorm_body(a_ref, b_ref, ga_ref, gb_ref, o_ref):
    wa = a_ref.shape[-1]
    a, b = a_ref[...], b_ref[...]
    o_ref[:, :wa] = (_rms(a, wa) * ga_ref[...]).astype(o_ref.dtype)
    o_ref[:, wa:] = (_rms(b, b.shape[-1]) * gb_ref[...]).astype(o_ref.dtype)


def _mixnorm(a, b, ga, gb):
    m, wa = a.shape
    wb = b.shape[1]
    tm = _tile(m, 256, 8)
    return pl.pallas_call(
        _mixnorm_body, grid=(m // tm,),
        in_specs=[pl.BlockSpec((tm, wa), lambda i: (i, 0)), pl.BlockSpec((tm, wb), lambda i: (i, 0)),
                  pl.BlockSpec((1, wa), lambda i: (0, 0)), pl.BlockSpec((1, wb), lambda i: (0, 0))],
        out_specs=pl.BlockSpec((tm, wa + wb), lambda i: (i, 0)),
        out_shape=jax.ShapeDtypeStruct((m, wa + wb), BF16),
        compiler_params=_params(1), name="mixnorm")(a, b, ga.reshape(1, wa), gb.reshape(1, wb))


def _rope_table_body(pos_ref, freq_ref, cos_ref, sin_ref):
    ang = pos_ref[...] * freq_ref[...]
    lane = lax.broadcasted_iota(jnp.int32, ang.shape, 1)
    sin = jnp.sin(ang)
    cos_ref[...] = jnp.where(lane < MLA_ROPE, jnp.cos(ang), 0.0)
    sin_ref[...] = jnp.where(lane < MLA_ROPE // 2, -sin, jnp.where(lane < MLA_ROPE, sin, 0.0))


def _rope_tables(positions):
    t = positions.size
    half = MLA_ROPE // 2
    inv_freq = ROPE_THETA ** (-jnp.arange(half, dtype=F32) / half)
    freq = jnp.concatenate([inv_freq, inv_freq, jnp.zeros((LANE - MLA_ROPE,), F32)]).reshape(1, LANE)
    pos = positions.astype(F32).reshape(t, 1)
    tm = _tile(t, 512, 8)
    return pl.pallas_call(
        _rope_table_body, grid=(t // tm,),
        in_specs=[pl.BlockSpec((tm, 1), lambda i: (i, 0)), pl.BlockSpec((1, LANE), lambda i: (0, 0))],
        out_specs=[pl.BlockSpec((tm, LANE), lambda i: (i, 0))] * 2,
        out_shape=[jax.ShapeDtypeStruct((t, LANE), F32)] * 2,
        compiler_params=_params(1), name="rope_tables")(pos, freq)


def _weight_spec(layer, k, tn):
    return pl.BlockSpec((None, k, tn), lambda i, j: (layer, 0, j))


def _mm_body(a_ref, w_ref, *rest, n_extra, epilogue):
    extras, outs = rest[:n_extra], rest[n_extra:]
    y = jnp.dot(a_ref[...], w_ref[...].astype(BF16), preferred_element_type=F32)
    epilogue(y, extras, outs)


def _matmul(a, w, layer, *, tm, tn, epilogue, out_shape, out_specs, extras=(), extra_specs=(), name):
    m, k = a.shape
    n = w.shape[2]
    body = functools.partial(_mm_body, n_extra=len(extras), epilogue=epilogue)
    return pl.pallas_call(
        body, grid=(m // tm, n // tn),
        in_specs=[pl.BlockSpec((tm, k), lambda i, j: (i, 0)), _weight_spec(layer, k, tn), *extra_specs],
        out_specs=out_specs, out_shape=out_shape,
        compiler_params=_params(2), name=name)(a, w, *extras)


def _row_spec(tm, width):
    return pl.BlockSpec((tm, width), lambda i, j: (i, 0))


def _const_spec(width):
    return pl.BlockSpec((1, width), lambda i, j: (0, 0))


def _tile_spec(tm, tn):
    return pl.BlockSpec((tm, tn), lambda i, j: (i, j))


def _store_epilogue(y, extras, outs):
    outs[0][...] = y.astype(outs[0].dtype)


def _residual_epilogue(y, extras, outs):
    outs[0][...] = extras[0][...] + y


def _matmul_plain(a, w, layer, out_dtype, name, tm=1024, tn=512):
    m, n = a.shape[0], w.shape[2]
    tm, tn = _tile(m, tm, 8), _tile(n, tn, LANE)
    return _matmul(a, w, layer, tm=tm, tn=tn, epilogue=_store_epilogue,
                   out_shape=jax.ShapeDtypeStruct((m, n), out_dtype),
                   out_specs=_tile_spec(tm, tn), name=name)


def _matmul_residual(a, w, layer, res, name, tm=1024, tn=512):
    m, n = a.shape[0], w.shape[2]
    tm, tn = _tile(m, tm, 8), _tile(n, tn, LANE)
    return _matmul(a, w, layer, tm=tm, tn=tn, epilogue=_residual_epilogue,
                   extras=(res,), extra_specs=(_tile_spec(tm, tn),),
                   out_shape=jax.ShapeDtypeStruct((m, n), F32),
                   out_specs=_tile_spec(tm, tn), name=name)


def _latent_epilogue(y, extras, outs, *, q_rank, kv_rank):
    gq_ref, gkv_ref, gkr_ref, cos_ref, sin_ref = extras
    cq_ref, ckv_ref, kr_ref = outs
    cq_ref[...] = (_rms(y[:, :q_rank], q_rank) * gq_ref[...]).astype(cq_ref.dtype)
    ckv = y[:, q_rank:q_rank + kv_rank]
    ckv_ref[...] = (_rms(ckv, kv_rank) * gkv_ref[...]).astype(ckv_ref.dtype)
    group = y[:, q_rank + kv_rank:]
    lane = lax.broadcasted_iota(jnp.int32, group.shape, 1)
    twice = jnp.where(lane < MLA_ROPE, group, pltpu.roll(group, MLA_ROPE, 1))
    kr = _rms(twice, 2 * MLA_ROPE) * gkr_ref[...]
    kr_ref[...] = _rope(kr, cos_ref[...], sin_ref[...]).astype(kr_ref.dtype)


def _norm_latent_body(x_ref, g_ref, w_ref, gq_ref, gkv_ref, gkr_ref, cos_ref, sin_ref,
                      n_ref, cq_ref, ckv_ref, kr_ref, *, q_rank, kv_rank):
    x = x_ref[...]
    n = (_rms(x, x.shape[-1]) * g_ref[...]).astype(BF16)
    n_ref[...] = n
    y = _dot_nt(n, w_ref[...])
    _latent_epilogue(y, (gq_ref, gkv_ref, gkr_ref, cos_ref, sin_ref), (cq_ref, ckv_ref, kr_ref),
                     q_rank=q_rank, kv_rank=kv_rank)


def _norm_latent_proj(x, g_attn, w_lat_t, layer, g_q_lat, g_kv_lat, g_k_rope, cos, sin):
    m, d = x.shape
    q_rank, kv_rank = g_q_lat.shape[0], g_kv_lat.shape[0]
    n = w_lat_t.shape[1]
    tm = _tile(m, 256, 8)
    row = lambda width: pl.BlockSpec((tm, width), lambda i: (i, 0))
    const = lambda width: pl.BlockSpec((1, width), lambda i: (0, 0))
    return pl.pallas_call(
        functools.partial(_norm_latent_body, q_rank=q_rank, kv_rank=kv_rank), grid=(m // tm,),
        in_specs=[row(d), const(d), pl.BlockSpec((None, n, d), lambda i: (layer, 0, 0)),
                  const(q_rank), const(kv_rank), const(LANE), row(LANE), row(LANE)],
        out_specs=[row(d), row(q_rank), row(kv_rank), row(LANE)],
        out_shape=[jax.ShapeDtypeStruct((m, d), BF16), jax.ShapeDtypeStruct((m, q_rank), BF16),
                   jax.ShapeDtypeStruct((m, kv_rank), BF16), jax.ShapeDtypeStruct((m, LANE), BF16)],
        compiler_params=_params(1), name="norm_latent_proj")(
            x, g_attn.reshape(1, d), w_lat_t, g_q_lat.reshape(1, -1), g_kv_lat.reshape(1, -1), g_k_rope, cos, sin)


def _sb_proj_body(a_ref, w_ref, o_ref, *, q_tiles, scale):
    j = pl.program_id(1)
    y = _dot_nt(a_ref[...], w_ref[0].astype(BF16))
    o_ref[...] = (y * jnp.where(j < q_tiles, scale, 1.0)).astype(o_ref.dtype)


def _sb_proj(a, w_in_t, layer, first):
    m, k = a.shape
    n = w_in_t.shape[1] - first
    tm, tn = _tile(m, 1024, 8), _tile(n // 3, 512, LANE)
    body = functools.partial(_sb_proj_body, q_tiles=(n // 3) // tn, scale=LOG2_E / math.sqrt(SB_DIM))
    return pl.pallas_call(
        body, grid=(m // tm, n // tn),
        in_specs=[pl.BlockSpec((tm, k), lambda i, j: (i, 0)),
                  pl.BlockSpec((pl.Element(1), pl.Element(tn), pl.Element(k)),
                               lambda i, j: (layer, pl.multiple_of(first + j * tn, 8), 0))],
        out_specs=_tile_spec(tm, tn), out_shape=jax.ShapeDtypeStruct((m, n), BF16),
        compiler_params=_params(2), name="sb_proj")(a, w_in_t)


def _q_up_body(cq_ref, w_ref, gn_ref, gr_ref, cos_ref, sin_ref, o_ref, *, heads, scale):
    cq = cq_ref[...]
    cos, sin = cos_ref[...] * scale, sin_ref[...] * scale
    gn = gn_ref[...] * scale
    for h in range(heads):
        lo = h * QK_PAD
        y = jnp.dot(cq, w_ref[:, lo:lo + QK_PAD], preferred_element_type=F32)
        o_ref[:, lo:lo + MLA_NOPE] = (_rms(y[:, :MLA_NOPE], MLA_NOPE) * gn).astype(o_ref.dtype)
        r = _rms(y[:, MLA_NOPE:], 2 * MLA_ROPE) * gr_ref[...]
        o_ref[:, lo + MLA_NOPE:lo + QK_PAD] = _rope(r, cos, sin).astype(o_ref.dtype)


def _q_up(cq, w_q, layer, g_nope, g_rope, cos, sin):
    m, k = cq.shape
    n = w_q.shape[2]
    tm, heads = _tile(m, 1024, 8), 4
    tn = heads * QK_PAD
    body = functools.partial(_q_up_body, heads=heads, scale=LOG2_E / math.sqrt(MLA_NOPE + MLA_ROPE))
    return pl.pallas_call(
        body, grid=(m // tm, n // tn),
        in_specs=[pl.BlockSpec((tm, k), lambda i, j: (i, 0)), _weight_spec(layer, k, tn),
                  _const_spec(LANE), _const_spec(LANE), _row_spec(tm, LANE), _row_spec(tm, LANE)],
        out_specs=_tile_spec(tm, tn), out_shape=jax.ShapeDtypeStruct((m, n), BF16),
        compiler_params=_params(2), name="q_up")(cq, w_q, g_nope, g_rope, cos, sin)


def _k_up_epilogue(y, extras, outs, *, heads):
    gn_ref, kr_ref = extras
    o_ref = outs[0]
    for h in range(heads):
        k = _rms(y[:, h * MLA_NOPE:(h + 1) * MLA_NOPE], MLA_NOPE) * gn_ref[...]
        o_ref[:, h * QK_PAD:h * QK_PAD + MLA_NOPE] = k.astype(o_ref.dtype)
        o_ref[:, h * QK_PAD + MLA_NOPE:(h + 1) * QK_PAD] = kr_ref[...]


def _k_up(ckv, w_k, layer, g_nope, k_rope):
    m, n = ckv.shape[0], w_k.shape[2]
    tm, heads = _tile(m, 1024, 8), 4
    tn = heads * MLA_NOPE
    return _matmul(
        ckv, w_k, layer, tm=tm, tn=tn, epilogue=functools.partial(_k_up_epilogue, heads=heads),
        extras=(g_nope, k_rope), extra_specs=(_const_spec(LANE), _row_spec(tm, LANE)),
        out_shape=jax.ShapeDtypeStruct((m, (n // MLA_NOPE) * QK_PAD), BF16),
        out_specs=_tile_spec(tm, heads * QK_PAD), name="k_up")


def _headnorm_epilogue(y, extras, outs, *, heads, scale):
    g_ref = extras[0]
    for h in range(heads):
        sl = slice(h * X_DIM, (h + 1) * X_DIM)
        outs[0][:, sl] = (_rms(y[:, sl], X_DIM) * g_ref[...] * scale).astype(outs[0].dtype)


def _norm_xq_body(x_ref, g_ref, w_ref, gq_ref, o_ref, *, heads, scale):
    x = x_ref[...]
    xn = (_rms(x, x.shape[-1]) * g_ref[...]).astype(BF16)
    y = jnp.dot(xn, w_ref[...].astype(BF16), preferred_element_type=F32)
    _headnorm_epilogue(y, (gq_ref,), (o_ref,), heads=heads, scale=scale)


def _norm_xq_proj(x, g_cross, w_xq, layer, g_xq):
    m, d = x.shape
    n = w_xq.shape[2]
    tm = _tile(m, 512, 8)
    body = functools.partial(_norm_xq_body, heads=n // X_DIM, scale=1.0 / math.sqrt(X_DIM))
    return pl.pallas_call(
        body, grid=(m // tm,),
        in_specs=[pl.BlockSpec((tm, d), lambda i: (i, 0)), pl.BlockSpec((1, d), lambda i: (0, 0)),
                  pl.BlockSpec((None, d, n), lambda i: (layer, 0, 0)),
                  pl.BlockSpec((1, X_DIM), lambda i: (0, 0))],
        out_specs=pl.BlockSpec((tm, n), lambda i: (i, 0)),
        out_shape=jax.ShapeDtypeStruct((m, n), BF16),
        compiler_params=_params(1), name="norm_xq_proj")(x, g_cross.reshape(1, d), w_xq, g_xq.reshape(1, X_DIM))


def _out_norm_body(a_ref, w_ref, res_ref, g_ref, x_ref, h_ref):
    x = res_ref[...] + jnp.dot(a_ref[...], w_ref[...].astype(BF16), preferred_element_type=F32)
    x_ref[...] = x
    h_ref[...] = (_rms(x, x.shape[-1]) * g_ref[...]).astype(h_ref.dtype)


def _matmul_residual_norm(a, w, layer, res, g, name):
    m, k = a.shape
    n = w.shape[2]
    tm = _tile(m, 256, 8)
    return pl.pallas_call(
        _out_norm_body, grid=(m // tm,),
        in_specs=[pl.BlockSpec((tm, k), lambda i: (i, 0)), pl.BlockSpec((None, k, n), lambda i: (layer, 0, 0)),
                  pl.BlockSpec((tm, n), lambda i: (i, 0)), pl.BlockSpec((1, n), lambda i: (0, 0))],
        out_specs=[pl.BlockSpec((tm, n), lambda i: (i, 0))] * 2,
        out_shape=[jax.ShapeDtypeStruct((m, n), F32), jax.ShapeDtypeStruct((m, n), BF16)],
        compiler_params=_params(1), name=name)(a, w, res, g.reshape(1, n))


def _xkv_epilogue(y, extras, outs, *, heads):
    g_ref = extras[0]
    k_ref, v_ref = outs
    width = heads * X_DIM
    for h in range(heads):
        sl = slice(h * X_DIM, (h + 1) * X_DIM)
        k_ref[:, sl] = (_rms(y[:, sl], X_DIM) * g_ref[...]).astype(k_ref.dtype)
    v_ref[...] = y[:, width:].astype(v_ref.dtype)


def _xkv_proj(a, w_xkv, layer, g_xk):
    m, n = a.shape[0], w_xkv.shape[2]
    tm, width = _tile(m, 512, 8), n // 2
    return _matmul(
        a, w_xkv, layer, tm=tm, tn=n, epilogue=functools.partial(_xkv_epilogue, heads=width // X_DIM),
        extras=(g_xk.reshape(1, X_DIM),), extra_specs=(_const_spec(X_DIM),),
        out_shape=[jax.ShapeDtypeStruct((m, width), BF16)] * 2,
        out_specs=[_row_spec(tm, width)] * 2, name="xkv_proj")


def _swiglu_body(a_ref, wg_ref, wu_ref, wd_ref, o_ref, wd_out_ref):
    a = a_ref[...]
    g = jnp.dot(a, wg_ref[...].astype(BF16), preferred_element_type=F32)
    u = jnp.dot(a, wu_ref[...].astype(BF16), preferred_element_type=F32)
    o_ref[...] = (g / (1.0 + jnp.exp(-g)) * u).astype(o_ref.dtype)

    @pl.when(pl.program_id(0) == 0)
    def _():
        wd_out_ref[...] = wd_ref[...].astype(wd_out_ref.dtype)


def _swiglu(a, w_gate, w_up, w_down, layer):
    m, k = a.shape
    n = w_gate.shape[2]
    tm, tn = _tile(m, 1024, 8), _tile(n, 256, LANE)
    steps = n // tn
    rows, d_out = w_down.shape[1] // steps, w_down.shape[2]
    assert rows * steps == w_down.shape[1] and rows % 16 == 0
    block = lambda i, j: jnp.where(i == 0, j, steps - 1)
    return pl.pallas_call(
        _swiglu_body, grid=(m // tm, steps),
        in_specs=[pl.BlockSpec((tm, k), lambda i, j: (i, 0)),
                  _weight_spec(layer, k, tn), _weight_spec(layer, k, tn),
                  pl.BlockSpec((None, rows, d_out), lambda i, j: (layer, block(i, j), 0))],
        out_specs=[_tile_spec(tm, tn), pl.BlockSpec((None, rows, d_out), lambda i, j: (0, block(i, j), 0))],
        out_shape=[jax.ShapeDtypeStruct((m, n), BF16), jax.ShapeDtypeStruct((1,) + w_down.shape[1:], BF16)],
        compiler_params=_params(2), name="swiglu")(a, w_gate, w_up, w_down)


def _dot_nt(a, b):
    return lax.dot_general(a, b, (((1,), (1,)), ((), ())), preferred_element_type=F32)


def _mla_attn_body(q_ref, k_ref, v_ref, o_ref, *, tile):
    n_tiles = q_ref.shape[0] // tile
    row = lax.broadcasted_iota(jnp.int32, (tile, tile), 0)
    col = lax.broadcasted_iota(jnp.int32, (tile, tile), 1)
    def scores(qi):
        lo, hi = qi * tile, (qi + 1) * tile
        q = q_ref[lo:hi, :]
        s_diag = jnp.where(col <= row, _dot_nt(q, k_ref[lo:hi, :]), -jnp.inf)
        return s_diag, (_dot_nt(q, k_ref[0:lo, :]) if qi else None)

    def probs(qi, s_diag, s_off):
        m = jnp.max(s_diag, axis=1, keepdims=True)
        if qi:
            m = jnp.maximum(m, jnp.max(s_off, axis=1, keepdims=True))
        p_diag = jnp.exp2(s_diag - m)
        l = jnp.sum(p_diag, axis=1, keepdims=True)
        if not qi:
            return p_diag.astype(BF16), None, l
        p_off = jnp.exp2(s_off - m)
        return p_diag.astype(BF16), p_off.astype(BF16), l + jnp.sum(p_off, axis=1, keepdims=True)

    def values(qi, p_diag, p_off, l):
        lo, hi = qi * tile, (qi + 1) * tile
        acc = jnp.dot(p_diag, v_ref[lo:hi, :], preferred_element_type=F32)
        if qi:
            acc = acc + jnp.dot(p_off, v_ref[0:lo, :], preferred_element_type=F32)
        o_ref[lo:hi, :] = acc / l

    nxt, prev = scores(0), None
    for qi in range(n_tiles):
        now = nxt
        if qi + 1 < n_tiles:
            nxt = scores(qi + 1)
        if qi:
            values(qi - 1, *prev)
        prev = probs(qi, *now)
    values(n_tiles - 1, *prev)


def _mla_attention(q, k, v, batch):
    t = q.shape[0]
    seq = t // batch
    heads = q.shape[1] // QK_PAD
    tile = _tile(seq, ATTN_TILE, 8)
    return pl.pallas_call(
        functools.partial(_mla_attn_body, tile=tile), grid=(batch, heads),
        in_specs=[pl.BlockSpec((seq, QK_PAD), lambda b, h: (b, h)),
                  pl.BlockSpec((seq, QK_PAD), lambda b, h: (b, h)),
                  pl.BlockSpec((seq, MLA_V), lambda b, h: (b, h))],
        out_specs=pl.BlockSpec((seq, MLA_V), lambda b, h: (b, h)),
        out_shape=jax.ShapeDtypeStruct((t, heads * MLA_V), F32),
        compiler_params=_params(2), name="mla_attention")(q, k, v)


def _sb_attn_body(q_ref, k_ref, v_ref, o_ref, *, tile):
    n_tiles = q_ref.shape[0] // tile
    row = lax.broadcasted_iota(jnp.int32, (tile, tile), 0)
    col = lax.broadcasted_iota(jnp.int32, (tile, tile), 1)
    strict = col < row
    ones_below = jnp.where(row > col, 1.0, 0.0).astype(BF16)

    def logs(z):
        sign = jnp.uint32(1 << 31)
        neg_abs = lax.bitcast_convert_type(lax.bitcast_convert_type(z, jnp.uint32) | sign, F32)
        log_beta = jnp.minimum(z, 0.0) - jnp.log2(1.0 + jnp.exp2(neg_abs))
        return log_beta, log_beta - z

    def suffix_sums(log_keep):
        return jnp.dot(log_keep.astype(BF16), ones_below, preferred_element_type=F32)

    def logits(qi):
        q = q_ref[qi * tile:(qi + 1) * tile, :]
        z_diag = _dot_nt(q, k_ref[qi * tile:(qi + 1) * tile, :])
        if not qi:
            return z_diag, None
        return z_diag, jnp.concatenate(
            [_dot_nt(q, k_ref[c * tile:(c + 1) * tile, :]) for c in range(qi)], axis=0)

    def exponents(qi, z_diag, z):
        log_beta, log_keep = logs(z_diag)
        log_keep = jnp.where(strict, log_keep, 0.0)
        e_diag = log_beta + suffix_sums(log_keep)
        if not qi:
            return e_diag, None
        log_beta, keep = logs(z)
        totals = jnp.sum(keep, axis=1, keepdims=True)
        run = jnp.sum(log_keep, axis=1, keepdims=True)
        carries = [None] * qi
        for c in reversed(range(qi)):
            carries[c] = run
            run = run + totals[c * tile:(c + 1) * tile]
        return e_diag, log_beta + suffix_sums(keep) + jnp.concatenate(carries, axis=0)

    def weighted_values(qi, e_diag, e):
        lo_row, hi_row = qi * tile, (qi + 1) * tile
        a = jnp.where(strict, jnp.exp2(e_diag), 0.0)
        acc = jnp.dot(a.astype(BF16), v_ref[lo_row:hi_row, :], preferred_element_type=F32)
        if qi:
            a = jnp.exp2(e).astype(BF16)
            a = jnp.concatenate([a[c * tile:(c + 1) * tile] for c in range(qi)], axis=1)
            acc = acc + jnp.dot(a, v_ref[0:lo_row, :], preferred_element_type=F32)
        o_ref[lo_row:hi_row, :] = acc

    z_next = logits(0)
    for qi in range(n_tiles):
        z_now = z_next
        if qi + 1 < n_tiles:
            z_next = logits(qi + 1)
        weighted_values(qi, *exponents(qi, *z_now))


def _sb_attention(qkv, batch):
    t = qkv.shape[0]
    seq = t // batch
    heads = qkv.shape[1] // (3 * SB_DIM)
    tile = _tile(seq, ATTN_TILE, 8)
    return pl.pallas_call(
        functools.partial(_sb_attn_body, tile=tile), grid=(batch, heads),
        in_specs=[pl.BlockSpec((seq, SB_DIM), lambda b, h: (b, h)),
                  pl.BlockSpec((seq, SB_DIM), lambda b, h: (b, heads + h)),
                  pl.BlockSpec((seq, SB_DIM), lambda b, h: (b, 2 * heads + h))],
        out_specs=pl.BlockSpec((seq, SB_DIM), lambda b, h: (b, h)),
        out_shape=jax.ShapeDtypeStruct((t, heads * SB_DIM), F32),
        compiler_params=_params(2), name="sb_attention")(qkv, qkv, qkv)


def _cross_attn_body(q_ref, k_ref, v_ref, o_ref, *, heads):
    for h in range(heads):
        sl = slice(h * X_DIM, (h + 1) * X_DIM)
        s = _dot_nt(q_ref[:, sl], k_ref[:, sl])
        p = jnp.exp(s - jnp.max(s, axis=1, keepdims=True))
        o = jnp.dot(p.astype(BF16), v_ref[:, sl], preferred_element_type=F32)
        o_ref[:, sl] = (o / jnp.sum(p, axis=1, keepdims=True)).astype(o_ref.dtype)


def _cross_attention(q, k, v, batch):
    t, width = q.shape
    seq, mem_len = t // batch, k.shape[0] // batch
    tq = _tile(seq, 512, 8)
    n_q = seq // tq
    return pl.pallas_call(
        functools.partial(_cross_attn_body, heads=width // X_DIM), grid=(batch, n_q),
        in_specs=[pl.BlockSpec((tq, width), lambda b, i: (b * n_q + i, 0)),
                  pl.BlockSpec((mem_len, width), lambda b, i: (b, 0)),
                  pl.BlockSpec((mem_len, width), lambda b, i: (b, 0))],
        out_specs=pl.BlockSpec((tq, width), lambda b, i: (b * n_q + i, 0)),
        out_shape=jax.ShapeDtypeStruct((t, width), BF16),
        compiler_params=_params(2), name="cross_attention")(q, k, v)


def _twice(g):
    return jnp.tile(g, 2).reshape(1, 2 * g.shape[0])


def _relaid_weights(w_in_t, w_q_up, w_kv_up, w_xkv, q_rank, kv_rank):
    assert 2 * MLA_ROPE == LANE and (q_rank + kv_rank) % LANE == 0
    layers, d = w_in_t.shape[0], w_in_t.shape[2]
    w_q = w_q_up.reshape(layers, q_rank, MLA_HEADS, MLA_NOPE + MLA_ROPE)
    w_q = jnp.concatenate([w_q, w_q[..., MLA_NOPE:]], axis=-1)
    w_kv = w_kv_up.reshape(layers, kv_rank, MLA_HEADS, MLA_NOPE + MLA_V)
    w_x = w_xkv.reshape(layers, d, X_HEADS, 2 * X_DIM)
    ws = dict(lat_t=w_in_t[:, :q_rank + kv_rank + LANE],
              q=w_q.reshape(layers, q_rank, MLA_HEADS * QK_PAD),
              k=w_kv[..., :MLA_NOPE].reshape(layers, kv_rank, -1),
              v=w_kv[..., MLA_NOPE:].reshape(layers, kv_rank, -1),
              xkv=jnp.concatenate([w_x[..., :X_DIM].reshape(layers, d, -1),
                                   w_x[..., X_DIM:].reshape(layers, d, -1)], axis=2))
    return {name: w.astype(BF16) for name, w in ws.items()}


def kernel(x, mem, positions, g_attn, w_in, g_q_lat, g_kv_lat, w_q_up, w_kv_up, g_mla_q, g_mla_k,
           g_mla_out, g_sb_out, w_out, g_cross, g_mem, w_xq, w_xkv, g_xq, g_xk, w_xo, g_ffn,
           w_gate, w_up, w_down):
    batch, seq, d = x.shape
    depth = w_in.shape[0]
    q_rank, kv_rank = g_q_lat.shape[1], g_kv_lat.shape[1]
    x = x.reshape(batch * seq, d)
    mem2 = mem.reshape(-1, d)
    cos, sin = _rope_tables(positions)
    w_in_t = jnp.swapaxes(w_in, 1, 2)
    w = _relaid_weights(w_in_t, w_q_up, w_kv_up, w_xkv, q_rank, kv_rank)
    for l in range(depth):
        n, cq, ckv, k_rope = _norm_latent_proj(x, g_attn[l], w["lat_t"], l, g_q_lat[l], g_kv_lat[l],
                                               _twice(g_mla_k[l, MLA_NOPE:]), cos, sin)
        q = _q_up(cq, w["q"], l, g_mla_q[l, :MLA_NOPE].reshape(1, LANE),
                  _twice(g_mla_q[l, MLA_NOPE:]), cos, sin)
        k = _k_up(ckv, w["k"], l, g_mla_k[l, :MLA_NOPE].reshape(1, LANE), k_rope)
        v = _matmul_plain(ckv, w["v"], l, BF16, "v_up")
        o_mla = _mla_attention(q, k, v, batch)
        o_sb = _sb_attention(_sb_proj(n, w_in_t, l, q_rank + kv_rank + MLA_ROPE), batch)
        mixed = _mixnorm(o_mla, o_sb, g_mla_out[l], g_sb_out[l])
        x = _matmul_residual(mixed, w_out, l, x, "out_proj")
        xq = _norm_xq_proj(x, g_cross[l], w_xq, l, g_xq[l])
        xk, xv = _xkv_proj(_rmsnorm(mem2, g_mem[l], "norm_mem"), w["xkv"], l, g_xk[l])
        x, h = _matmul_residual_norm(_cross_attention(xq, xk, xv, batch), w_xo, l, x, g_ffn[l], "cross_out")
        hidden, w_down_bf16 = _swiglu(h, w_gate, w_up, w_down, l)
        x = _matmul_residual(hidden, w_down_bf16, 0, x, "ffn_down", tm=512)
    return x.reshape(batch, seq, d)
```

```python
import functools
import math

import jax
import jax.numpy as jnp
from jax import lax
from jax.experimental import pallas as pl
from jax.experimental.pallas import tpu as pltpu

MLA_HEADS = 16
MLA_NOPE = 128
MLA_ROPE = 64
MLA_V = 128
SB_HEADS = 16
SB_DIM = 128
X_HEADS = 4
X_DIM = 128
ROPE_THETA = 10000.0
EPS = 1e-6

LANE = 128
QK_PAD = 2 * LANE
VMEM_LIMIT_BYTES = 56 * 1024 * 1024
ATTN_TILE = 256
LOG2_E = math.log2(math.e)
BF16 = jnp.bfloat16
F32 = jnp.float32


def _tile(dim, pref, align):
    if dim <= pref:
        return dim
    t = (pref // align) * align
    while t >= align:
        if dim % t == 0:
            return t
        t -= align
    return dim


def _params(ndims):
    return pltpu.CompilerParams(dimension_semantics=("arbitrary",) * ndims,
                                vmem_limit_bytes=VMEM_LIMIT_BYTES)


def _rms(y, width):
    ms = jnp.sum(y * y, axis=-1, keepdims=True) * (1.0 / width)
    return y * lax.rsqrt(ms + EPS)


def _rope(r, cos, sin):
    return r * cos + pltpu.roll(r, MLA_ROPE // 2, 1) * sin


def _rmsnorm_body(x_ref, g_ref, o_ref):
    x = x_ref[...]
    o_ref[...] = (_rms(x, x.shape[-1]) * g_ref[...]).astype(o_ref.dtype)


def _rmsnorm(x, g, name):
    m, d = x.shape
    tm = _tile(m, 256, 8)
    return pl.pallas_call(
        _rmsnorm_body, grid=(m // tm,),
        in_specs=[pl.BlockSpec((tm, d), lambda i: (i, 0)), pl.BlockSpec((1, d), lambda i: (0, 0))],
        out_specs=pl.BlockSpec((tm, d), lambda i: (i, 0)),
        out_shape=jax.ShapeDtypeStruct((m, d), BF16),
        compiler_params=_params(1), name=name)(x, g.reshape(1, d))


def _mixnorm_body(a_ref, b_ref, ga_ref, gb_ref, o_ref):
    wa = a_ref.shape[-1]
    a, b = a_ref[...], b_ref[...]
    o_ref[:, :wa] = (_rms(a, wa) * ga_ref[...]).astype(o_ref.dtype)
    o_ref[:, wa:] = (_rms(b, b.shape[-1]) * gb_ref[...]).astype(o_ref.dtype)


def _mixnorm(a, b, ga, gb):
    m, wa = a.shape
    wb = b.shape[1]
    tm = _tile(m, 256, 8)
    return pl.pallas_call(
        _mixnorm_body, grid=(m // tm,),
        in_specs=[pl.BlockSpec((tm, wa), lambda i: (i, 0)), pl.BlockSpec((tm, wb), lambda i: (i, 0)),
                  pl.BlockSpec((1, wa), lambda i: (0, 0)), pl.BlockSpec((1, wb), lambda i: (0, 0))],
        out_specs=pl.BlockSpec((tm, wa + wb), lambda i: (i, 0)),
        out_shape=jax.ShapeDtypeStruct((m, wa + wb), BF16),
        compiler_params=_params(1), name="mixnorm")(a, b, ga.reshape(1, wa), gb.reshape(1, wb))


def _rope_table_body(pos_ref, freq_ref, cos_ref, sin_ref):
    ang = pos_ref[...] * freq_ref[...]
    lane = lax.broadcasted_iota(jnp.int32, ang.shape, 1)
    sin = jnp.sin(ang)
    cos_ref[...] = jnp.where(lane < MLA_ROPE, jnp.cos(ang), 0.0)
    sin_ref[...] = jnp.where(lane < MLA_ROPE // 2, -sin, jnp.where(lane < MLA_ROPE, sin, 0.0))


def _rope_tables(positions):
    t = positions.size
    half = MLA_ROPE // 2
    inv_freq = ROPE_THETA ** (-jnp.arange(half, dtype=F32) / half)
    freq = jnp.concatenate([inv_freq, inv_freq, jnp.zeros((LANE - MLA_ROPE,), F32)]).reshape(1, LANE)
    pos = positions.astype(F32).reshape(t, 1)
    tm = _tile(t, 512, 8)
    return pl.pallas_call(
        _rope_table_body, grid=(t // tm,),
        in_specs=[pl.BlockSpec((tm, 1), lambda i: (i, 0)), pl.BlockSpec((1, LANE), lambda i: (0, 0))],
        out_specs=[pl.BlockSpec((tm, LANE), lambda i: (i, 0))] * 2,
        out_shape=[jax.ShapeDtypeStruct((t, LANE), F32)] * 2,
        compiler_params=_params(1), name="rope_tables")(pos, freq)


def _weight_spec(layer, k, tn):
    return pl.BlockSpec((None, k, tn), lambda i, j: (layer, 0, j))


def _mm_body(a_ref, w_ref, *rest, n_extra, epilogue):
    extras, outs = rest[:n_extra], rest[n_extra:]
    y = jnp.dot(a_ref[...], w_ref[...].astype(BF16), preferred_element_type=F32)
    epilogue(y, extras, outs)


def _matmul(a, w, layer, *, tm, tn, epilogue, out_shape, out_specs, extras=(), extra_specs=(), name):
    m, k = a.shape
    n = w.shape[2]
    body = functools.partial(_mm_body, n_extra=len(extras), epilogue=epilogue)
    return pl.pallas_call(
        body, grid=(m // tm, n // tn),
        in_specs=[pl.BlockSpec((tm, k), lambda i, j: (i, 0)), _weight_spec(layer, k, tn), *extra_specs],
        out_specs=out_specs, out_shape=out_shape,
        compiler_params=_params(2), name=name)(a, w, *extras)


def _row_spec(tm, width):
    return pl.BlockSpec((tm, width), lambda i, j: (i, 0))


def _const_spec(width):
    return pl.BlockSpec((1, width), lambda i, j: (0, 0))


def _tile_spec(tm, tn):
    return pl.BlockSpec((tm, tn), lambda i, j: (i, j))


def _store_epilogue(y, extras, outs):
    outs[0][...] = y.astype(outs[0].dtype)


def _residual_epilogue(y, extras, outs):
    outs[0][...] = extras[0][...] + y


def _matmul_plain(a, w, layer, out_dtype, name, tm=1024, tn=512):
    m, n = a.shape[0], w.shape[2]
    tm, tn = _tile(m, tm, 8), _tile(n, tn, LANE)
    return _matmul(a, w, layer, tm=tm, tn=tn, epilogue=_store_epilogue,
                   out_shape=jax.ShapeDtypeStruct((m, n), out_dtype),
                   out_specs=_tile_spec(tm, tn), name=name)


def _matmul_residual(a, w, layer, res, name, tm=1024, tn=512):
    m, n = a.shape[0], w.shape[2]
    tm, tn = _tile(m, tm, 8), _tile(n, tn, LANE)
    return _matmul(a, w, layer, tm=tm, tn=tn, epilogue=_residual_epilogue,
                   extras=(res,), extra_specs=(_tile_spec(tm, tn),),
                   out_shape=jax.ShapeDtypeStruct((m, n), F32),
                   out_specs=_tile_spec(tm, tn), name=name)


def _latent_epilogue(y, extras, outs, *, q_rank, kv_rank):
    gq_ref, gkv_ref, gkr_ref, cos_ref, sin_ref = extras
    cq_ref, ckv_ref, kr_ref = outs
    cq_ref[...] = (_rms(y[:, :q_rank], q_rank) * gq_ref[...]).astype(cq_ref.dtype)
    ckv = y[:, q_rank:q_rank + kv_rank]
    ckv_ref[...] = (_rms(ckv, kv_rank) * gkv_ref[...]).astype(ckv_ref.dtype)
    group = y[:, q_rank + kv_rank:]
    lane = lax.broadcasted_iota(jnp.int32, group.shape, 1)
    twice = jnp.where(lane < MLA_ROPE, group, pltpu.roll(group, MLA_ROPE, 1))
    kr = _rms(twice, 2 * MLA_ROPE) * gkr_ref[...]
    kr_ref[...] = _rope(kr, cos_ref[...], sin_ref[...]).astype(kr_ref.dtype)


def _norm_latent_body(x_ref, g_ref, w_ref, gq_ref, gkv_ref, gkr_ref, cos_ref, sin_ref,
                      n_ref, cq_ref, ckv_ref, kr_ref, *, q_rank, kv_rank):
    x = x_ref[...]
    n = (_rms(x, x.shape[-1]) * g_ref[...]).astype(BF16)
    n_ref[...] = n
    y = _dot_nt(n, w_ref[...])
    _latent_epilogue(y, (gq_ref, gkv_ref, gkr_ref, cos_ref, sin_ref), (cq_ref, ckv_ref, kr_ref),
                     q_rank=q_rank, kv_rank=kv_rank)


def _norm_latent_proj(x, g_attn, w_lat_t, layer, g_q_lat, g_kv_lat, g_k_rope, cos, sin):
    m, d = x.shape
    q_rank, kv_rank = g_q_lat.shape[0], g_kv_lat.shape[0]
    n = w_lat_t.shape[1]
    tm = _tile(m, 256, 8)
    row = lambda width: pl.BlockSpec((tm, width), lambda i: (i, 0))
    const = lambda width: pl.BlockSpec((1, width), lambda i: (0, 0))
    return pl.pallas_call(
        functools.partial(_norm_latent_body, q_rank=q_rank, kv_rank=kv_rank), grid=(m // tm,),
        in_specs=[row(d), const(d), pl.BlockSpec((None, n, d), lambda i: (layer, 0, 0)),
                  const(q_rank), const(kv_rank), const(LANE), row(LANE), row(LANE)],
        out_specs=[row(d), row(q_rank), row(kv_rank), row(LANE)],
        out_shape=[jax.ShapeDtypeStruct((m, d), BF16), jax.ShapeDtypeStruct((m, q_rank), BF16),
                   jax.ShapeDtypeStruct((m, kv_rank), BF16), jax.ShapeDtypeStruct((m, LANE), BF16)],
        compiler_params=_params(1), name="norm_latent_proj")(
            x, g_attn.reshape(1, d), w_lat_t, g_q_lat.reshape(1, -1), g_kv_lat.reshape(1, -1), g_k_rope, cos, sin)


def _sb_proj_body(a_ref, w_ref, o_ref, *, q_tiles, scale):
    j = pl.program_id(1)
    y = _dot_nt(a_ref[...], w_ref[0].astype(BF16))
    o_ref[...] = (y * jnp.where(j < q_tiles, scale, 1.0)).astype(o_ref.dtype)


def _sb_proj(a, w_in_t, layer, first):
    m, k = a.shape
    n = w_in_t.shape[1] - first
    tm, tn = _tile(m, 1024, 8), _tile(n // 3, 512, LANE)
    body = functools.partial(_sb_proj_body, q_tiles=(n // 3) // tn, scale=LOG2_E / math.sqrt(SB_DIM))
    return pl.pallas_call(
        body, grid=(m // tm, n // tn),
        in_specs=[pl.BlockSpec((tm, k), lambda i, j: (i, 0)),
                  pl.BlockSpec((pl.Element(1), pl.Element(tn), pl.Element(k)),
                               lambda i, j: (layer, pl.multiple_of(first + j * tn, 8), 0))],
        out_specs=_tile_spec(tm, tn), out_shape=jax.ShapeDtypeStruct((m, n), BF16),
        compiler_params=_params(2), name="sb_proj")(a, w_in_t)


def _q_up_body(cq_ref, w_ref, gn_ref, gr_ref, cos_ref, sin_ref, o_ref, *, heads, scale):
    cq = cq_ref[...]
    cos, sin = cos_ref[...] * scale, sin_ref[...] * scale
    gn = gn_ref[...] * scale
    for h in range(heads):
        lo = h * QK_PAD
        y = jnp.dot(cq, w_ref[:, lo:lo + QK_PAD], preferred_element_type=F32)
        o_ref[:, lo:lo + MLA_NOPE] = (_rms(y[:, :MLA_NOPE], MLA_NOPE) * gn).astype(o_ref.dtype)
        r = _rms(y[:, MLA_NOPE:], 2 * MLA_ROPE) * gr_ref[...]
        o_ref[:, lo + MLA_NOPE:lo + QK_PAD] = _rope(r, cos, sin).astype(o_ref.dtype)


def _q_up(cq, w_q, layer, g_nope, g_rope, cos, sin):
    m, k = cq.shape
    n = w_q.shape[2]
    tm, heads = _tile(m, 1024, 8), 4
    tn = heads * QK_PAD
    body = functools.partial(_q_up_body, heads=heads, scale=LOG2_E / math.sqrt(MLA_NOPE + MLA_ROPE))
    return pl.pallas_call(
        body, grid=(m // tm, n // tn),
        in_specs=[pl.BlockSpec((tm, k), lambda i, j: (i, 0)), _weight_spec(layer, k, tn),
                  _const_spec(LANE), _const_spec(LANE), _row_spec(tm, LANE), _row_spec(tm, LANE)],
        out_specs=_tile_spec(tm, tn), out_shape=jax.ShapeDtypeStruct((m, n), BF16),
        compiler_params=_params(2), name="q_up")(cq, w_q, g_nope, g_rope, cos, sin)


def _k_up_epilogue(y, extras, outs, *, heads):
    gn_ref, kr_ref = extras
    o_ref = outs[0]
    for h in range(heads):
        k = _rms(y[:, h * MLA_NOPE:(h + 1) * MLA_NOPE], MLA_NOPE) * gn_ref[...]
        o_ref[:, h * QK_PAD:h * QK_PAD + MLA_NOPE] = k.astype(o_ref.dtype)
        o_ref[:, h * QK_PAD + MLA_NOPE:(h + 1) * QK_PAD] = kr_ref[...]


def _k_up(ckv, w_k, layer, g_nope, k_rope):
    m, n = ckv.shape[0], w_k.shape[2]
    tm, heads = _tile(m, 1024, 8), 4
    tn = heads * MLA_NOPE
    return _matmul(
        ckv, w_k, layer, tm=tm, tn=tn, epilogue=functools.partial(_k_up_epilogue, heads=heads),
        extras=(g_nope, k_rope), extra_specs=(_const_spec(LANE), _row_spec(tm, LANE)),
        out_shape=jax.ShapeDtypeStruct((m, (n // MLA_NOPE) * QK_PAD), BF16),
        out_specs=_tile_spec(tm, heads * QK_PAD), name="k_up")


def _headnorm_epilogue(y, extras, outs, *, heads, scale):
    g_ref = extras[0]
    for h in range(heads):
        sl = slice(h * X_DIM, (h + 1) * X_DIM)
        outs[0][:, sl] = (_rms(y[:, sl], X_DIM) * g_ref[...] * scale).astype(outs[0].dtype)


def _norm_xq_body(x_ref, g_ref, w_ref, gq_ref, o_ref, *, heads, scale):
    x = x_ref[...]
    xn = (_rms(x, x.shape[-1]) * g_ref[...]).astype(BF16)
    y = jnp.dot(xn, w_ref[...].astype(BF16), preferred_element_type=F32)
    _headnorm_epilogue(y, (gq_ref,), (o_ref,), heads=heads, scale=scale)


def _norm_xq_proj(x, g_cross, w_xq, layer, g_xq):
    m, d = x.shape
    n = w_xq.shape[2]
    tm = _tile(m, 512, 8)
    body = functools.partial(_norm_xq_body, heads=n // X_DIM, scale=1.0 / math.sqrt(X_DIM))
    return pl.pallas_call(
        body, grid=(m // tm,),
        in_specs=[pl.BlockSpec((tm, d), lambda i: (i, 0)), pl.BlockSpec((1, d), lambda i: (0, 0)),
                  pl.BlockSpec((None, d, n), lambda i: (layer, 0, 0)),
                  pl.BlockSpec((1, X_DIM), lambda i: (0, 0))],
        out_specs=pl.BlockSpec((tm, n), lambda i: (i, 0)),
        out_shape=jax.ShapeDtypeStruct((m, n), BF16),
        compiler_params=_params(1), name="norm_xq_proj")(x, g_cross.reshape(1, d), w_xq, g_xq.reshape(1, X_DIM))


def _out_norm_body(a_ref, w_ref, res_ref, g_ref, x_ref, h_ref):
    x = res_ref[...] + jnp.dot(a_ref[...], w_ref[...].astype(BF16), preferred_element_type=F32)
    x_ref[...] = x
    h_ref[...] = (_rms(x, x.shape[-1]) * g_ref[...]).astype(h_ref.dtype)


def _matmul_residual_norm(a, w, layer, res, g, name):
    m, k = a.shape
    n = w.shape[2]
    tm = _tile(m, 256, 8)
    return pl.pallas_call(
        _out_norm_body, grid=(m // tm,),
        in_specs=[pl.BlockSpec((tm, k), lambda i: (i, 0)), pl.BlockSpec((None, k, n), lambda i: (layer, 0, 0)),
                  pl.BlockSpec((tm, n), lambda i: (i, 0)), pl.BlockSpec((1, n), lambda i: (0, 0))],
        out_specs=[pl.BlockSpec((tm, n), lambda i: (i, 0))] * 2,
        out_shape=[jax.ShapeDtypeStruct((m, n), F32), jax.ShapeDtypeStruct((m, n), BF16)],
        compiler_params=_params(1), name=name)(a, w, res, g.reshape(1, n))


def _xkv_epilogue(y, extras, outs, *, heads):
    g_ref = extras[0]
    k_ref, v_ref = outs
    width = heads * X_DIM
    for h in range(heads):
        sl = slice(h * X_DIM, (h + 1) * X_DIM)
        k_ref[:, sl] = (_rms(y[:, sl], X_DIM) * g_ref[...]).astype(k_ref.dtype)
    v_ref[...] = y[:, width:].astype(v_ref.dtype)


def _xkv_proj(a, w_xkv, layer, g_xk):
    m, n = a.shape[0], w_xkv.shape[2]
    tm, width = _tile(m, 512, 8), n // 2
    return _matmul(
        a, w_xkv, layer, tm=tm, tn=n, epilogue=functools.partial(_xkv_epilogue, heads=width // X_DIM),
        extras=(g_xk.reshape(1, X_DIM),), extra_specs=(_const_spec(X_DIM),),
        out_shape=[jax.ShapeDtypeStruct((m, width), BF16)] * 2,
        out_specs=[_row_spec(tm, width)] * 2, name="xkv_proj")


def _swiglu_body(a_ref, wg_ref, wu_ref, o_ref):
    a = a_ref[...]
    g = jnp.dot(a, wg_ref[...].astype(BF16), preferred_element_type=F32)
    u = jnp.dot(a, wu_ref[...].astype(BF16), preferred_element_type=F32)
    o_ref[...] = (g / (1.0 + jnp.exp(-g)) * u).astype(o_ref.dtype)


def _swiglu(a, w_gate, w_up, layer):
    m, k = a.shape
    n = w_gate.shape[2]
    tm, tn = _tile(m, 1024, 8), _tile(n, 256, LANE)
    return pl.pallas_call(
        _swiglu_body, grid=(m // tm, n // tn),
        in_specs=[pl.BlockSpec((tm, k), lambda i, j: (i, 0)),
                  _weight_spec(layer, k, tn), _weight_spec(layer, k, tn)],
        out_specs=_tile_spec(tm, tn), out_shape=jax.ShapeDtypeStruct((m, n), BF16),
        compiler_params=_params(2), name="swiglu")(a, w_gate, w_up)


def _dot_nt(a, b):
    return lax.dot_general(a, b, (((1,), (1,)), ((), ())), preferred_element_type=F32)


def _cast_block_rows(rows, steps):
    block = -(-rows // steps)
    while rows % block or block % 16:
        block += 1
    return block


def _mla_attn_body(q_ref, k_ref, v_ref, w_ref, o_ref, w_out_ref, *, tile, cast_blocks):
    step = pl.program_id(0) * pl.num_programs(1) + pl.program_id(1)

    @pl.when(step < cast_blocks)
    def _():
        w_out_ref[...] = w_ref[...].astype(w_out_ref.dtype)

    n_tiles = q_ref.shape[0] // tile
    row = lax.broadcasted_iota(jnp.int32, (tile, tile), 0)
    col = lax.broadcasted_iota(jnp.int32, (tile, tile), 1)

    def scores(qi):
        lo, hi = qi * tile, (qi + 1) * tile
        q = q_ref[lo:hi, :]
        s_diag = jnp.where(col <= row, _dot_nt(q, k_ref[lo:hi, :]), -jnp.inf)
        return s_diag, (_dot_nt(q, k_ref[0:lo, :]) if qi else None)

    def probs(qi, s_diag, s_off):
        m = jnp.max(s_diag, axis=1, keepdims=True)
        if qi:
            m = jnp.maximum(m, jnp.max(s_off, axis=1, keepdims=True))
        p_diag = jnp.exp2(s_diag - m)
        l = jnp.sum(p_diag, axis=1, keepdims=True)
        if not qi:
            return p_diag.astype(BF16), None, l
        p_off = jnp.exp2(s_off - m)
        return p_diag.astype(BF16), p_off.astype(BF16), l + jnp.sum(p_off, axis=1, keepdims=True)

    def values(qi, p_diag, p_off, l):
        lo, hi = qi * tile, (qi + 1) * tile
        acc = jnp.dot(p_diag, v_ref[lo:hi, :], preferred_element_type=F32)
        if qi:
            acc = acc + jnp.dot(p_off, v_ref[0:lo, :], preferred_element_type=F32)
        o_ref[lo:hi, :] = acc / l

    nxt, prev = scores(0), None
    for qi in range(n_tiles):
        now = nxt
        if qi + 1 < n_tiles:
            nxt = scores(qi + 1)
        if qi:
            values(qi - 1, *prev)
        prev = probs(qi, *now)
    values(n_tiles - 1, *prev)


def _mla_attention(q, k, v, batch, w, layer):
    t = q.shape[0]
    seq = t // batch
    heads = q.shape[1] // QK_PAD
    tile = _tile(seq, ATTN_TILE, 8)
    rows, cols = w.shape[1:]
    block = _cast_block_rows(rows, batch * heads)
    last = rows // block - 1
    w_block = lambda b, h: jnp.minimum(b * heads + h, last)
    return pl.pallas_call(
        functools.partial(_mla_attn_body, tile=tile, cast_blocks=last + 1), grid=(batch, heads),
        in_specs=[pl.BlockSpec((seq, QK_PAD), lambda b, h: (b, h)),
                  pl.BlockSpec((seq, QK_PAD), lambda b, h: (b, h)),
                  pl.BlockSpec((seq, MLA_V), lambda b, h: (b, h)),
                  pl.BlockSpec((None, block, cols), lambda b, h: (layer, w_block(b, h), 0))],
        out_specs=[pl.BlockSpec((seq, MLA_V), lambda b, h: (b, h)),
                   pl.BlockSpec((None, block, cols), lambda b, h: (0, w_block(b, h), 0))],
        out_shape=[jax.ShapeDtypeStruct((t, heads * MLA_V), F32),
                   jax.ShapeDtypeStruct((1, rows, cols), BF16)],
        compiler_params=_params(2), name="mla_attention")(q, k, v, w)


def _sb_attn_body(q_ref, k_ref, v_ref, o_ref, *, tile):
    n_tiles = q_ref.shape[0] // tile
    row = lax.broadcasted_iota(jnp.int32, (tile, tile), 0)
    col = lax.broadcasted_iota(jnp.int32, (tile, tile), 1)
    strict = col < row
    ones_below = jnp.where(row > col, 1.0, 0.0).astype(BF16)

    def logs(z):
        sign = jnp.uint32(1 << 31)
        neg_abs = lax.bitcast_convert_type(lax.bitcast_convert_type(z, jnp.uint32) | sign, F32)
        log_beta = jnp.minimum(z, 0.0) - jnp.log2(1.0 + jnp.exp2(neg_abs))
        return log_beta, log_beta - z

    def suffix_sums(log_keep):
        return jnp.dot(log_keep.astype(BF16), ones_below, preferred_element_type=F32)

    def logits(qi):
        q = q_ref[qi * tile:(qi + 1) * tile, :]
        z_diag = _dot_nt(q, k_ref[qi * tile:(qi + 1) * tile, :])
        if not qi:
            return z_diag, None
        return z_diag, jnp.concatenate(
            [_dot_nt(q, k_ref[c * tile:(c + 1) * tile, :]) for c in range(qi)], axis=0)

    def exponents(qi, z_diag, z):
        log_beta, log_keep = logs(z_diag)
        log_keep = jnp.where(strict, log_keep, 0.0)
        e_diag = log_beta + suffix_sums(log_keep)
        if not qi:
            return e_diag, None
        log_beta, keep = logs(z)
        totals = jnp.sum(keep, axis=1, keepdims=True)
        run = jnp.sum(log_keep, axis=1, keepdims=True)
        carries = [None] * qi
        for c in reversed(range(qi)):
            carries[c] = run
            run = run + totals[c * tile:(c + 1) * tile]
        return e_diag, log_beta + suffix_sums(keep) + jnp.concatenate(carries, axis=0)

    def weighted_values(qi, e_diag, e):
        lo_row, hi_row = qi * tile, (qi + 1) * tile
        a = jnp.where(strict, jnp.exp2(e_diag), 0.0)
        acc = jnp.dot(a.astype(BF16), v_ref[lo_row:hi_row, :], preferred_element_type=F32)
        if qi:
            a = jnp.exp2(e).astype(BF16)
            a = jnp.concatenate([a[c * tile:(c + 1) * tile] for c in range(qi)], axis=1)
            acc = acc + jnp.dot(a, v_ref[0:lo_row, :], preferred_element_type=F32)
        o_ref[lo_row:hi_row, :] = acc

    z_next = logits(0)
    for qi in range(n_tiles):
        z_now = z_next
        if qi + 1 < n_tiles:
            z_next = logits(qi + 1)
        weighted_values(qi, *exponents(qi, *z_now))


def _sb_attention(qkv, batch):
    t = qkv.shape[0]
    seq = t // batch
    heads = qkv.shape[1] // (3 * SB_DIM)
    tile = _tile(seq, ATTN_TILE, 8)
    return pl.pallas_call(
        functools.partial(_sb_attn_body, tile=tile), grid=(batch, heads),
        in_specs=[pl.BlockSpec((seq, SB_DIM), lambda b, h: (b, h)),
                  pl.BlockSpec((seq, SB_DIM), lambda b, h: (b, heads + h)),
                  pl.BlockSpec((seq, SB_DIM), lambda b, h: (b, 2 * heads + h))],
        out_specs=pl.BlockSpec((seq, SB_DIM), lambda b, h: (b, h)),
        out_shape=jax.ShapeDtypeStruct((t, heads * SB_DIM), F32),
        compiler_params=_params(2), name="sb_attention")(qkv, qkv, qkv)


def _cross_attn_body(q_ref, k_ref, v_ref, o_ref, *, heads):
    for h in range(heads):
        sl = slice(h * X_DIM, (h + 1) * X_DIM)
        s = _dot_nt(q_ref[:, sl], k_ref[:, sl])
        p = jnp.exp(s - jnp.max(s, axis=1, keepdims=True))
        o = jnp.dot(p.astype(BF16), v_ref[:, sl], preferred_element_type=F32)
        o_ref[:, sl] = (o / jnp.sum(p, axis=1, keepdims=True)).astype(o_ref.dtype)


def _cross_attention(q, k, v, batch):
    t, width = q.shape
    seq, mem_len = t // batch, k.shape[0] // batch
    tq = _tile(seq, 512, 8)
    n_q = seq // tq
    return pl.pallas_call(
        functools.partial(_cross_attn_body, heads=width // X_DIM), grid=(batch, n_q),
        in_specs=[pl.BlockSpec((tq, width), lambda b, i: (b * n_q + i, 0)),
                  pl.BlockSpec((mem_len, width), lambda b, i: (b, 0)),
                  pl.BlockSpec((mem_len, width), lambda b, i: (b, 0))],
        out_specs=pl.BlockSpec((tq, width), lambda b, i: (b * n_q + i, 0)),
        out_shape=jax.ShapeDtypeStruct((t, width), BF16),
        compiler_params=_params(2), name="cross_attention")(q, k, v)


def _twice(g):
    return jnp.tile(g, 2).reshape(1, 2 * g.shape[0])


def _relaid_weights(w_in_t, w_q_up, w_kv_up, w_xkv, q_rank, kv_rank):
    assert 2 * MLA_ROPE == LANE and (q_rank + kv_rank) % LANE == 0
    layers, d = w_in_t.shape[0], w_in_t.shape[2]
    w_q = w_q_up.reshape(layers, q_rank, MLA_HEADS, MLA_NOPE + MLA_ROPE)
    w_q = jnp.concatenate([w_q, w_q[..., MLA_NOPE:]], axis=-1)
    w_kv = w_kv_up.reshape(layers, kv_rank, MLA_HEADS, MLA_NOPE + MLA_V)
    w_x = w_xkv.reshape(layers, d, X_HEADS, 2 * X_DIM)
    ws = dict(lat_t=w_in_t[:, :q_rank + kv_rank + LANE],
              q=w_q.reshape(layers, q_rank, MLA_HEADS * QK_PAD),
              k=w_kv[..., :MLA_NOPE].reshape(layers, kv_rank, -1),
              v=w_kv[..., MLA_NOPE:].reshape(layers, kv_rank, -1),
              xkv=jnp.concatenate([w_x[..., :X_DIM].reshape(layers, d, -1),
                                   w_x[..., X_DIM:].reshape(layers, d, -1)], axis=2))
    return {name: w.astype(BF16) for name, w in ws.items()}


def kernel(x, mem, positions, g_attn, w_in, g_q_lat, g_kv_lat, w_q_up, w_kv_up, g_mla_q, g_mla_k,
           g_mla_out, g_sb_out, w_out, g_cross, g_mem, w_xq, w_xkv, g_xq, g_xk, w_xo, g_ffn,
           w_gate, w_up, w_down):
    batch, seq, d = x.shape
    depth = w_in.shape[0]
    q_rank, kv_rank = g_q_lat.shape[1], g_kv_lat.shape[1]
    x = x.reshape(batch * seq, d)
    mem2 = mem.reshape(-1, d)
    cos, sin = _rope_tables(positions)
    w_in_t = jnp.swapaxes(w_in, 1, 2)
    w = _relaid_weights(w_in_t, w_q_up, w_kv_up, w_xkv, q_rank, kv_rank)
    for l in range(depth):
        n, cq, ckv, k_rope = _norm_latent_proj(x, g_attn[l], w["lat_t"], l, g_q_lat[l], g_kv_lat[l],
                                               _twice(g_mla_k[l, MLA_NOPE:]), cos, sin)
        q = _q_up(cq, w["q"], l, g_mla_q[l, :MLA_NOPE].reshape(1, LANE),
                  _twice(g_mla_q[l, MLA_NOPE:]), cos, sin)
        k = _k_up(ckv, w["k"], l, g_mla_k[l, :MLA_NOPE].reshape(1, LANE), k_rope)
        v = _matmul_plain(ckv, w["v"], l, BF16, "v_up")
        o_mla, w_down_bf16 = _mla_attention(q, k, v, batch, w_down, l)
        o_sb = _sb_attention(_sb_proj(n, w_in_t, l, q_rank + kv_rank + MLA_ROPE), batch)
        mixed = _mixnorm(o_mla, o_sb, g_mla_out[l], g_sb_out[l])
        x = _matmul_residual(mixed, w_out, l, x, "out_proj")
        xq = _norm_xq_proj(x, g_cross[l], w_xq, l, g_xq[l])
        xk, xv = _xkv_proj(_rmsnorm(mem2, g_mem[l], "norm_mem"), w["xkv"], l, g_xk[l])
        x, h = _matmul_residual_norm(_cross_attention(xq, xk, xv, batch), w_xo, l, x, g_ffn[l], "cross_out")
        x = _matmul_residual(_swiglu(h, w_gate, w_up, l), w_down_bf16, 0, x, "ffn_down", tm=512)
    return x.reshape(batch, seq, d)
```

```python
import functools
import math

import jax
import jax.numpy as jnp
from jax import lax
from jax.experimental import pallas as pl
from jax.experimental.pallas import tpu as pltpu

MLA_HEADS = 16
MLA_NOPE = 128
MLA_ROPE = 64
MLA_V = 128
SB_HEADS = 16
SB_DIM = 128
X_HEADS = 4
X_DIM = 128
ROPE_THETA = 10000.0
EPS = 1e-6

LANE = 128
QK_PAD = 2 * LANE
VMEM_LIMIT_BYTES = 56 * 1024 * 1024
ATTN_TILE = 256
LOG2_E = math.log2(math.e)
BF16 = jnp.bfloat16
F32 = jnp.float32


def _tile(dim, pref, align):
    if dim <= pref:
        return dim
    t = (pref // align) * align
    while t >= align:
        if dim % t == 0:
            return t
        t -= align
    return dim


def _params(ndims):
    return pltpu.CompilerParams(dimension_semantics=("arbitrary",) * ndims,
                                vmem_limit_bytes=VMEM_LIMIT_BYTES)


def _rms(y, width):
    ms = jnp.sum(y * y, axis=-1, keepdims=True) * (1.0 / width)
    return y * lax.rsqrt(ms + EPS)


def _rope(r, cos, sin):
    return r * cos + pltpu.roll(r, MLA_ROPE // 2, 1) * sin


def _rmsnorm_body(x_ref, g_ref, o_ref):
    x = x_ref[...]
    o_ref[...] = (_rms(x, x.shape[-1]) * g_ref[...]).astype(o_ref.dtype)


def _rmsnorm(x, g, name):
    m, d = x.shape
    tm = _tile(m, 256, 8)
    return pl.pallas_call(
        _rmsnorm_body, grid=(m // tm,),
        in_specs=[pl.BlockSpec((tm, d), lambda i: (i, 0)), pl.BlockSpec((1, d), lambda i: (0, 0))],
        out_specs=pl.BlockSpec((tm, d), lambda i: (i, 0)),
        out_shape=jax.ShapeDtypeStruct((m, d), BF16),
        compiler_params=_params(1), name=name)(x, g.reshape(1, d))


def _mixnorm_body(a_ref, b_ref, ga_ref, gb_ref, o_ref):
    wa = a_ref.shape[-1]
    a, b = a_ref[...], b_ref[...]
    o_ref[:, :wa] = (_rms(a, wa) * ga_ref[...]).astype(o_ref.dtype)
    o_ref[:, wa:] = (_rms(b, b.shape[-1]) * gb_ref[...]).astype(o_ref.dtype)


def _mixnorm(a, b, ga, gb):
    m, wa = a.shape
    wb = b.shape[1]
    tm = _tile(m, 256, 8)
    return pl.pallas_call(
        _mixnorm_body, grid=(m // tm,),
        in_specs=[pl.BlockSpec((tm, wa), lambda i: (i, 0)), pl.BlockSpec((tm, wb), lambda i: (i, 0)),
                  pl.BlockSpec((1, wa), lambda i: (0, 0)), pl.BlockSpec((1, wb), lambda i: (0, 0))],
        out_specs=pl.BlockSpec((tm, wa + wb), lambda i: (i, 0)),
        out_shape=jax.ShapeDtypeStruct((m, wa + wb), BF16),
        compiler_params=_params(1), name="mixnorm")(a, b, ga.reshape(1, wa), gb.reshape(1, wb))


def _rope_table_body(pos_ref, freq_ref, cos_ref, sin_ref):
    ang = pos_ref[...] * freq_ref[...]
    lane = lax.broadcasted_iota(jnp.int32, ang.shape, 1)
    sin = jnp.sin(ang)
    cos_ref[...] = jnp.where(lane < MLA_ROPE, jnp.cos(ang), 0.0)
    sin_ref[...] = jnp.where(lane < MLA_ROPE // 2, -sin, jnp.where(lane < MLA_ROPE, sin, 0.0))


def _rope_tables(positions):
    t = positions.size
    half = MLA_ROPE // 2
    inv_freq = ROPE_THETA ** (-jnp.arange(half, dtype=F32) / half)
    freq = jnp.concatenate([inv_freq, inv_freq, jnp.zeros((LANE - MLA_ROPE,), F32)]).reshape(1, LANE)
    pos = positions.astype(F32).reshape(t, 1)
    tm = _tile(t, 512, 8)
    return pl.pallas_call(
        _rope_table_body, grid=(t // tm,),
        in_specs=[pl.BlockSpec((tm, 1), lambda i: (i, 0)), pl.BlockSpec((1, LANE), lambda i: (0, 0))],
        out_specs=[pl.BlockSpec((tm, LANE), lambda i: (i, 0))] * 2,
        out_shape=[jax.ShapeDtypeStruct((t, LANE), F32)] * 2,
        compiler_params=_params(1), name="rope_tables")(pos, freq)


def _weight_spec(layer, k, tn):
    return pl.BlockSpec((None, k, tn), lambda i, j: (layer, 0, j))


def _mm_body(a_ref, w_ref, *rest, n_extra, epilogue):
    extras, outs = rest[:n_extra], rest[n_extra:]
    y = jnp.dot(a_ref[...], w_ref[...].astype(BF16), preferred_element_type=F32)
    epilogue(y, extras, outs)


def _matmul(a, w, layer, *, tm, tn, epilogue, out_shape, out_specs, extras=(), extra_specs=(), name):
    m, k = a.shape
    n = w.shape[2]
    body = functools.partial(_mm_body, n_extra=len(extras), epilogue=epilogue)
    return pl.pallas_call(
        body, grid=(m // tm, n // tn),
        in_specs=[pl.BlockSpec((tm, k), lambda i, j: (i, 0)), _weight_spec(layer, k, tn), *extra_specs],
        out_specs=out_specs, out_shape=out_shape,
        compiler_params=_params(2), name=name)(a, w, *extras)


def _row_spec(tm, width):
    return pl.BlockSpec((tm, width), lambda i, j: (i, 0))


def _const_spec(width):
    return pl.BlockSpec((1, width), lambda i, j: (0, 0))


def _tile_spec(tm, tn):
    return pl.BlockSpec((tm, tn), lambda i, j: (i, j))


def _residual_epilogue(y, extras, outs):
    outs[0][...] = extras[0][...] + y


def _matmul_residual(a, w, layer, res, name, tm=1024, tn=512):
    m, n = a.shape[0], w.shape[2]
    tm, tn = _tile(m, tm, 8), _tile(n, tn, LANE)
    return _matmul(a, w, layer, tm=tm, tn=tn, epilogue=_residual_epilogue,
                   extras=(res,), extra_specs=(_tile_spec(tm, tn),),
                   out_shape=jax.ShapeDtypeStruct((m, n), F32),
                   out_specs=_tile_spec(tm, tn), name=name)


def _latent_epilogue(y, extras, outs, *, q_rank, kv_rank):
    gq_ref, gkv_ref, gkr_ref, cos_ref, sin_ref = extras
    cq_ref, ckv_ref, kr_ref = outs
    cq_ref[...] = (_rms(y[:, :q_rank], q_rank) * gq_ref[...]).astype(cq_ref.dtype)
    ckv = y[:, q_rank:q_rank + kv_rank]
    ckv_ref[...] = (_rms(ckv, kv_rank) * gkv_ref[...]).astype(ckv_ref.dtype)
    group = y[:, q_rank + kv_rank:]
    lane = lax.broadcasted_iota(jnp.int32, group.shape, 1)
    twice = jnp.where(lane < MLA_ROPE, group, pltpu.roll(group, MLA_ROPE, 1))
    kr = _rms(twice, 2 * MLA_ROPE) * gkr_ref[...]
    kr_ref[...] = _rope(kr, cos_ref[...], sin_ref[...]).astype(kr_ref.dtype)


def _norm_latent_body(x_ref, g_ref, w_ref, gq_ref, gkv_ref, gkr_ref, cos_ref, sin_ref,
                      n_ref, cq_ref, ckv_ref, kr_ref, *, q_rank, kv_rank):
    x = x_ref[...]
    n = (_rms(x, x.shape[-1]) * g_ref[...]).astype(BF16)
    n_ref[...] = n
    y = _dot_nt(n, w_ref[...])
    _latent_epilogue(y, (gq_ref, gkv_ref, gkr_ref, cos_ref, sin_ref), (cq_ref, ckv_ref, kr_ref),
                     q_rank=q_rank, kv_rank=kv_rank)


def _norm_latent_proj(x, g_attn, w_lat_t, layer, g_q_lat, g_kv_lat, g_k_rope, cos, sin):
    m, d = x.shape
    q_rank, kv_rank = g_q_lat.shape[0], g_kv_lat.shape[0]
    n = w_lat_t.shape[1]
    tm = _tile(m, 256, 8)
    row = lambda width: pl.BlockSpec((tm, width), lambda i: (i, 0))
    const = lambda width: pl.BlockSpec((1, width), lambda i: (0, 0))
    return pl.pallas_call(
        functools.partial(_norm_latent_body, q_rank=q_rank, kv_rank=kv_rank), grid=(m // tm,),
        in_specs=[row(d), const(d), pl.BlockSpec((None, n, d), lambda i: (layer, 0, 0)),
                  const(q_rank), const(kv_rank), const(LANE), row(LANE), row(LANE)],
        out_specs=[row(d), row(q_rank), row(kv_rank), row(LANE)],
        out_shape=[jax.ShapeDtypeStruct((m, d), BF16), jax.ShapeDtypeStruct((m, q_rank), BF16),
                   jax.ShapeDtypeStruct((m, kv_rank), BF16), jax.ShapeDtypeStruct((m, LANE), BF16)],
        compiler_params=_params(1), name="norm_latent_proj")(
            x, g_attn.reshape(1, d), w_lat_t, g_q_lat.reshape(1, -1), g_kv_lat.reshape(1, -1), g_k_rope, cos, sin)


def _sb_proj_body(a_ref, w_ref, o_ref, *, q_tiles, scale):
    j = pl.program_id(1)
    y = _dot_nt(a_ref[...], w_ref[0].astype(BF16))
    o_ref[...] = (y * jnp.where(j < q_tiles, scale, 1.0)).astype(o_ref.dtype)


def _sb_proj(a, w_in_t, layer, first):
    m, k = a.shape
    n = w_in_t.shape[1] - first
    tm, tn = _tile(m, 1024, 8), _tile(n // 3, 512, LANE)
    body = functools.partial(_sb_proj_body, q_tiles=(n // 3) // tn, scale=LOG2_E / math.sqrt(SB_DIM))
    return pl.pallas_call(
        body, grid=(m // tm, n // tn),
        in_specs=[pl.BlockSpec((tm, k), lambda i, j: (i, 0)),
                  pl.BlockSpec((pl.Element(1), pl.Element(tn), pl.Element(k)),
                               lambda i, j: (layer, pl.multiple_of(first + j * tn, 8), 0))],
        out_specs=_tile_spec(tm, tn), out_shape=jax.ShapeDtypeStruct((m, n), BF16),
        compiler_params=_params(2), name="sb_proj")(a, w_in_t)


def _q_up_body(cq_ref, w_ref, gn_ref, gr_ref, cos_ref, sin_ref, o_ref, *, heads, scale):
    cq = cq_ref[...]
    cos, sin = cos_ref[...] * scale, sin_ref[...] * scale
    gn = gn_ref[...] * scale
    for h in range(heads):
        lo = h * QK_PAD
        y = jnp.dot(cq, w_ref[:, lo:lo + QK_PAD], preferred_element_type=F32)
        o_ref[:, lo:lo + MLA_NOPE] = (_rms(y[:, :MLA_NOPE], MLA_NOPE) * gn).astype(o_ref.dtype)
        r = _rms(y[:, MLA_NOPE:], 2 * MLA_ROPE) * gr_ref[...]
        o_ref[:, lo + MLA_NOPE:lo + QK_PAD] = _rope(r, cos, sin).astype(o_ref.dtype)


def _q_up(cq, w_q, layer, g_nope, g_rope, cos, sin):
    m, k = cq.shape
    n = w_q.shape[2]
    tm, heads = _tile(m, 1024, 8), 4
    tn = heads * QK_PAD
    body = functools.partial(_q_up_body, heads=heads, scale=LOG2_E / math.sqrt(MLA_NOPE + MLA_ROPE))
    return pl.pallas_call(
        body, grid=(m // tm, n // tn),
        in_specs=[pl.BlockSpec((tm, k), lambda i, j: (i, 0)), _weight_spec(layer, k, tn),
                  _const_spec(LANE), _const_spec(LANE), _row_spec(tm, LANE), _row_spec(tm, LANE)],
        out_specs=_tile_spec(tm, tn), out_shape=jax.ShapeDtypeStruct((m, n), BF16),
        compiler_params=_params(2), name="q_up")(cq, w_q, g_nope, g_rope, cos, sin)


def _kv_up_epilogue(y, extras, outs, *, heads):
    gn_ref, kr_ref = extras
    k_ref, v_ref = outs
    pair = MLA_NOPE + MLA_V
    for h in range(heads):
        k = _rms(y[:, h * pair:h * pair + MLA_NOPE], MLA_NOPE) * gn_ref[...]
        k_ref[:, h * QK_PAD:h * QK_PAD + MLA_NOPE] = k.astype(k_ref.dtype)
        k_ref[:, h * QK_PAD + MLA_NOPE:(h + 1) * QK_PAD] = kr_ref[...]
        v_ref[:, h * MLA_V:(h + 1) * MLA_V] = y[:, h * pair + MLA_NOPE:(h + 1) * pair].astype(v_ref.dtype)


def _kv_up(ckv, w_kv_up, layer, g_nope, k_rope):
    assert MLA_NOPE == LANE and MLA_V == LANE
    m, n = ckv.shape[0], w_kv_up.shape[2]
    all_heads = n // (MLA_NOPE + MLA_V)
    tm, heads = _tile(m, 1024, 8), 4
    return _matmul(
        ckv, w_kv_up, layer, tm=tm, tn=heads * (MLA_NOPE + MLA_V),
        epilogue=functools.partial(_kv_up_epilogue, heads=heads),
        extras=(g_nope, k_rope), extra_specs=(_const_spec(LANE), _row_spec(tm, LANE)),
        out_shape=[jax.ShapeDtypeStruct((m, all_heads * QK_PAD), BF16),
                   jax.ShapeDtypeStruct((m, all_heads * MLA_V), BF16)],
        out_specs=[_tile_spec(tm, heads * QK_PAD), _tile_spec(tm, heads * MLA_V)], name="kv_up")


def _headnorm_epilogue(y, extras, outs, *, heads, scale):
    g_ref = extras[0]
    for h in range(heads):
        sl = slice(h * X_DIM, (h + 1) * X_DIM)
        outs[0][:, sl] = (_rms(y[:, sl], X_DIM) * g_ref[...] * scale).astype(outs[0].dtype)


def _norm_xq_body(x_ref, g_ref, w_ref, gq_ref, o_ref, *, heads, scale):
    x = x_ref[...]
    xn = (_rms(x, x.shape[-1]) * g_ref[...]).astype(BF16)
    y = jnp.dot(xn, w_ref[...].astype(BF16), preferred_element_type=F32)
    _headnorm_epilogue(y, (gq_ref,), (o_ref,), heads=heads, scale=scale)


def _norm_xq_proj(x, g_cross, w_xq, layer, g_xq):
    m, d = x.shape
    n = w_xq.shape[2]
    tm = _tile(m, 512, 8)
    body = functools.partial(_norm_xq_body, heads=n // X_DIM, scale=1.0 / math.sqrt(X_DIM))
    return pl.pallas_call(
        body, grid=(m // tm,),
        in_specs=[pl.BlockSpec((tm, d), lambda i: (i, 0)), pl.BlockSpec((1, d), lambda i: (0, 0)),
                  pl.BlockSpec((None, d, n), lambda i: (layer, 0, 0)),
                  pl.BlockSpec((1, X_DIM), lambda i: (0, 0))],
        out_specs=pl.BlockSpec((tm, n), lambda i: (i, 0)),
        out_shape=jax.ShapeDtypeStruct((m, n), BF16),
        compiler_params=_params(1), name="norm_xq_proj")(x, g_cross.reshape(1, d), w_xq, g_xq.reshape(1, X_DIM))


def _out_norm_body(a_ref, w_ref, res_ref, g_ref, x_ref, h_ref):
    x = res_ref[...] + jnp.dot(a_ref[...], w_ref[...].astype(BF16), preferred_element_type=F32)
    x_ref[...] = x
    h_ref[...] = (_rms(x, x.shape[-1]) * g_ref[...]).astype(h_ref.dtype)


def _matmul_residual_norm(a, w, layer, res, g, name):
    m, k = a.shape
    n = w.shape[2]
    tm = _tile(m, 256, 8)
    return pl.pallas_call(
        _out_norm_body, grid=(m // tm,),
        in_specs=[pl.BlockSpec((tm, k), lambda i: (i, 0)), pl.BlockSpec((None, k, n), lambda i: (layer, 0, 0)),
                  pl.BlockSpec((tm, n), lambda i: (i, 0)), pl.BlockSpec((1, n), lambda i: (0, 0))],
        out_specs=[pl.BlockSpec((tm, n), lambda i: (i, 0))] * 2,
        out_shape=[jax.ShapeDtypeStruct((m, n), F32), jax.ShapeDtypeStruct((m, n), BF16)],
        compiler_params=_params(1), name=name)(a, w, res, g.reshape(1, n))


def _xkv_epilogue(y, extras, outs, *, heads):
    g_ref = extras[0]
    k_ref, v_ref = outs
    for h in range(heads):
        sl = slice(h * X_DIM, (h + 1) * X_DIM)
        k = _rms(y[:, 2 * h * X_DIM:(2 * h + 1) * X_DIM], X_DIM) * g_ref[...]
        k_ref[:, sl] = k.astype(k_ref.dtype)
        v_ref[:, sl] = y[:, (2 * h + 1) * X_DIM:(2 * h + 2) * X_DIM].astype(v_ref.dtype)


def _xkv_proj(a, w_xkv, layer, g_xk):
    m, n = a.shape[0], w_xkv.shape[2]
    tm, width = _tile(m, 512, 8), n // 2
    return _matmul(
        a, w_xkv, layer, tm=tm, tn=n, epilogue=functools.partial(_xkv_epilogue, heads=width // X_DIM),
        extras=(g_xk.reshape(1, X_DIM),), extra_specs=(_const_spec(X_DIM),),
        out_shape=[jax.ShapeDtypeStruct((m, width), BF16)] * 2,
        out_specs=[_row_spec(tm, width)] * 2, name="xkv_proj")


def _swiglu_body(a_ref, wg_ref, wu_ref, o_ref):
    a = a_ref[...]
    g = jnp.dot(a, wg_ref[...].astype(BF16), preferred_element_type=F32)
    u = jnp.dot(a, wu_ref[...].astype(BF16), preferred_element_type=F32)
    o_ref[...] = (g / (1.0 + jnp.exp(-g)) * u).astype(o_ref.dtype)


def _swiglu(a, w_gate, w_up, layer):
    m, k = a.shape
    n = w_gate.shape[2]
    tm, tn = _tile(m, 2048, 8), _tile(n, 256, LANE)
    return pl.pallas_call(
        _swiglu_body, grid=(m // tm, n // tn),
        in_specs=[pl.BlockSpec((tm, k), lambda i, j: (i, 0), pipeline_mode=pl.Buffered(1)),
                  _weight_spec(layer, k, tn), _weight_spec(layer, k, tn)],
        out_specs=_tile_spec(tm, tn), out_shape=jax.ShapeDtypeStruct((m, n), BF16),
        compiler_params=_params(2), name="swiglu")(a, w_gate, w_up)


def _dot_nt(a, b):
    return lax.dot_general(a, b, (((1,), (1,)), ((), ())), preferred_element_type=F32)


def _cast_block_rows(rows, steps):
    block = -(-rows // steps)
    while rows % block or block % 16:
        block += 1
    return block


def _mla_attn_body(q_ref, k_ref, v_ref, w_ref, o_ref, w_out_ref, *, tile, cast_blocks):
    step = pl.program_id(0) * pl.num_programs(1) + pl.program_id(1)

    @pl.when(step < cast_blocks)
    def _():
        w_out_ref[...] = w_ref[...].astype(w_out_ref.dtype)

    n_tiles = q_ref.shape[0] // tile
    row = lax.broadcasted_iota(jnp.int32, (tile, tile), 0)
    col = lax.broadcasted_iota(jnp.int32, (tile, tile), 1)

    def scores(qi):
        lo, hi = qi * tile, (qi + 1) * tile
        q = q_ref[lo:hi, :]
        s_diag = jnp.where(col <= row, _dot_nt(q, k_ref[lo:hi, :]), -jnp.inf)
        return s_diag, (_dot_nt(q, k_ref[0:lo, :]) if qi else None)

    def probs(qi, s_diag, s_off):
        m = jnp.max(s_diag, axis=1, keepdims=True)
        if qi:
            m = jnp.maximum(m, jnp.max(s_off, axis=1, keepdims=True))
        p_diag = jnp.exp2(s_diag - m)
        l = jnp.sum(p_diag, axis=1, keepdims=True)
        if not qi:
            return p_diag.astype(BF16), None, l
        p_off = jnp.exp2(s_off - m)
        return p_diag.astype(BF16), p_off.astype(BF16), l + jnp.sum(p_off, axis=1, keepdims=True)

    def values(qi, p_diag, p_off, l):
        lo, hi = qi * tile, (qi + 1) * tile
        acc = jnp.dot(p_diag, v_ref[lo:hi, :], preferred_element_type=F32)
        if qi:
            acc = acc + jnp.dot(p_off, v_ref[0:lo, :], preferred_element_type=F32)
        o_ref[lo:hi, :] = acc / l

    nxt, prev = scores(0), None
    for qi in range(n_tiles):
        now = nxt
        if qi + 1 < n_tiles:
            nxt = scores(qi + 1)
        if qi:
            values(qi - 1, *prev)
        prev = probs(qi, *now)
    values(n_tiles - 1, *prev)


def _mla_attention(q, k, v, batch, w, layer):
    t = q.shape[0]
    seq = t // batch
    heads = q.shape[1] // QK_PAD
    tile = _tile(seq, ATTN_TILE, 8)
    rows, cols = w.shape[1:]
    block = _cast_block_rows(rows, batch * heads)
    last = rows // block - 1
    w_block = lambda b, h: jnp.minimum(b * heads + h, last)
    return pl.pallas_call(
        functools.partial(_mla_attn_body, tile=tile, cast_blocks=last + 1), grid=(batch, heads),
        in_specs=[pl.BlockSpec((seq, QK_PAD), lambda b, h: (b, h)),
                  pl.BlockSpec((seq, QK_PAD), lambda b, h: (b, h)),
                  pl.BlockSpec((seq, MLA_V), lambda b, h: (b, h)),
                  pl.BlockSpec((None, block, cols), lambda b, h: (layer, w_block(b, h), 0))],
        out_specs=[pl.BlockSpec((seq, MLA_V), lambda b, h: (b, h)),
                   pl.BlockSpec((None, block, cols), lambda b, h: (0, w_block(b, h), 0))],
        out_shape=[jax.ShapeDtypeStruct((t, heads * MLA_V), F32),
                   jax.ShapeDtypeStruct((1, rows, cols), BF16)],
        compiler_params=_params(2), name="mla_attention")(q, k, v, w)


def _sb_attn_body(q_ref, k_ref, v_ref, o_ref, *, tile):
    n_tiles = q_ref.shape[0] // tile
    row = lax.broadcasted_iota(jnp.int32, (tile, tile), 0)
    col = lax.broadcasted_iota(jnp.int32, (tile, tile), 1)
    strict = col < row
    ones_below = jnp.where(row > col, 1.0, 0.0).astype(BF16)

    def logs(z):
        sign = jnp.uint32(1 << 31)
        neg_abs = lax.bitcast_convert_type(lax.bitcast_convert_type(z, jnp.uint32) | sign, F32)
        log_beta = jnp.minimum(z, 0.0) - jnp.log2(1.0 + jnp.exp2(neg_abs))
        return log_beta, log_beta - z

    def suffix_sums(log_keep):
        return jnp.dot(log_keep.astype(BF16), ones_below, preferred_element_type=F32)

    def logits(qi):
        q = q_ref[qi * tile:(qi + 1) * tile, :]
        z_diag = _dot_nt(q, k_ref[qi * tile:(qi + 1) * tile, :])
        if not qi:
            return z_diag, None
        return z_diag, jnp.concatenate(
            [_dot_nt(q, k_ref[c * tile:(c + 1) * tile, :]) for c in range(qi)], axis=0)

    def exponents(qi, z_diag, z):
        log_beta, log_keep = logs(z_diag)
        log_keep = jnp.where(strict, log_keep, 0.0)
        e_diag = log_beta + suffix_sums(log_keep)
        if not qi:
            return e_diag, None
        log_beta, keep = logs(z)
        totals = jnp.sum(keep, axis=1, keepdims=True)
        run = jnp.sum(log_keep, axis=1, keepdims=True)
        carries = [None] * qi
        for c in reversed(range(qi)):
            carries[c] = run
            run = run + totals[c * tile:(c + 1) * tile]
        return e_diag, log_beta + suffix_sums(keep) + jnp.concatenate(carries, axis=0)

    def weighted_values(qi, e_diag, e):
        lo_row, hi_row = qi * tile, (qi + 1) * tile
        a = jnp.where(strict, jnp.exp2(e_diag), 0.0)
        acc = jnp.dot(a.astype(BF16), v_ref[lo_row:hi_row, :], preferred_element_type=F32)
        if qi:
            a = jnp.exp2(e).astype(BF16)
            a = jnp.concatenate([a[c * tile:(c + 1) * tile] for c in range(qi)], axis=1)
            acc = acc + jnp.dot(a, v_ref[0:lo_row, :], preferred_element_type=F32)
        o_ref[lo_row:hi_row, :] = acc

    z_next = logits(0)
    for qi in range(n_tiles):
        z_now = z_next
        if qi + 1 < n_tiles:
            z_next = logits(qi + 1)
        weighted_values(qi, *exponents(qi, *z_now))


def _sb_attention(qkv, batch):
    t = qkv.shape[0]
    seq = t // batch
    heads = qkv.shape[1] // (3 * SB_DIM)
    tile = _tile(seq, ATTN_TILE, 8)
    return pl.pallas_call(
        functools.partial(_sb_attn_body, tile=tile), grid=(batch, heads),
        in_specs=[pl.BlockSpec((seq, SB_DIM), lambda b, h: (b, h)),
                  pl.BlockSpec((seq, SB_DIM), lambda b, h: (b, heads + h)),
                  pl.BlockSpec((seq, SB_DIM), lambda b, h: (b, 2 * heads + h))],
        out_specs=pl.BlockSpec((seq, SB_DIM), lambda b, h: (b, h)),
        out_shape=jax.ShapeDtypeStruct((t, heads * SB_DIM), F32),
        compiler_params=_params(2), name="sb_attention")(qkv, qkv, qkv)


def _cross_attn_body(q_ref, k_ref, v_ref, o_ref, *, heads):
    for h in range(heads):
        sl = slice(h * X_DIM, (h + 1) * X_DIM)
        s = _dot_nt(q_ref[:, sl], k_ref[:, sl])
        p = jnp.exp(s - jnp.max(s, axis=1, keepdims=True))
        o = jnp.dot(p.astype(BF16), v_ref[:, sl], preferred_element_type=F32)
        o_ref[:, sl] = (o / jnp.sum(p, axis=1, keepdims=True)).astype(o_ref.dtype)


def _cross_attention(q, k, v, batch):
    t, width = q.shape
    seq, mem_len = t // batch, k.shape[0] // batch
    tq = _tile(seq, 512, 8)
    n_q = seq // tq
    return pl.pallas_call(
        functools.partial(_cross_attn_body, heads=width // X_DIM), grid=(batch, n_q),
        in_specs=[pl.BlockSpec((tq, width), lambda b, i: (b * n_q + i, 0)),
                  pl.BlockSpec((mem_len, width), lambda b, i: (b, 0)),
                  pl.BlockSpec((mem_len, width), lambda b, i: (b, 0))],
        out_specs=pl.BlockSpec((tq, width), lambda b, i: (b * n_q + i, 0)),
        out_shape=jax.ShapeDtypeStruct((t, width), BF16),
        compiler_params=_params(2), name="cross_attention")(q, k, v)


def _twice(g):
    return jnp.tile(g, 2).reshape(1, 2 * g.shape[0])


def _relaid_weights(w_in_t, w_q_up, q_rank, kv_rank):
    assert 2 * MLA_ROPE == LANE and (q_rank + kv_rank) % LANE == 0
    layers = w_in_t.shape[0]
    w_q = w_q_up.reshape(layers, q_rank, MLA_HEADS, MLA_NOPE + MLA_ROPE)
    w_q = jnp.concatenate([w_q, w_q[..., MLA_NOPE:]], axis=-1)
    ws = dict(lat_t=w_in_t[:, :q_rank + kv_rank + LANE],
              q=w_q.reshape(layers, q_rank, MLA_HEADS * QK_PAD))
    return {name: w.astype(BF16) for name, w in ws.items()}


def kernel(x, mem, positions, g_attn, w_in, g_q_lat, g_kv_lat, w_q_up, w_kv_up, g_mla_q, g_mla_k,
           g_mla_out, g_sb_out, w_out, g_cross, g_mem, w_xq, w_xkv, g_xq, g_xk, w_xo, g_ffn,
           w_gate, w_up, w_down):
    batch, seq, d = x.shape
    depth = w_in.shape[0]
    q_rank, kv_rank = g_q_lat.shape[1], g_kv_lat.shape[1]
    x = x.reshape(batch * seq, d)
    mem2 = mem.reshape(-1, d)
    cos, sin = _rope_tables(positions)
    w_in_t = jnp.swapaxes(w_in, 1, 2)
    w = _relaid_weights(w_in_t, w_q_up, q_rank, kv_rank)
    for l in range(depth):
        n, cq, ckv, k_rope = _norm_latent_proj(x, g_attn[l], w["lat_t"], l, g_q_lat[l], g_kv_lat[l],
                                               _twice(g_mla_k[l, MLA_NOPE:]), cos, sin)
        q = _q_up(cq, w["q"], l, g_mla_q[l, :MLA_NOPE].reshape(1, LANE),
                  _twice(g_mla_q[l, MLA_NOPE:]), cos, sin)
        k, v = _kv_up(ckv, w_kv_up, l, g_mla_k[l, :MLA_NOPE].reshape(1, LANE), k_rope)
        o_mla, w_down_bf16 = _mla_attention(q, k, v, batch, w_down, l)
        o_sb = _sb_attention(_sb_proj(n, w_in_t, l, q_rank + kv_rank + MLA_ROPE), batch)
        mixed = _mixnorm(o_mla, o_sb, g_mla_out[l], g_sb_out[l])
        x = _matmul_residual(mixed, w_out, l, x, "out_proj")
        xq = _norm_xq_proj(x, g_cross[l], w_xq, l, g_xq[l])
        xk, xv = _xkv_proj(_rmsnorm(mem2, g_mem[l], "norm_mem"), w_xkv, l, g_xk[l])
        x, h = _matmul_residual_norm(_cross_attention(xq, xk, xv, batch), w_xo, l, x, g_ffn[l], "cross_out")
        x = _matmul_residual(_swiglu(h, w_gate, w_up, l), w_down_bf16, 0, x, "ffn_down", tm=512)
    return x.reshape(batch, seq, d)
```

```python
import functools
import math

import jax
import jax.numpy as jnp
from jax import lax
from jax.experimental import pallas as pl
from jax.experimental.pallas import tpu as pltpu

MLA_HEADS = 16
MLA_NOPE = 128
MLA_ROPE = 64
MLA_V = 128
SB_HEADS = 16
SB_DIM = 128
X_HEADS = 4
X_DIM = 128
ROPE_THETA = 10000.0
EPS = 1e-6

LANE = 128
QK_PAD = 2 * LANE
VMEM_LIMIT_BYTES = 56 * 1024 * 1024
ATTN_TILE = 256
LOG2_E = math.log2(math.e)
BF16 = jnp.bfloat16
F32 = jnp.float32


def _tile(dim, pref, align):
    if dim <= pref:
        return dim
    t = (pref // align) * align
    while t >= align:
        if dim % t == 0:
            return t
        t -= align
    return dim


def _params(ndims):
    return pltpu.CompilerParams(dimension_semantics=("arbitrary",) * ndims,
                                vmem_limit_bytes=VMEM_LIMIT_BYTES)


def _rms(y, width):
    ms = jnp.sum(y * y, axis=-1, keepdims=True) * (1.0 / width)
    return y * lax.rsqrt(ms + EPS)


def _rope(r, cos, sin):
    return r * cos + pltpu.roll(r, MLA_ROPE // 2, 1) * sin


def _rmsnorm_body(x_ref, g_ref, o_ref):
    x = x_ref[...]
    o_ref[...] = (_rms(x, x.shape[-1]) * g_ref[...]).astype(o_ref.dtype)


def _rmsnorm(x, g, name):
    m, d = x.shape
    tm = _tile(m, 256, 8)
    return pl.pallas_call(
        _rmsnorm_body, grid=(m // tm,),
        in_specs=[pl.BlockSpec((tm, d), lambda i: (i, 0)), pl.BlockSpec((1, d), lambda i: (0, 0))],
        out_specs=pl.BlockSpec((tm, d), lambda i: (i, 0)),
        out_shape=jax.ShapeDtypeStruct((m, d), BF16),
        compiler_params=_params(1), name=name)(x, g.reshape(1, d))


def _mixnorm_body(a_ref, b_ref, ga_ref, gb_ref, o_ref):
    wa = a_ref.shape[-1]
    a, b = a_ref[...], b_ref[...]
    o_ref[:, :wa] = (_rms(a, wa) * ga_ref[...]).astype(o_ref.dtype)
    o_ref[:, wa:] = (_rms(b, b.shape[-1]) * gb_ref[...]).astype(o_ref.dtype)


def _mixnorm(a, b, ga, gb):
    m, wa = a.shape
    wb = b.shape[1]
    tm = _tile(m, 256, 8)
    return pl.pallas_call(
        _mixnorm_body, grid=(m // tm,),
        in_specs=[pl.BlockSpec((tm, wa), lambda i: (i, 0)), pl.BlockSpec((tm, wb), lambda i: (i, 0)),
                  pl.BlockSpec((1, wa), lambda i: (0, 0)), pl.BlockSpec((1, wb), lambda i: (0, 0))],
        out_specs=pl.BlockSpec((tm, wa + wb), lambda i: (i, 0)),
        out_shape=jax.ShapeDtypeStruct((m, wa + wb), BF16),
        compiler_params=_params(1), name="mixnorm")(a, b, ga.reshape(1, wa), gb.reshape(1, wb))


def _rope_table_body(pos_ref, freq_ref, cos_ref, sin_ref):
    ang = pos_ref[...] * freq_ref[...]
    lane = lax.broadcasted_iota(jnp.int32, ang.shape, 1)
    sin = jnp.sin(ang)
    cos_ref[...] = jnp.where(lane < MLA_ROPE, jnp.cos(ang), 0.0)
    sin_ref[...] = jnp.where(lane < MLA_ROPE // 2, -sin, jnp.where(lane < MLA_ROPE, sin, 0.0))


def _rope_tables(positions):
    t = positions.size
    half = MLA_ROPE // 2
    inv_freq = ROPE_THETA ** (-jnp.arange(half, dtype=F32) / half)
    freq = jnp.concatenate([inv_freq, inv_freq, jnp.zeros((LANE - MLA_ROPE,), F32)]).reshape(1, LANE)
    pos = positions.astype(F32).reshape(t, 1)
    tm = _tile(t, 512, 8)
    return pl.pallas_call(
        _rope_table_body, grid=(t // tm,),
        in_specs=[pl.BlockSpec((tm, 1), lambda i: (i, 0)), pl.BlockSpec((1, LANE), lambda i: (0, 0))],
        out_specs=[pl.BlockSpec((tm, LANE), lambda i: (i, 0))] * 2,
        out_shape=[jax.ShapeDtypeStruct((t, LANE), F32)] * 2,
        compiler_params=_params(1), name="rope_tables")(pos, freq)


def _weight_spec(layer, k, tn):
    return pl.BlockSpec((None, k, tn), lambda i, j: (layer, 0, j))


def _mm_body(a_ref, w_ref, *rest, n_extra, epilogue):
    extras, outs = rest[:n_extra], rest[n_extra:]
    y = jnp.dot(a_ref[...], w_ref[...].astype(BF16), preferred_element_type=F32)
    epilogue(y, extras, outs)


def _matmul(a, w, layer, *, tm, tn, epilogue, out_shape, out_specs, extras=(), extra_specs=(), name):
    m, k = a.shape
    n = w.shape[2]
    body = functools.partial(_mm_body, n_extra=len(extras), epilogue=epilogue)
    return pl.pallas_call(
        body, grid=(m // tm, n // tn),
        in_specs=[pl.BlockSpec((tm, k), lambda i, j: (i, 0)), _weight_spec(layer, k, tn), *extra_specs],
        out_specs=out_specs, out_shape=out_shape,
        compiler_params=_params(2), name=name)(a, w, *extras)


def _row_spec(tm, width):
    return pl.BlockSpec((tm, width), lambda i, j: (i, 0))


def _const_spec(width):
    return pl.BlockSpec((1, width), lambda i, j: (0, 0))


def _tile_spec(tm, tn):
    return pl.BlockSpec((tm, tn), lambda i, j: (i, j))


def _residual_epilogue(y, extras, outs):
    outs[0][...] = extras[0][...] + y


def _matmul_residual(a, w, layer, res, name, tm=1024, tn=512):
    m, n = a.shape[0], w.shape[2]
    tm, tn = _tile(m, tm, 8), _tile(n, tn, LANE)
    return _matmul(a, w, layer, tm=tm, tn=tn, epilogue=_residual_epilogue,
                   extras=(res,), extra_specs=(_tile_spec(tm, tn),),
                   out_shape=jax.ShapeDtypeStruct((m, n), F32),
                   out_specs=_tile_spec(tm, tn), name=name)


def _latent_epilogue(y, extras, outs, *, q_rank, kv_rank):
    gq_ref, gkv_ref, gkr_ref, cos_ref, sin_ref = extras
    cq_ref, ckv_ref, kr_ref = outs
    cq_ref[...] = (_rms(y[:, :q_rank], q_rank) * gq_ref[...]).astype(cq_ref.dtype)
    ckv = y[:, q_rank:q_rank + kv_rank]
    ckv_ref[...] = (_rms(ckv, kv_rank) * gkv_ref[...]).astype(ckv_ref.dtype)
    group = y[:, q_rank + kv_rank:]
    lane = lax.broadcasted_iota(jnp.int32, group.shape, 1)
    twice = jnp.where(lane < MLA_ROPE, group, pltpu.roll(group, MLA_ROPE, 1))
    kr = _rms(twice, 2 * MLA_ROPE) * gkr_ref[...]
    kr_ref[...] = _rope(kr, cos_ref[...], sin_ref[...]).astype(kr_ref.dtype)


def _norm_latent_body(x_ref, g_ref, w_ref, gq_ref, gkv_ref, gkr_ref, cos_ref, sin_ref,
                      n_ref, cq_ref, ckv_ref, kr_ref, *, q_rank, kv_rank):
    x = x_ref[...]
    n = (_rms(x, x.shape[-1]) * g_ref[...]).astype(BF16)
    n_ref[...] = n
    y = _dot_nt(n, w_ref[...])
    _latent_epilogue(y, (gq_ref, gkv_ref, gkr_ref, cos_ref, sin_ref), (cq_ref, ckv_ref, kr_ref),
                     q_rank=q_rank, kv_rank=kv_rank)


def _norm_latent_proj(x, g_attn, w_lat_t, layer, g_q_lat, g_kv_lat, g_k_rope, cos, sin):
    m, d = x.shape
    q_rank, kv_rank = g_q_lat.shape[0], g_kv_lat.shape[0]
    n = w_lat_t.shape[1]
    tm = _tile(m, 256, 8)
    row = lambda width: pl.BlockSpec((tm, width), lambda i: (i, 0))
    const = lambda width: pl.BlockSpec((1, width), lambda i: (0, 0))
    return pl.pallas_call(
        functools.partial(_norm_latent_body, q_rank=q_rank, kv_rank=kv_rank), grid=(m // tm,),
        in_specs=[row(d), const(d), pl.BlockSpec((None, n, d), lambda i: (layer, 0, 0)),
                  const(q_rank), const(kv_rank), const(LANE), row(LANE), row(LANE)],
        out_specs=[row(d), row(q_rank), row(kv_rank), row(LANE)],
        out_shape=[jax.ShapeDtypeStruct((m, d), BF16), jax.ShapeDtypeStruct((m, q_rank), BF16),
                   jax.ShapeDtypeStruct((m, kv_rank), BF16), jax.ShapeDtypeStruct((m, LANE), BF16)],
        compiler_params=_params(1), name="norm_latent_proj")(
            x, g_attn.reshape(1, d), w_lat_t, g_q_lat.reshape(1, -1), g_kv_lat.reshape(1, -1), g_k_rope, cos, sin)


def _sb_proj_body(a_ref, w_ref, o_ref, *, q_tiles, scale):
    j = pl.program_id(1)
    y = _dot_nt(a_ref[...], w_ref[0].astype(BF16))
    o_ref[...] = (y * jnp.where(j < q_tiles, scale, 1.0)).astype(o_ref.dtype)


def _sb_proj(a, w_in_t, layer, first):
    m, k = a.shape
    n = w_in_t.shape[1] - first
    tm, tn = _tile(m, 1024, 8), _tile(n // 3, 512, LANE)
    body = functools.partial(_sb_proj_body, q_tiles=(n // 3) // tn, scale=LOG2_E / math.sqrt(SB_DIM))
    return pl.pallas_call(
        body, grid=(m // tm, n // tn),
        in_specs=[pl.BlockSpec((tm, k), lambda i, j: (i, 0)),
                  pl.BlockSpec((pl.Element(1), pl.Element(tn), pl.Element(k)),
                               lambda i, j: (layer, pl.multiple_of(first + j * tn, 8), 0))],
        out_specs=_tile_spec(tm, tn), out_shape=jax.ShapeDtypeStruct((m, n), BF16),
        compiler_params=_params(2), name="sb_proj")(a, w_in_t)


def _q_up_body(cq_ref, w_ref, gn_ref, gr_ref, cos_ref, sin_ref, o_ref, *, heads, scale):
    cq = cq_ref[...]
    cos, sin = cos_ref[...] * scale, sin_ref[...] * scale
    gn = gn_ref[...] * scale
    row = lax.broadcasted_iota(jnp.int32, (2 * QK_PAD, QK_PAD), 0)
    col = lax.broadcasted_iota(jnp.int32, (2 * QK_PAD, QK_PAD), 1)
    group_mean = jnp.where((row % QK_PAD) // LANE == col // LANE, 1.0 / LANE, 0.0).astype(BF16)
    ys = [jnp.dot(cq, w_ref[:, h * QK_PAD:(h + 1) * QK_PAD], preferred_element_type=F32) for h in range(heads)]
    parts = []
    for y in ys:
        sq = y * y
        hi = sq.astype(BF16)
        parts.append(jnp.concatenate([hi, (sq - hi.astype(F32)).astype(BF16)], axis=1))
    means = [jnp.dot(p, group_mean, preferred_element_type=F32) for p in parts]
    for h in range(heads):
        lo = h * QK_PAD
        yn = ys[h] * lax.rsqrt(means[h] + EPS)
        o_ref[:, lo:lo + MLA_NOPE] = (yn[:, :MLA_NOPE] * gn).astype(o_ref.dtype)
        r = yn[:, MLA_NOPE:] * gr_ref[...]
        o_ref[:, lo + MLA_NOPE:lo + QK_PAD] = _rope(r, cos, sin).astype(o_ref.dtype)


def _q_up(cq, w_q, layer, g_nope, g_rope, cos, sin):
    m, k = cq.shape
    n = w_q.shape[2]
    tm, heads = _tile(m, 1024, 8), 4
    tn = heads * QK_PAD
    body = functools.partial(_q_up_body, heads=heads, scale=LOG2_E / math.sqrt(MLA_NOPE + MLA_ROPE))
    return pl.pallas_call(
        body, grid=(m // tm, n // tn),
        in_specs=[pl.BlockSpec((tm, k), lambda i, j: (i, 0)), _weight_spec(layer, k, tn),
                  _const_spec(LANE), _const_spec(LANE), _row_spec(tm, LANE), _row_spec(tm, LANE)],
        out_specs=_tile_spec(tm, tn), out_shape=jax.ShapeDtypeStruct((m, n), BF16),
        compiler_params=_params(2), name="q_up")(cq, w_q, g_nope, g_rope, cos, sin)


def _kv_up_epilogue(y, extras, outs, *, heads):
    gn_ref, kr_ref = extras
    k_ref, v_ref = outs
    pair = MLA_NOPE + MLA_V
    for h in range(heads):
        k = _rms(y[:, h * pair:h * pair + MLA_NOPE], MLA_NOPE) * gn_ref[...]
        k_ref[:, h * QK_PAD:h * QK_PAD + MLA_NOPE] = k.astype(k_ref.dtype)
        k_ref[:, h * QK_PAD + MLA_NOPE:(h + 1) * QK_PAD] = kr_ref[...]
        v_ref[:, h * MLA_V:(h + 1) * MLA_V] = y[:, h * pair + MLA_NOPE:(h + 1) * pair].astype(v_ref.dtype)


def _kv_up(ckv, w_kv_up, layer, g_nope, k_rope):
    assert MLA_NOPE == LANE and MLA_V == LANE
    m, n = ckv.shape[0], w_kv_up.shape[2]
    all_heads = n // (MLA_NOPE + MLA_V)
    tm, heads = _tile(m, 1024, 8), 4
    return _matmul(
        ckv, w_kv_up, layer, tm=tm, tn=heads * (MLA_NOPE + MLA_V),
        epilogue=functools.partial(_kv_up_epilogue, heads=heads),
        extras=(g_nope, k_rope), extra_specs=(_const_spec(LANE), _row_spec(tm, LANE)),
        out_shape=[jax.ShapeDtypeStruct((m, all_heads * QK_PAD), BF16),
                   jax.ShapeDtypeStruct((m, all_heads * MLA_V), BF16)],
        out_specs=[_tile_spec(tm, heads * QK_PAD), _tile_spec(tm, heads * MLA_V)], name="kv_up")


def _headnorm_epilogue(y, extras, outs, *, heads, scale):
    g_ref = extras[0]
    for h in range(heads):
        sl = slice(h * X_DIM, (h + 1) * X_DIM)
        outs[0][:, sl] = (_rms(y[:, sl], X_DIM) * g_ref[...] * scale).astype(outs[0].dtype)


def _norm_xq_body(x_ref, g_ref, w_ref, gq_ref, o_ref, *, heads, scale):
    x = x_ref[...]
    xn = (_rms(x, x.shape[-1]) * g_ref[...]).astype(BF16)
    y = jnp.dot(xn, w_ref[...].astype(BF16), preferred_element_type=F32)
    _headnorm_epilogue(y, (gq_ref,), (o_ref,), heads=heads, scale=scale)


def _norm_xq_proj(x, g_cross, w_xq, layer, g_xq):
    m, d = x.shape
    n = w_xq.shape[2]
    tm = _tile(m, 512, 8)
    body = functools.partial(_norm_xq_body, heads=n // X_DIM, scale=1.0 / math.sqrt(X_DIM))
    return pl.pallas_call(
        body, grid=(m // tm,),
        in_specs=[pl.BlockSpec((tm, d), lambda i: (i, 0)), pl.BlockSpec((1, d), lambda i: (0, 0)),
                  pl.BlockSpec((None, d, n), lambda i: (layer, 0, 0)),
                  pl.BlockSpec((1, X_DIM), lambda i: (0, 0))],
        out_specs=pl.BlockSpec((tm, n), lambda i: (i, 0)),
        out_shape=jax.ShapeDtypeStruct((m, n), BF16),
        compiler_params=_params(1), name="norm_xq_proj")(x, g_cross.reshape(1, d), w_xq, g_xq.reshape(1, X_DIM))


def _out_norm_body(a_ref, w_ref, res_ref, g_ref, x_ref, h_ref):
    x = res_ref[...] + jnp.dot(a_ref[...], w_ref[...].astype(BF16), preferred_element_type=F32)
    x_ref[...] = x
    h_ref[...] = (_rms(x, x.shape[-1]) * g_ref[...]).astype(h_ref.dtype)


def _matmul_residual_norm(a, w, layer, res, g, name):
    m, k = a.shape
    n = w.shape[2]
    tm = _tile(m, 256, 8)
    return pl.pallas_call(
        _out_norm_body, grid=(m // tm,),
        in_specs=[pl.BlockSpec((tm, k), lambda i: (i, 0)), pl.BlockSpec((None, k, n), lambda i: (layer, 0, 0)),
                  pl.BlockSpec((tm, n), lambda i: (i, 0)), pl.BlockSpec((1, n), lambda i: (0, 0))],
        out_specs=[pl.BlockSpec((tm, n), lambda i: (i, 0))] * 2,
        out_shape=[jax.ShapeDtypeStruct((m, n), F32), jax.ShapeDtypeStruct((m, n), BF16)],
        compiler_params=_params(1), name=name)(a, w, res, g.reshape(1, n))


def _xkv_epilogue(y, extras, outs, *, heads):
    g_ref = extras[0]
    k_ref, v_ref = outs
    for h in range(heads):
        sl = slice(h * X_DIM, (h + 1) * X_DIM)
        k = _rms(y[:, 2 * h * X_DIM:(2 * h + 1) * X_DIM], X_DIM) * g_ref[...]
        k_ref[:, sl] = k.astype(k_ref.dtype)
        v_ref[:, sl] = y[:, (2 * h + 1) * X_DIM:(2 * h + 2) * X_DIM].astype(v_ref.dtype)


def _xkv_proj(a, w_xkv, layer, g_xk):
    m, n = a.shape[0], w_xkv.shape[2]
    tm, width = _tile(m, 512, 8), n // 2
    return _matmul(
        a, w_xkv, layer, tm=tm, tn=n, epilogue=functools.partial(_xkv_epilogue, heads=width // X_DIM),
        extras=(g_xk.reshape(1, X_DIM),), extra_specs=(_const_spec(X_DIM),),
        out_shape=[jax.ShapeDtypeStruct((m, width), BF16)] * 2,
        out_specs=[_row_spec(tm, width)] * 2, name="xkv_proj")


def _swiglu_body(a_ref, wg_ref, wu_ref, o_ref):
    a = a_ref[...]
    g = jnp.dot(a, wg_ref[...].astype(BF16), preferred_element_type=F32)
    u = jnp.dot(a, wu_ref[...].astype(BF16), preferred_element_type=F32)
    o_ref[...] = (g / (1.0 + jnp.exp(-g)) * u).astype(o_ref.dtype)


def _swiglu(a, w_gate, w_up, layer):
    m, k = a.shape
    n = w_gate.shape[2]
    tm, tn = _tile(m, 1024, 8), _tile(n, 256, LANE)
    return pl.pallas_call(
        _swiglu_body, grid=(m // tm, n // tn),
        in_specs=[pl.BlockSpec((tm, k), lambda i, j: (i, 0)),
                  _weight_spec(layer, k, tn), _weight_spec(layer, k, tn)],
        out_specs=_tile_spec(tm, tn), out_shape=jax.ShapeDtypeStruct((m, n), BF16),
        compiler_params=_params(2), name="swiglu")(a, w_gate, w_up)


def _dot_nt(a, b):
    return lax.dot_general(a, b, (((1,), (1,)), ((), ())), preferred_element_type=F32)


def _cast_block_rows(rows, steps):
    block = -(-rows // steps)
    while rows % block or block % 16:
        block += 1
    return block


def _mla_attn_body(q_ref, k_ref, v_ref, w_ref, o_ref, w_out_ref, *, tile, cast_blocks):
    step = pl.program_id(0) * pl.num_programs(1) + pl.program_id(1)

    @pl.when(step < cast_blocks)
    def _():
        w_out_ref[...] = w_ref[...].astype(w_out_ref.dtype)

    n_tiles = q_ref.shape[0] // tile
    row = lax.broadcasted_iota(jnp.int32, (tile, tile), 0)
    col = lax.broadcasted_iota(jnp.int32, (tile, tile), 1)

    def scores(qi):
        lo, hi = qi * tile, (qi + 1) * tile
        s = _dot_nt(q_ref[lo:hi, :], k_ref[0:hi, :])
        last = jnp.where(col <= row, s[:, lo:], -jnp.inf)
        return jnp.concatenate([s[:, :lo], last], axis=1) if qi else last

    def probs(s):
        p = jnp.exp2(s - jnp.max(s, axis=1, keepdims=True))
        return p.astype(BF16), jnp.sum(p, axis=1, keepdims=True)

    def values(qi, p, l):
        lo, hi = qi * tile, (qi + 1) * tile
        o_ref[lo:hi, :] = jnp.dot(p, v_ref[0:hi, :], preferred_element_type=F32) / l

    nxt, prev = scores(0), None
    for qi in range(n_tiles):
        now = nxt
        if qi + 1 < n_tiles:
            nxt = scores(qi + 1)
        if qi:
            values(qi - 1, *prev)
        prev = probs(now)
    values(n_tiles - 1, *prev)


def _mla_attention(q, k, v, batch, w, layer):
    t = q.shape[0]
    seq = t // batch
    heads = q.shape[1] // QK_PAD
    tile = _tile(seq, ATTN_TILE, 8)
    rows, cols = w.shape[1:]
    block = _cast_block_rows(rows, batch * heads)
    last = rows // block - 1
    w_block = lambda b, h: jnp.minimum(b * heads + h, last)
    return pl.pallas_call(
        functools.partial(_mla_attn_body, tile=tile, cast_blocks=last + 1), grid=(batch, heads),
        in_specs=[pl.BlockSpec((seq, QK_PAD), lambda b, h: (b, h)),
                  pl.BlockSpec((seq, QK_PAD), lambda b, h: (b, h)),
                  pl.BlockSpec((seq, MLA_V), lambda b, h: (b, h)),
                  pl.BlockSpec((None, block, cols), lambda b, h: (layer, w_block(b, h), 0))],
        out_specs=[pl.BlockSpec((seq, MLA_V), lambda b, h: (b, h)),
                   pl.BlockSpec((None, block, cols), lambda b, h: (0, w_block(b, h), 0))],
        out_shape=[jax.ShapeDtypeStruct((t, heads * MLA_V), F32),
                   jax.ShapeDtypeStruct((1, rows, cols), BF16)],
        compiler_params=_params(2), name="mla_attention")(q, k, v, w)


def _sb_attn_body(q_ref, k_ref, v_ref, o_ref, *, tile):
    n_tiles = q_ref.shape[0] // tile
    row = lax.broadcasted_iota(jnp.int32, (tile, tile), 0)
    col = lax.broadcasted_iota(jnp.int32, (tile, tile), 1)
    strict = col < row
    ones_below = jnp.where(row > col, 1.0, 0.0).astype(BF16)

    def logs(z):
        sign = jnp.uint32(1 << 31)
        neg_abs = lax.bitcast_convert_type(lax.bitcast_convert_type(z, jnp.uint32) | sign, F32)
        log_beta = jnp.minimum(z, 0.0) - jnp.log2(1.0 + jnp.exp2(neg_abs))
        return log_beta, log_beta - z

    def suffix_sums(log_keep):
        return jnp.dot(log_keep.astype(BF16), ones_below, preferred_element_type=F32)

    def logits(qi):
        q = q_ref[qi * tile:(qi + 1) * tile, :]
        return jnp.concatenate(
            [_dot_nt(q, k_ref[c * tile:(c + 1) * tile, :]) for c in range(qi + 1)], axis=0)

    def masked_diagonal(x, qi, fill):
        last = jnp.where(strict, x[qi * tile:], fill)
        return jnp.concatenate([x[:qi * tile], last], axis=0) if qi else last

    def weighted_values(qi, z):
        log_beta, keep = logs(z)
        keep = masked_diagonal(keep, qi, 0.0)
        totals = jnp.sum(keep, axis=1, keepdims=True)
        run = jnp.zeros((tile, 1), F32)
        carries = [None] * (qi + 1)
        for c in reversed(range(qi + 1)):
            carries[c] = run
            run = run + totals[c * tile:(c + 1) * tile]
        a = jnp.exp2(log_beta + suffix_sums(keep) + jnp.concatenate(carries, axis=0))
        a = masked_diagonal(a, qi, 0.0).astype(BF16)
        a = jnp.concatenate([a[c * tile:(c + 1) * tile] for c in range(qi + 1)], axis=1)
        o_ref[qi * tile:(qi + 1) * tile, :] = jnp.dot(a, v_ref[0:(qi + 1) * tile, :],
                                                      preferred_element_type=F32)

    z_next = logits(0)
    for qi in range(n_tiles):
        z_now = z_next
        if qi + 1 < n_tiles:
            z_next = logits(qi + 1)
        weighted_values(qi, z_now)


def _sb_attention(qkv, batch):
    t = qkv.shape[0]
    seq = t // batch
    heads = qkv.shape[1] // (3 * SB_DIM)
    tile = _tile(seq, ATTN_TILE, 8)
    return pl.pallas_call(
        functools.partial(_sb_attn_body, tile=tile), grid=(batch, heads),
        in_specs=[pl.BlockSpec((seq, SB_DIM), lambda b, h: (b, h)),
                  pl.BlockSpec((seq, SB_DIM), lambda b, h: (b, heads + h)),
                  pl.BlockSpec((seq, SB_DIM), lambda b, h: (b, 2 * heads + h))],
        out_specs=pl.BlockSpec((seq, SB_DIM), lambda b, h: (b, h)),
        out_shape=jax.ShapeDtypeStruct((t, heads * SB_DIM), F32),
        compiler_params=_params(2), name="sb_attention")(qkv, qkv, qkv)


def _cross_attn_body(q_ref, k_ref, v_ref, o_ref, *, heads):
    for h in range(heads):
        sl = slice(h * X_DIM, (h + 1) * X_DIM)
        s = _dot_nt(q_ref[:, sl], k_ref[:, sl])
        p = jnp.exp(s - jnp.max(s, axis=1, keepdims=True))
        o = jnp.dot(p.astype(BF16), v_ref[:, sl], preferred_element_type=F32)
        o_ref[:, sl] = (o / jnp.sum(p, axis=1, keepdims=True)).astype(o_ref.dtype)


def _cross_attention(q, k, v, batch):
    t, width = q.shape
    seq, mem_len = t // batch, k.shape[0] // batch
    tq = _tile(seq, 512, 8)
    n_q = seq // tq
    return pl.pallas_call(
        functools.partial(_cross_attn_body, heads=width // X_DIM), grid=(batch, n_q),
        in_specs=[pl.BlockSpec((tq, width), lambda b, i: (b * n_q + i, 0)),
                  pl.BlockSpec((mem_len, width), lambda b, i: (b, 0)),
                  pl.BlockSpec((mem_len, width), lambda b, i: (b, 0))],
        out_specs=pl.BlockSpec((tq, width), lambda b, i: (b * n_q + i, 0)),
        out_shape=jax.ShapeDtypeStruct((t, width), BF16),
        compiler_params=_params(2), name="cross_attention")(q, k, v)


def _twice(g):
    return jnp.tile(g, 2).reshape(1, 2 * g.shape[0])


def _relaid_weights(w_in_t, w_q_up, q_rank, kv_rank):
    assert 2 * MLA_ROPE == LANE and (q_rank + kv_rank) % LANE == 0
    layers = w_in_t.shape[0]
    w_q = w_q_up.reshape(layers, q_rank, MLA_HEADS, MLA_NOPE + MLA_ROPE)
    w_q = jnp.concatenate([w_q, w_q[..., MLA_NOPE:]], axis=-1)
    ws = dict(lat_t=w_in_t[:, :q_rank + kv_rank + LANE],
              q=w_q.reshape(layers, q_rank, MLA_HEADS * QK_PAD))
    return {name: w.astype(BF16) for name, w in ws.items()}


def kernel(x, mem, positions, g_attn, w_in, g_q_lat, g_kv_lat, w_q_up, w_kv_up, g_mla_q, g_mla_k,
           g_mla_out, g_sb_out, w_out, g_cross, g_mem, w_xq, w_xkv, g_xq, g_xk, w_xo, g_ffn,
           w_gate, w_up, w_down):
    batch, seq, d = x.shape
    depth = w_in.shape[0]
    q_rank, kv_rank = g_q_lat.shape[1], g_kv_lat.shape[1]
    x = x.reshape(batch * seq, d)
    mem2 = mem.reshape(-1, d)
    cos, sin = _rope_tables(positions)
    w_in_t = jnp.swapaxes(w_in, 1, 2)
    w = _relaid_weights(w_in_t, w_q_up, q_rank, kv_rank)
    for l in range(depth):
        n, cq, ckv, k_rope = _norm_latent_proj(x, g_attn[l], w["lat_t"], l, g_q_lat[l], g_kv_lat[l],
                                               _twice(g_mla_k[l, MLA_NOPE:]), cos, sin)
        q = _q_up(cq, w["q"], l, g_mla_q[l, :MLA_NOPE].reshape(1, LANE),
                  _twice(g_mla_q[l, MLA_NOPE:]), cos, sin)
        k, v = _kv_up(ckv, w_kv_up, l, g_mla_k[l, :MLA_NOPE].reshape(1, LANE), k_rope)
        o_mla, w_down_bf16 = _mla_attention(q, k, v, batch, w_down, l)
        o_sb = _sb_attention(_sb_proj(n, w_in_t, l, q_rank + kv_rank + MLA_ROPE), batch)
        mixed = _mixnorm(o_mla, o_sb, g_mla_out[l], g_sb_out[l])
        x = _matmul_residual(mixed, w_out, l, x, "out_proj")
        xq = _norm_xq_proj(x, g_cross[l], w_xq, l, g_xq[l])
        xk, xv = _xkv_proj(_rmsnorm(mem2, g_mem[l], "norm_mem"), w_xkv, l, g_xk[l])
        x, h = _matmul_residual_norm(_cross_attention(xq, xk, xv, batch), w_xo, l, x, g_ffn[l], "cross_out")
        x = _matmul_residual(_swiglu(h, w_gate, w_up, l), w_down_bf16, 0, x, "ffn_down", tm=512)
    return x.reshape(batch, seq, d)
```

```python
import functools
import math

import jax
import jax.numpy as jnp
from jax import lax
from jax.experimental import pallas as pl
from jax.experimental.pallas import tpu as pltpu

MLA_HEADS = 16
MLA_NOPE = 128
MLA_ROPE = 64
MLA_V = 128
SB_HEADS = 16
SB_DIM = 128
X_HEADS = 4
X_DIM = 128
ROPE_THETA = 10000.0
EPS = 1e-6

LANE = 128
QK_PAD = 2 * LANE
VMEM_LIMIT_BYTES = 56 * 1024 * 1024
ATTN_TILE = 256
LOG2_E = math.log2(math.e)
BF16 = jnp.bfloat16
F32 = jnp.float32


def _tile(dim, pref, align):
    if dim <= pref:
        return dim
    t = (pref // align) * align
    while t >= align:
        if dim % t == 0:
            return t
        t -= align
    return dim


def _params(ndims):
    return pltpu.CompilerParams(dimension_semantics=("arbitrary",) * ndims,
                                vmem_limit_bytes=VMEM_LIMIT_BYTES)


def _rms(y, width):
    ms = jnp.sum(y * y, axis=-1, keepdims=True) * (1.0 / width)
    return y * lax.rsqrt(ms + EPS)


def _rope(r, cos, sin):
    return r * cos + pltpu.roll(r, MLA_ROPE // 2, 1) * sin


def _rmsnorm_body(x_ref, g_ref, o_ref):
    x = x_ref[...]
    o_ref[...] = (_rms(x, x.shape[-1]) * g_ref[...]).astype(o_ref.dtype)


def _rmsnorm(x, g, name):
    m, d = x.shape
    tm = _tile(m, 256, 8)
    return pl.pallas_call(
        _rmsnorm_body, grid=(m // tm,),
        in_specs=[pl.BlockSpec((tm, d), lambda i: (i, 0)), pl.BlockSpec((1, d), lambda i: (0, 0))],
        out_specs=pl.BlockSpec((tm, d), lambda i: (i, 0)),
        out_shape=jax.ShapeDtypeStruct((m, d), BF16),
        compiler_params=_params(1), name=name)(x, g.reshape(1, d))


def _mixnorm_body(a_ref, b_ref, ga_ref, gb_ref, o_ref):
    wa = a_ref.shape[-1]
    a, b = a_ref[...], b_ref[...]
    o_ref[:, :wa] = (_rms(a, wa) * ga_ref[...]).astype(o_ref.dtype)
    o_ref[:, wa:] = (_rms(b, b.shape[-1]) * gb_ref[...]).astype(o_ref.dtype)


def _mixnorm(a, b, ga, gb):
    m, wa = a.shape
    wb = b.shape[1]
    tm = _tile(m, 256, 8)
    return pl.pallas_call(
        _mixnorm_body, grid=(m // tm,),
        in_specs=[pl.BlockSpec((tm, wa), lambda i: (i, 0)), pl.BlockSpec((tm, wb), lambda i: (i, 0)),
                  pl.BlockSpec((1, wa), lambda i: (0, 0)), pl.BlockSpec((1, wb), lambda i: (0, 0))],
        out_specs=pl.BlockSpec((tm, wa + wb), lambda i: (i, 0)),
        out_shape=jax.ShapeDtypeStruct((m, wa + wb), BF16),
        compiler_params=_params(1), name="mixnorm")(a, b, ga.reshape(1, wa), gb.reshape(1, wb))


def _rope_table_body(pos_ref, freq_ref, cos_ref, sin_ref):
    ang = pos_ref[...] * freq_ref[...]
    lane = lax.broadcasted_iota(jnp.int32, ang.shape, 1)
    sin = jnp.sin(ang)
    cos_ref[...] = jnp.where(lane < MLA_ROPE, jnp.cos(ang), 0.0)
    sin_ref[...] = jnp.where(lane < MLA_ROPE // 2, -sin, jnp.where(lane < MLA_ROPE, sin, 0.0))


def _rope_tables(positions):
    t = positions.size
    half = MLA_ROPE // 2
    inv_freq = ROPE_THETA ** (-jnp.arange(half, dtype=F32) / half)
    freq = jnp.concatenate([inv_freq, inv_freq, jnp.zeros((LANE - MLA_ROPE,), F32)]).reshape(1, LANE)
    pos = positions.astype(F32).reshape(t, 1)
    tm = _tile(t, 512, 8)
    return pl.pallas_call(
        _rope_table_body, grid=(t // tm,),
        in_specs=[pl.BlockSpec((tm, 1), lambda i: (i, 0)), pl.BlockSpec((1, LANE), lambda i: (0, 0))],
        out_specs=[pl.BlockSpec((tm, LANE), lambda i: (i, 0))] * 2,
        out_shape=[jax.ShapeDtypeStruct((t, LANE), F32)] * 2,
        compiler_params=_params(1), name="rope_tables")(pos, freq)


def _weight_spec(layer, k, tn):
    return pl.BlockSpec((None, k, tn), lambda i, j: (layer, 0, j))


def _lhs_spec(tm, k, single_buffered):
    mode = dict(pipeline_mode=pl.Buffered(1)) if single_buffered else {}
    return pl.BlockSpec((tm, k), lambda i, j: (i, 0), **mode)


def _mm_body(a_ref, w_ref, *rest, n_extra, epilogue):
    extras, outs = rest[:n_extra], rest[n_extra:]
    y = jnp.dot(a_ref[...], w_ref[...].astype(BF16), preferred_element_type=F32)
    epilogue(y, extras, outs)


def _matmul(a, w, layer, *, tm, tn, epilogue, out_shape, out_specs, extras=(), extra_specs=(), name,
            single_buffered_lhs=False):
    m, k = a.shape
    n = w.shape[2]
    body = functools.partial(_mm_body, n_extra=len(extras), epilogue=epilogue)
    return pl.pallas_call(
        body, grid=(m // tm, n // tn),
        in_specs=[_lhs_spec(tm, k, single_buffered_lhs), _weight_spec(layer, k, tn), *extra_specs],
        out_specs=out_specs, out_shape=out_shape,
        compiler_params=_params(2), name=name)(a, w, *extras)


def _row_spec(tm, width):
    return pl.BlockSpec((tm, width), lambda i, j: (i, 0))


def _const_spec(width):
    return pl.BlockSpec((1, width), lambda i, j: (0, 0))


def _tile_spec(tm, tn):
    return pl.BlockSpec((tm, tn), lambda i, j: (i, j))


def _residual_epilogue(y, extras, outs):
    outs[0][...] = extras[0][...] + y


def _matmul_residual(a, w, layer, res, name, tm=1024, tn=512, single_buffered_lhs=False):
    m, n = a.shape[0], w.shape[2]
    tm, tn = _tile(m, tm, 8), _tile(n, tn, LANE)
    return _matmul(a, w, layer, tm=tm, tn=tn, epilogue=_residual_epilogue, single_buffered_lhs=single_buffered_lhs,
                   extras=(res,), extra_specs=(_tile_spec(tm, tn),),
                   out_shape=jax.ShapeDtypeStruct((m, n), F32),
                   out_specs=_tile_spec(tm, tn), name=name)


def _latent_epilogue(y, extras, outs, *, q_rank, kv_rank):
    gq_ref, gkv_ref, gkr_ref, cos_ref, sin_ref = extras
    cq_ref, ckv_ref, kr_ref = outs
    cq_ref[...] = (_rms(y[:, :q_rank], q_rank) * gq_ref[...]).astype(cq_ref.dtype)
    ckv = y[:, q_rank:q_rank + kv_rank]
    ckv_ref[...] = (_rms(ckv, kv_rank) * gkv_ref[...]).astype(ckv_ref.dtype)
    group = y[:, q_rank + kv_rank:]
    lane = lax.broadcasted_iota(jnp.int32, group.shape, 1)
    twice = jnp.where(lane < MLA_ROPE, group, pltpu.roll(group, MLA_ROPE, 1))
    kr = _rms(twice, 2 * MLA_ROPE) * gkr_ref[...]
    kr_ref[...] = _rope(kr, cos_ref[...], sin_ref[...]).astype(kr_ref.dtype)


def _norm_latent_body(x_ref, g_ref, w_ref, gq_ref, gkv_ref, gkr_ref, cos_ref, sin_ref,
                      n_ref, cq_ref, ckv_ref, kr_ref, *, q_rank, kv_rank):
    half = x_ref.shape[0] // 2
    halves = [slice(0, half), slice(half, 2 * half)]
    ys = []
    for rows in halves:
        x = x_ref[rows, :]
        n = (_rms(x, x.shape[-1]) * g_ref[...]).astype(BF16)
        n_ref[rows, :] = n
        ys.append(_dot_nt(n, w_ref[...]))
    for rows, y in zip(halves, ys):
        _latent_epilogue(y, (gq_ref, gkv_ref, gkr_ref, cos_ref.at[rows, :], sin_ref.at[rows, :]),
                         (cq_ref.at[rows, :], ckv_ref.at[rows, :], kr_ref.at[rows, :]),
                         q_rank=q_rank, kv_rank=kv_rank)


def _norm_latent_proj(x, g_attn, w_lat_t, layer, g_q_lat, g_kv_lat, g_k_rope, cos, sin):
    m, d = x.shape
    q_rank, kv_rank = g_q_lat.shape[0], g_kv_lat.shape[0]
    n = w_lat_t.shape[1]
    tm = _tile(m, 512, 16)
    row = lambda width: pl.BlockSpec((tm, width), lambda i: (i, 0))
    const = lambda width: pl.BlockSpec((1, width), lambda i: (0, 0))
    weight = pl.BlockSpec((None, n, d), lambda i: (layer, 0, 0), pipeline_mode=pl.Buffered(1))
    return pl.pallas_call(
        functools.partial(_norm_latent_body, q_rank=q_rank, kv_rank=kv_rank), grid=(m // tm,),
        in_specs=[row(d), const(d), weight,
                  const(q_rank), const(kv_rank), const(LANE), row(LANE), row(LANE)],
        out_specs=[row(d), row(q_rank), row(kv_rank), row(LANE)],
        out_shape=[jax.ShapeDtypeStruct((m, d), BF16), jax.ShapeDtypeStruct((m, q_rank), BF16),
                   jax.ShapeDtypeStruct((m, kv_rank), BF16), jax.ShapeDtypeStruct((m, LANE), BF16)],
        compiler_params=_params(1), name="norm_latent_proj")(
            x, g_attn.reshape(1, d), w_lat_t, g_q_lat.reshape(1, -1), g_kv_lat.reshape(1, -1), g_k_rope, cos, sin)


def _sb_proj_body(a_ref, w_ref, o_ref, *, q_tiles, scale):
    j = pl.program_id(1)
    y = _dot_nt(a_ref[...], w_ref[0].astype(BF16))
    o_ref[...] = (y * jnp.where(j < q_tiles, scale, 1.0)).astype(o_ref.dtype)


def _sb_proj(a, w_in_t, layer, first):
    m, k = a.shape
    n = w_in_t.shape[1] - first
    tm, tn = _tile(m, 2048, 8), _tile(n // 3, 512, LANE)
    body = functools.partial(_sb_proj_body, q_tiles=(n // 3) // tn, scale=LOG2_E / math.sqrt(SB_DIM))
    return pl.pallas_call(
        body, grid=(m // tm, n // tn),
        in_specs=[_lhs_spec(tm, k, True),
                  pl.BlockSpec((pl.Element(1), pl.Element(tn), pl.Element(k)),
                               lambda i, j: (layer, pl.multiple_of(first + j * tn, 8), 0))],
        out_specs=_tile_spec(tm, tn), out_shape=jax.ShapeDtypeStruct((m, n), BF16),
        compiler_params=_params(2), name="sb_proj")(a, w_in_t)


def _q_up_body(cq_ref, w_ref, gn_ref, gr_ref, cos_ref, sin_ref, o_ref, *, heads, scale):
    cq = cq_ref[...]
    cos, sin = cos_ref[...] * scale, sin_ref[...] * scale
    gn = gn_ref[...] * scale
    row = lax.broadcasted_iota(jnp.int32, (2 * QK_PAD, QK_PAD), 0)
    col = lax.broadcasted_iota(jnp.int32, (2 * QK_PAD, QK_PAD), 1)
    group_mean = jnp.where((row % QK_PAD) // LANE == col // LANE, 1.0 / LANE, 0.0).astype(BF16)
    ys = [jnp.dot(cq, w_ref[:, h * QK_PAD:(h + 1) * QK_PAD], preferred_element_type=F32) for h in range(heads)]
    parts = []
    for y in ys:
        sq = y * y
        hi = sq.astype(BF16)
        parts.append(jnp.concatenate([hi, (sq - hi.astype(F32)).astype(BF16)], axis=1))
    means = [jnp.dot(p, group_mean, preferred_element_type=F32) for p in parts]
    for h in range(heads):
        lo = h * QK_PAD
        yn = ys[h] * lax.rsqrt(means[h] + EPS)
        o_ref[:, lo:lo + MLA_NOPE] = (yn[:, :MLA_NOPE] * gn).astype(o_ref.dtype)
        r = yn[:, MLA_NOPE:] * gr_ref[...]
        o_ref[:, lo + MLA_NOPE:lo + QK_PAD] = _rope(r, cos, sin).astype(o_ref.dtype)


def _q_up(cq, w_q, layer, g_nope, g_rope, cos, sin):
    m, k = cq.shape
    n = w_q.shape[2]
    tm, heads = _tile(m, 1024, 8), 4
    tn = heads * QK_PAD
    body = functools.partial(_q_up_body, heads=heads, scale=LOG2_E / math.sqrt(MLA_NOPE + MLA_ROPE))
    return pl.pallas_call(
        body, grid=(m // tm, n // tn),
        in_specs=[pl.BlockSpec((tm, k), lambda i, j: (i, 0)), _weight_spec(layer, k, tn),
                  _const_spec(LANE), _const_spec(LANE), _row_spec(tm, LANE), _row_spec(tm, LANE)],
        out_specs=_tile_spec(tm, tn), out_shape=jax.ShapeDtypeStruct((m, n), BF16),
        compiler_params=_params(2), name="q_up")(cq, w_q, g_nope, g_rope, cos, sin)


def _kv_up_epilogue(y, extras, outs, *, heads):
    gn_ref, kr_ref = extras
    k_ref, v_ref = outs
    pair = MLA_NOPE + MLA_V
    for h in range(heads):
        k = _rms(y[:, h * pair:h * pair + MLA_NOPE], MLA_NOPE) * gn_ref[...]
        k_ref[:, h * QK_PAD:h * QK_PAD + MLA_NOPE] = k.astype(k_ref.dtype)
        k_ref[:, h * QK_PAD + MLA_NOPE:(h + 1) * QK_PAD] = kr_ref[...]
        v_ref[:, h * MLA_V:(h + 1) * MLA_V] = y[:, h * pair + MLA_NOPE:(h + 1) * pair].astype(v_ref.dtype)


def _kv_up(ckv, w_kv_up, layer, g_nope, k_rope):
    assert MLA_NOPE == LANE and MLA_V == LANE
    m, n = ckv.shape[0], w_kv_up.shape[2]
    all_heads = n // (MLA_NOPE + MLA_V)
    tm, heads = _tile(m, 1024, 8), 4
    return _matmul(
        ckv, w_kv_up, layer, tm=tm, tn=heads * (MLA_NOPE + MLA_V),
        epilogue=functools.partial(_kv_up_epilogue, heads=heads),
        extras=(g_nope, k_rope), extra_specs=(_const_spec(LANE), _row_spec(tm, LANE)),
        out_shape=[jax.ShapeDtypeStruct((m, all_heads * QK_PAD), BF16),
                   jax.ShapeDtypeStruct((m, all_heads * MLA_V), BF16)],
        out_specs=[_tile_spec(tm, heads * QK_PAD), _tile_spec(tm, heads * MLA_V)], name="kv_up")


def _headnorm_epilogue(y, extras, outs, *, heads, scale):
    g_ref = extras[0]
    for h in range(heads):
        sl = slice(h * X_DIM, (h + 1) * X_DIM)
        outs[0][:, sl] = (_rms(y[:, sl], X_DIM) * g_ref[...] * scale).astype(outs[0].dtype)


def _norm_xq_body(x_ref, g_ref, w_ref, gq_ref, o_ref, *, heads, scale):
    x = x_ref[...]
    xn = (_rms(x, x.shape[-1]) * g_ref[...]).astype(BF16)
    y = jnp.dot(xn, w_ref[...].astype(BF16), preferred_element_type=F32)
    _headnorm_epilogue(y, (gq_ref,), (o_ref,), heads=heads, scale=scale)


def _norm_xq_proj(x, g_cross, w_xq, layer, g_xq):
    m, d = x.shape
    n = w_xq.shape[2]
    tm = _tile(m, 512, 8)
    body = functools.partial(_norm_xq_body, heads=n // X_DIM, scale=1.0 / math.sqrt(X_DIM))
    return pl.pallas_call(
        body, grid=(m // tm,),
        in_specs=[pl.BlockSpec((tm, d), lambda i: (i, 0)), pl.BlockSpec((1, d), lambda i: (0, 0)),
                  pl.BlockSpec((None, d, n), lambda i: (layer, 0, 0)),
                  pl.BlockSpec((1, X_DIM), lambda i: (0, 0))],
        out_specs=pl.BlockSpec((tm, n), lambda i: (i, 0)),
        out_shape=jax.ShapeDtypeStruct((m, n), BF16),
        compiler_params=_params(1), name="norm_xq_proj")(x, g_cross.reshape(1, d), w_xq, g_xq.reshape(1, X_DIM))


def _out_norm_body(a_ref, w_ref, res_ref, g_ref, x_ref, h_ref):
    x = res_ref[...] + jnp.dot(a_ref[...], w_ref[...].astype(BF16), preferred_element_type=F32)
    x_ref[...] = x
    h_ref[...] = (_rms(x, x.shape[-1]) * g_ref[...]).astype(h_ref.dtype)


def _matmul_residual_norm(a, w, layer, res, g, name):
    m, k = a.shape
    n = w.shape[2]
    tm = _tile(m, 256, 8)
    return pl.pallas_call(
        _out_norm_body, grid=(m // tm,),
        in_specs=[pl.BlockSpec((tm, k), lambda i: (i, 0)), pl.BlockSpec((None, k, n), lambda i: (layer, 0, 0)),
                  pl.BlockSpec((tm, n), lambda i: (i, 0)), pl.BlockSpec((1, n), lambda i: (0, 0))],
        out_specs=[pl.BlockSpec((tm, n), lambda i: (i, 0))] * 2,
        out_shape=[jax.ShapeDtypeStruct((m, n), F32), jax.ShapeDtypeStruct((m, n), BF16)],
        compiler_params=_params(1), name=name)(a, w, res, g.reshape(1, n))


def _xkv_epilogue(y, extras, outs, *, heads):
    g_ref = extras[0]
    k_ref, v_ref = outs
    for h in range(heads):
        sl = slice(h * X_DIM, (h + 1) * X_DIM)
        k = _rms(y[:, 2 * h * X_DIM:(2 * h + 1) * X_DIM], X_DIM) * g_ref[...]
        k_ref[:, sl] = k.astype(k_ref.dtype)
        v_ref[:, sl] = y[:, (2 * h + 1) * X_DIM:(2 * h + 2) * X_DIM].astype(v_ref.dtype)


def _xkv_proj(a, w_xkv, layer, g_xk):
    m, n = a.shape[0], w_xkv.shape[2]
    tm, width = _tile(m, 512, 8), n // 2
    return _matmul(
        a, w_xkv, layer, tm=tm, tn=n, epilogue=functools.partial(_xkv_epilogue, heads=width // X_DIM),
        extras=(g_xk.reshape(1, X_DIM),), extra_specs=(_const_spec(X_DIM),),
        out_shape=[jax.ShapeDtypeStruct((m, width), BF16)] * 2,
        out_specs=[_row_spec(tm, width)] * 2, name="xkv_proj")


def _swiglu_body(a_ref, wg_ref, wu_ref, o_ref):
    a = a_ref[...]
    g = jnp.dot(a, wg_ref[...].astype(BF16), preferred_element_type=F32)
    u = jnp.dot(a, wu_ref[...].astype(BF16), preferred_element_type=F32)
    o_ref[...] = (g / (1.0 + jnp.exp(-g)) * u).astype(o_ref.dtype)


def _swiglu(a, w_gate, w_up, layer):
    m, k = a.shape
    n = w_gate.shape[2]
    tm, tn = _tile(m, 1024, 8), _tile(n, 256, LANE)
    return pl.pallas_call(
        _swiglu_body, grid=(m // tm, n // tn),
        in_specs=[pl.BlockSpec((tm, k), lambda i, j: (i, 0)),
                  _weight_spec(layer, k, tn), _weight_spec(layer, k, tn)],
        out_specs=_tile_spec(tm, tn), out_shape=jax.ShapeDtypeStruct((m, n), BF16),
        compiler_params=_params(2), name="swiglu")(a, w_gate, w_up)


def _dot_nt(a, b):
    return lax.dot_general(a, b, (((1,), (1,)), ((), ())), preferred_element_type=F32)


def _cast_block_rows(rows, steps):
    block = -(-rows // steps)
    while rows % block or block % 16:
        block += 1
    return block


def _mla_attn_body(q_ref, k_ref, v_ref, w_ref, o_ref, w_out_ref, *, tile, cast_blocks):
    step = pl.program_id(0) * pl.num_programs(1) + pl.program_id(1)

    @pl.when(step < cast_blocks)
    def _():
        w_out_ref[...] = w_ref[...].astype(w_out_ref.dtype)

    n_tiles = q_ref.shape[0] // tile
    row = lax.broadcasted_iota(jnp.int32, (tile, tile), 0)
    col = lax.broadcasted_iota(jnp.int32, (tile, tile), 1)

    def scores(qi):
        lo, hi = qi * tile, (qi + 1) * tile
        s = _dot_nt(q_ref[lo:hi, :], k_ref[0:hi, :])
        last = jnp.where(col <= row, s[:, lo:], -jnp.inf)
        return jnp.concatenate([s[:, :lo], last], axis=1) if qi else last

    def probs(s):
        p = jnp.exp2(s - jnp.max(s, axis=1, keepdims=True))
        return p.astype(BF16), jnp.sum(p, axis=1, keepdims=True)

    def values(qi, p, l):
        lo, hi = qi * tile, (qi + 1) * tile
        o_ref[lo:hi, :] = jnp.dot(p, v_ref[0:hi, :], preferred_element_type=F32) / l

    nxt, prev = scores(0), None
    for qi in range(n_tiles):
        now = nxt
        if qi + 1 < n_tiles:
            nxt = scores(qi + 1)
        if qi:
            values(qi - 1, *prev)
        prev = probs(now)
    values(n_tiles - 1, *prev)


def _mla_attention(q, k, v, batch, w, layer):
    t = q.shape[0]
    seq = t // batch
    heads = q.shape[1] // QK_PAD
    tile = _tile(seq, ATTN_TILE, 8)
    rows, cols = w.shape[1:]
    block = _cast_block_rows(rows, batch * heads)
    last = rows // block - 1
    w_block = lambda b, h: jnp.minimum(b * heads + h, last)
    return pl.pallas_call(
        functools.partial(_mla_attn_body, tile=tile, cast_blocks=last + 1), grid=(batch, heads),
        in_specs=[pl.BlockSpec((seq, QK_PAD), lambda b, h: (b, h)),
                  pl.BlockSpec((seq, QK_PAD), lambda b, h: (b, h)),
                  pl.BlockSpec((seq, MLA_V), lambda b, h: (b, h)),
                  pl.BlockSpec((None, block, cols), lambda b, h: (layer, w_block(b, h), 0))],
        out_specs=[pl.BlockSpec((seq, MLA_V), lambda b, h: (b, h)),
                   pl.BlockSpec((None, block, cols), lambda b, h: (0, w_block(b, h), 0))],
        out_shape=[jax.ShapeDtypeStruct((t, heads * MLA_V), F32),
                   jax.ShapeDtypeStruct((1, rows, cols), BF16)],
        compiler_params=_params(2), name="mla_attention")(q, k, v, w)


def _sb_attn_body(q_ref, k_ref, v_ref, o_ref, *, tile):
    n_tiles = q_ref.shape[0] // tile
    row = lax.broadcasted_iota(jnp.int32, (tile, tile), 0)
    col = lax.broadcasted_iota(jnp.int32, (tile, tile), 1)
    strict = col < row
    ones_below = jnp.where(row > col, 1.0, 0.0).astype(BF16)

    def logs(z):
        sign = jnp.uint32(1 << 31)
        neg_abs = lax.bitcast_convert_type(lax.bitcast_convert_type(z, jnp.uint32) | sign, F32)
        log_beta = jnp.minimum(z, 0.0) - jnp.log2(1.0 + jnp.exp2(neg_abs))
        return log_beta, log_beta - z

    def suffix_sums(log_keep):
        return jnp.dot(log_keep.astype(BF16), ones_below, preferred_element_type=F32)

    def logits(qi):
        q = q_ref[qi * tile:(qi + 1) * tile, :]
        return jnp.concatenate(
            [_dot_nt(q, k_ref[c * tile:(c + 1) * tile, :]) for c in range(qi + 1)], axis=0)

    def masked_diagonal(x, qi, fill):
        last = jnp.where(strict, x[qi * tile:], fill)
        return jnp.concatenate([x[:qi * tile], last], axis=0) if qi else last

    def weighted_values(qi, z):
        log_beta, keep = logs(z)
        keep = masked_diagonal(keep, qi, 0.0)
        totals = jnp.sum(keep, axis=1, keepdims=True)
        run = jnp.zeros((tile, 1), F32)
        carries = [None] * (qi + 1)
        for c in reversed(range(qi + 1)):
            carries[c] = run
            run = run + totals[c * tile:(c + 1) * tile]
        a = jnp.exp2(log_beta + suffix_sums(keep) + jnp.concatenate(carries, axis=0))
        a = masked_diagonal(a, qi, 0.0).astype(BF16)
        a = jnp.concatenate([a[c * tile:(c + 1) * tile] for c in range(qi + 1)], axis=1)
        o_ref[qi * tile:(qi + 1) * tile, :] = jnp.dot(a, v_ref[0:(qi + 1) * tile, :],
                                                      preferred_element_type=F32)

    z_next = logits(0)
    for qi in range(n_tiles):
        z_now = z_next
        if qi + 1 < n_tiles:
            z_next = logits(qi + 1)
        weighted_values(qi, z_now)


def _sb_attention(qkv, batch):
    t = qkv.shape[0]
    seq = t // batch
    heads = qkv.shape[1] // (3 * SB_DIM)
    tile = _tile(seq, ATTN_TILE, 8)
    return pl.pallas_call(
        functools.partial(_sb_attn_body, tile=tile), grid=(batch, heads),
        in_specs=[pl.BlockSpec((seq, SB_DIM), lambda b, h: (b, h)),
                  pl.BlockSpec((seq, SB_DIM), lambda b, h: (b, heads + h)),
                  pl.BlockSpec((seq, SB_DIM), lambda b, h: (b, 2 * heads + h))],
        out_specs=pl.BlockSpec((seq, SB_DIM), lambda b, h: (b, h)),
        out_shape=jax.ShapeDtypeStruct((t, heads * SB_DIM), F32),
        compiler_params=_params(2), name="sb_attention")(qkv, qkv, qkv)


def _cross_attn_body(q_ref, k_ref, v_ref, o_ref, *, heads):
    for h in range(heads):
        sl = slice(h * X_DIM, (h + 1) * X_DIM)
        s = _dot_nt(q_ref[:, sl], k_ref[:, sl])
        p = jnp.exp(s - jnp.max(s, axis=1, keepdims=True))
        o = jnp.dot(p.astype(BF16), v_ref[:, sl], preferred_element_type=F32)
        o_ref[:, sl] = (o / jnp.sum(p, axis=1, keepdims=True)).astype(o_ref.dtype)


def _cross_attention(q, k, v, batch):
    t, width = q.shape
    seq, mem_len = t // batch, k.shape[0] // batch
    tq = _tile(seq, 512, 8)
    n_q = seq // tq
    return pl.pallas_call(
        functools.partial(_cross_attn_body, heads=width // X_DIM), grid=(batch, n_q),
        in_specs=[pl.BlockSpec((tq, width), lambda b, i: (b * n_q + i, 0)),
                  pl.BlockSpec((mem_len, width), lambda b, i: (b, 0)),
                  pl.BlockSpec((mem_len, width), lambda b, i: (b, 0))],
        out_specs=pl.BlockSpec((tq, width), lambda b, i: (b * n_q + i, 0)),
        out_shape=jax.ShapeDtypeStruct((t, width), BF16),
        compiler_params=_params(2), name="cross_attention")(q, k, v)


def _twice(g):
    return jnp.tile(g, 2).reshape(1, 2 * g.shape[0])


def _relaid_weights(w_in_t, w_q_up, q_rank, kv_rank):
    assert 2 * MLA_ROPE == LANE and (q_rank + kv_rank) % LANE == 0
    layers = w_in_t.shape[0]
    w_q = w_q_up.reshape(layers, q_rank, MLA_HEADS, MLA_NOPE + MLA_ROPE)
    w_q = jnp.concatenate([w_q, w_q[..., MLA_NOPE:]], axis=-1)
    ws = dict(lat_t=w_in_t[:, :q_rank + kv_rank + LANE],
              q=w_q.reshape(layers, q_rank, MLA_HEADS * QK_PAD))
    return {name: w.astype(BF16) for name, w in ws.items()}


def kernel(x, mem, positions, g_attn, w_in, g_q_lat, g_kv_lat, w_q_up, w_kv_up, g_mla_q, g_mla_k,
           g_mla_out, g_sb_out, w_out, g_cross, g_mem, w_xq, w_xkv, g_xq, g_xk, w_xo, g_ffn,
           w_gate, w_up, w_down):
    batch, seq, d = x.shape
    depth = w_in.shape[0]
    q_rank, kv_rank = g_q_lat.shape[1], g_kv_lat.shape[1]
    x = x.reshape(batch * seq, d)
    mem2 = mem.reshape(-1, d)
    cos, sin = _rope_tables(positions)
    w_in_t = jnp.swapaxes(w_in, 1, 2)
    w = _relaid_weights(w_in_t, w_q_up, q_rank, kv_rank)
    for l in range(depth):
        n, cq, ckv, k_rope = _norm_latent_proj(x, g_attn[l], w["lat_t"], l, g_q_lat[l], g_kv_lat[l],
                                               _twice(g_mla_k[l, MLA_NOPE:]), cos, sin)
        q = _q_up(cq, w["q"], l, g_mla_q[l, :MLA_NOPE].reshape(1, LANE),
                  _twice(g_mla_q[l, MLA_NOPE:]), cos, sin)
        k, v = _kv_up(ckv, w_kv_up, l, g_mla_k[l, :MLA_NOPE].reshape(1, LANE), k_rope)
        o_mla, w_down_bf16 = _mla_attention(q, k, v, batch, w_down, l)
        o_sb = _sb_attention(_sb_proj(n, w_in_t, l, q_rank + kv_rank + MLA_ROPE), batch)
        mixed = _mixnorm(o_mla, o_sb, g_mla_out[l], g_sb_out[l])
        x = _matmul_residual(mixed, w_out, l, x, "out_proj", tm=2048, single_buffered_lhs=True)
        xq = _norm_xq_proj(x, g_cross[l], w_xq, l, g_xq[l])
        xk, xv = _xkv_proj(_rmsnorm(mem2, g_mem[l], "norm_mem"), w_xkv, l, g_xk[l])
        x, h = _matmul_residual_norm(_cross_attention(xq, xk, xv, batch), w_xo, l, x, g_ffn[l], "cross_out")
        x = _matmul_residual(_swiglu(h, w_gate, w_up, l), w_down_bf16, 0, x, "ffn_down", tm=512)
    return x.reshape(batch, seq, d)
```

```python
import functools
import math

import jax
import jax.numpy as jnp
from jax import lax
from jax.experimental import pallas as pl
from jax.experimental.pallas import tpu as pltpu

MLA_HEADS = 16
MLA_NOPE = 128
MLA_ROPE = 64
MLA_V = 128
SB_HEADS = 16
SB_DIM = 128
X_HEADS = 4
X_DIM = 128
ROPE_THETA = 10000.0
EPS = 1e-6

LANE = 128
QK_PAD = 2 * LANE
VMEM_LIMIT_BYTES = 56 * 1024 * 1024
ATTN_TILE = 256
LOG2_E = math.log2(math.e)
BF16 = jnp.bfloat16
F32 = jnp.float32


def _tile(dim, pref, align):
    if dim <= pref:
        return dim
    t = (pref // align) * align
    while t >= align:
        if dim % t == 0:
            return t
        t -= align
    return dim


def _params(ndims):
    return pltpu.CompilerParams(dimension_semantics=("arbitrary",) * ndims,
                                vmem_limit_bytes=VMEM_LIMIT_BYTES)


def _rms(y, width):
    ms = jnp.sum(y * y, axis=-1, keepdims=True) * (1.0 / width)
    return y * lax.rsqrt(ms + EPS)


def _rope(r, cos, sin):
    return r * cos + pltpu.roll(r, MLA_ROPE // 2, 1) * sin


def _rmsnorm_body(x_ref, g_ref, o_ref):
    x = x_ref[...]
    o_ref[...] = (_rms(x, x.shape[-1]) * g_ref[...]).astype(o_ref.dtype)


def _rmsnorm(x, g, name):
    m, d = x.shape
    tm = _tile(m, 256, 8)
    return pl.pallas_call(
        _rmsnorm_body, grid=(m // tm,),
        in_specs=[pl.BlockSpec((tm, d), lambda i: (i, 0)), pl.BlockSpec((1, d), lambda i: (0, 0))],
        out_specs=pl.BlockSpec((tm, d), lambda i: (i, 0)),
        out_shape=jax.ShapeDtypeStruct((m, d), BF16),
        compiler_params=_params(1), name=name)(x, g.reshape(1, d))


def _mixnorm_body(a_ref, b_ref, ga_ref, gb_ref, o_ref):
    wa = a_ref.shape[-1]
    a, b = a_ref[...], b_ref[...]
    o_ref[:, :wa] = (_rms(a, wa) * ga_ref[...]).astype(o_ref.dtype)
    o_ref[:, wa:] = (_rms(b, b.shape[-1]) * gb_ref[...]).astype(o_ref.dtype)


def _mixnorm(a, b, ga, gb):
    m, wa = a.shape
    wb = b.shape[1]
    tm = _tile(m, 256, 8)
    return pl.pallas_call(
        _mixnorm_body, grid=(m // tm,),
        in_specs=[pl.BlockSpec((tm, wa), lambda i: (i, 0)), pl.BlockSpec((tm, wb), lambda i: (i, 0)),
                  pl.BlockSpec((1, wa), lambda i: (0, 0)), pl.BlockSpec((1, wb), lambda i: (0, 0))],
        out_specs=pl.BlockSpec((tm, wa + wb), lambda i: (i, 0)),
        out_shape=jax.ShapeDtypeStruct((m, wa + wb), BF16),
        compiler_params=_params(1), name="mixnorm")(a, b, ga.reshape(1, wa), gb.reshape(1, wb))


def _rope_table_body(pos_ref, freq_ref, cos_ref, sin_ref):
    ang = pos_ref[...] * freq_ref[...]
    lane = lax.broadcasted_iota(jnp.int32, ang.shape, 1)
    sin = jnp.sin(ang)
    cos_ref[...] = jnp.where(lane < MLA_ROPE, jnp.cos(ang), 0.0)
    sin_ref[...] = jnp.where(lane < MLA_ROPE // 2, -sin, jnp.where(lane < MLA_ROPE, sin, 0.0))


def _rope_tables(positions):
    t = positions.size
    half = MLA_ROPE // 2
    inv_freq = ROPE_THETA ** (-jnp.arange(half, dtype=F32) / half)
    freq = jnp.concatenate([inv_freq, inv_freq, jnp.zeros((LANE - MLA_ROPE,), F32)]).reshape(1, LANE)
    pos = positions.astype(F32).reshape(t, 1)
    tm = _tile(t, 512, 8)
    return pl.pallas_call(
        _rope_table_body, grid=(t // tm,),
        in_specs=[pl.BlockSpec((tm, 1), lambda i: (i, 0)), pl.BlockSpec((1, LANE), lambda i: (0, 0))],
        out_specs=[pl.BlockSpec((tm, LANE), lambda i: (i, 0))] * 2,
        out_shape=[jax.ShapeDtypeStruct((t, LANE), F32)] * 2,
        compiler_params=_params(1), name="rope_tables")(pos, freq)


def _weight_spec(layer, k, tn):
    return pl.BlockSpec((None, k, tn), lambda i, j: (layer, 0, j))


def _mm_body(a_ref, w_ref, *rest, n_extra, epilogue):
    extras, outs = rest[:n_extra], rest[n_extra:]
    y = jnp.dot(a_ref[...], w_ref[...].astype(BF16), preferred_element_type=F32)
    epilogue(y, extras, outs)


def _matmul(a, w, layer, *, tm, tn, epilogue, out_shape, out_specs, extras=(), extra_specs=(), name):
    m, k = a.shape
    n = w.shape[2]
    body = functools.partial(_mm_body, n_extra=len(extras), epilogue=epilogue)
    return pl.pallas_call(
        body, grid=(m // tm, n // tn),
        in_specs=[pl.BlockSpec((tm, k), lambda i, j: (i, 0)), _weight_spec(layer, k, tn), *extra_specs],
        out_specs=out_specs, out_shape=out_shape,
        compiler_params=_params(2), name=name)(a, w, *extras)


def _row_spec(tm, width):
    return pl.BlockSpec((tm, width), lambda i, j: (i, 0))


def _const_spec(width):
    return pl.BlockSpec((1, width), lambda i, j: (0, 0))


def _tile_spec(tm, tn):
    return pl.BlockSpec((tm, tn), lambda i, j: (i, j))


def _residual_epilogue(y, extras, outs):
    outs[0][...] = extras[0][...] + y


def _matmul_residual(a, w, layer, res, name, tm=1024, tn=512):
    m, n = a.shape[0], w.shape[2]
    tm, tn = _tile(m, tm, 8), _tile(n, tn, LANE)
    return _matmul(a, w, layer, tm=tm, tn=tn, epilogue=_residual_epilogue,
                   extras=(res,), extra_specs=(_tile_spec(tm, tn),),
                   out_shape=jax.ShapeDtypeStruct((m, n), F32),
                   out_specs=_tile_spec(tm, tn), name=name)


def _latent_epilogue(y, extras, outs, *, q_rank, kv_rank):
    gq_ref, gkv_ref, gkr_ref, cos_ref, sin_ref = extras
    cq_ref, ckv_ref, kr_ref = outs
    cq_ref[...] = (_rms(y[:, :q_rank], q_rank) * gq_ref[...]).astype(cq_ref.dtype)
    ckv = y[:, q_rank:q_rank + kv_rank]
    ckv_ref[...] = (_rms(ckv, kv_rank) * gkv_ref[...]).astype(ckv_ref.dtype)
    group = y[:, q_rank + kv_rank:]
    lane = lax.broadcasted_iota(jnp.int32, group.shape, 1)
    twice = jnp.where(lane < MLA_ROPE, group, pltpu.roll(group, MLA_ROPE, 1))
    kr = _rms(twice, 2 * MLA_ROPE) * gkr_ref[...]
    kr_ref[...] = _rope(kr, cos_ref[...], sin_ref[...]).astype(kr_ref.dtype)


def _norm_latent_body(x_ref, g_ref, w_ref, gq_ref, gkv_ref, gkr_ref, cos_ref, sin_ref,
                      n_ref, cq_ref, ckv_ref, kr_ref, *, q_rank, kv_rank):
    half = x_ref.shape[0] // 2
    halves = [slice(0, half), slice(half, 2 * half)]
    ys = []
    for rows in halves:
        x = x_ref[rows, :]
        n = (_rms(x, x.shape[-1]) * g_ref[...]).astype(BF16)
        n_ref[rows, :] = n
        ys.append(_dot_nt(n, w_ref[...]))
    for rows, y in zip(halves, ys):
        _latent_epilogue(y, (gq_ref, gkv_ref, gkr_ref, cos_ref.at[rows, :], sin_ref.at[rows, :]),
                         (cq_ref.at[rows, :], ckv_ref.at[rows, :], kr_ref.at[rows, :]),
                         q_rank=q_rank, kv_rank=kv_rank)


def _norm_latent_proj(x, g_attn, w_lat_t, layer, g_q_lat, g_kv_lat, g_k_rope, cos, sin):
    m, d = x.shape
    q_rank, kv_rank = g_q_lat.shape[0], g_kv_lat.shape[0]
    n = w_lat_t.shape[1]
    tm = _tile(m, 512, 16)
    row = lambda width: pl.BlockSpec((tm, width), lambda i: (i, 0))
    const = lambda width: pl.BlockSpec((1, width), lambda i: (0, 0))
    weight = pl.BlockSpec((None, n, d), lambda i: (layer, 0, 0), pipeline_mode=pl.Buffered(1))
    return pl.pallas_call(
        functools.partial(_norm_latent_body, q_rank=q_rank, kv_rank=kv_rank), grid=(m // tm,),
        in_specs=[row(d), const(d), weight,
                  const(q_rank), const(kv_rank), const(LANE), row(LANE), row(LANE)],
        out_specs=[row(d), row(q_rank), row(kv_rank), row(LANE)],
        out_shape=[jax.ShapeDtypeStruct((m, d), BF16), jax.ShapeDtypeStruct((m, q_rank), BF16),
                   jax.ShapeDtypeStruct((m, kv_rank), BF16), jax.ShapeDtypeStruct((m, LANE), BF16)],
        compiler_params=_params(1), name="norm_latent_proj")(
            x, g_attn.reshape(1, d), w_lat_t, g_q_lat.reshape(1, -1), g_kv_lat.reshape(1, -1), g_k_rope, cos, sin)


def _sb_proj_body(a_ref, w_ref, o_ref, *, q_tiles, scale):
    j = pl.program_id(1)
    y = _dot_nt(a_ref[...], w_ref[0].astype(BF16))
    o_ref[...] = (y * jnp.where(j < q_tiles, scale, 1.0)).astype(o_ref.dtype)


def _sb_proj(a, w_in_t, layer, first):
    m, k = a.shape
    n = w_in_t.shape[1] - first
    tm, tn = _tile(m, 1024, 8), _tile(n // 3, 512, LANE)
    body = functools.partial(_sb_proj_body, q_tiles=(n // 3) // tn, scale=LOG2_E / math.sqrt(SB_DIM))
    return pl.pallas_call(
        body, grid=(m // tm, n // tn),
        in_specs=[pl.BlockSpec((tm, k), lambda i, j: (i, 0)),
                  pl.BlockSpec((pl.Element(1), pl.Element(tn), pl.Element(k)),
                               lambda i, j: (layer, pl.multiple_of(first + j * tn, 8), 0))],
        out_specs=_tile_spec(tm, tn), out_shape=jax.ShapeDtypeStruct((m, n), BF16),
        compiler_params=_params(2), name="sb_proj")(a, w_in_t)


def _q_up_body(cq_ref, w_ref, gn_ref, gr_ref, cos_ref, sin_ref, o_ref, *, heads, scale):
    cq = cq_ref[...]
    cos, sin = cos_ref[...] * scale, sin_ref[...] * scale
    gn = gn_ref[...] * scale
    row = lax.broadcasted_iota(jnp.int32, (2 * QK_PAD, QK_PAD), 0)
    col = lax.broadcasted_iota(jnp.int32, (2 * QK_PAD, QK_PAD), 1)
    group_mean = jnp.where((row % QK_PAD) // LANE == col // LANE, 1.0 / LANE, 0.0).astype(BF16)
    ys = [jnp.dot(cq, w_ref[:, h * QK_PAD:(h + 1) * QK_PAD], preferred_element_type=F32) for h in range(heads)]
    parts = []
    for y in ys:
        sq = y * y
        hi = sq.astype(BF16)
        parts.append(jnp.concatenate([hi, (sq - hi.astype(F32)).astype(BF16)], axis=1))
    means = [jnp.dot(p, group_mean, preferred_element_type=F32) for p in parts]
    for h in range(heads):
        lo = h * QK_PAD
        yn = ys[h] * lax.rsqrt(means[h] + EPS)
        o_ref[:, lo:lo + MLA_NOPE] = (yn[:, :MLA_NOPE] * gn).astype(o_ref.dtype)
        r = yn[:, MLA_NOPE:] * gr_ref[...]
        o_ref[:, lo + MLA_NOPE:lo + QK_PAD] = _rope(r, cos, sin).astype(o_ref.dtype)


def _q_up(cq, w_q, layer, g_nope, g_rope, cos, sin):
    m, k = cq.shape
    n = w_q.shape[2]
    tm, heads = _tile(m, 1024, 8), 4
    tn = heads * QK_PAD
    body = functools.partial(_q_up_body, heads=heads, scale=LOG2_E / math.sqrt(MLA_NOPE + MLA_ROPE))
    return pl.pallas_call(
        body, grid=(m // tm, n // tn),
        in_specs=[pl.BlockSpec((tm, k), lambda i, j: (i, 0)), _weight_spec(layer, k, tn),
                  _const_spec(LANE), _const_spec(LANE), _row_spec(tm, LANE), _row_spec(tm, LANE)],
        out_specs=_tile_spec(tm, tn), out_shape=jax.ShapeDtypeStruct((m, n), BF16),
        compiler_params=_params(2), name="q_up")(cq, w_q, g_nope, g_rope, cos, sin)


def _kv_up_epilogue(y, extras, outs, *, heads):
    gn_ref, kr_ref = extras
    k_ref, v_ref = outs
    pair = MLA_NOPE + MLA_V
    for h in range(heads):
        k = _rms(y[:, h * pair:h * pair + MLA_NOPE], MLA_NOPE) * gn_ref[...]
        k_ref[:, h * QK_PAD:h * QK_PAD + MLA_NOPE] = k.astype(k_ref.dtype)
        k_ref[:, h * QK_PAD + MLA_NOPE:(h + 1) * QK_PAD] = kr_ref[...]
        v_ref[:, h * MLA_V:(h + 1) * MLA_V] = y[:, h * pair + MLA_NOPE:(h + 1) * pair].astype(v_ref.dtype)


def _kv_up(ckv, w_kv_up, layer, g_nope, k_rope):
    assert MLA_NOPE == LANE and MLA_V == LANE
    m, n = ckv.shape[0], w_kv_up.shape[2]
    all_heads = n // (MLA_NOPE + MLA_V)
    tm, heads = _tile(m, 1024, 8), 4
    return _matmul(
        ckv, w_kv_up, layer, tm=tm, tn=heads * (MLA_NOPE + MLA_V),
        epilogue=functools.partial(_kv_up_epilogue, heads=heads),
        extras=(g_nope, k_rope), extra_specs=(_const_spec(LANE), _row_spec(tm, LANE)),
        out_shape=[jax.ShapeDtypeStruct((m, all_heads * QK_PAD), BF16),
                   jax.ShapeDtypeStruct((m, all_heads * MLA_V), BF16)],
        out_specs=[_tile_spec(tm, heads * QK_PAD), _tile_spec(tm, heads * MLA_V)], name="kv_up")


def _headnorm_epilogue(y, extras, outs, *, heads, scale):
    g_ref = extras[0]
    for h in range(heads):
        sl = slice(h * X_DIM, (h + 1) * X_DIM)
        outs[0][:, sl] = (_rms(y[:, sl], X_DIM) * g_ref[...] * scale).astype(outs[0].dtype)


def _norm_xq_body(x_ref, g_ref, w_ref, gq_ref, o_ref, *, heads, scale):
    x = x_ref[...]
    xn = (_rms(x, x.shape[-1]) * g_ref[...]).astype(BF16)
    y = jnp.dot(xn, w_ref[...].astype(BF16), preferred_element_type=F32)
    _headnorm_epilogue(y, (gq_ref,), (o_ref,), heads=heads, scale=scale)


def _norm_xq_proj(x, g_cross, w_xq, layer, g_xq):
    m, d = x.shape
    n = w_xq.shape[2]
    tm = _tile(m, 512, 8)
    body = functools.partial(_norm_xq_body, heads=n // X_DIM, scale=1.0 / math.sqrt(X_DIM))
    return pl.pallas_call(
        body, grid=(m // tm,),
        in_specs=[pl.BlockSpec((tm, d), lambda i: (i, 0)), pl.BlockSpec((1, d), lambda i: (0, 0)),
                  pl.BlockSpec((None, d, n), lambda i: (layer, 0, 0)),
                  pl.BlockSpec((1, X_DIM), lambda i: (0, 0))],
        out_specs=pl.BlockSpec((tm, n), lambda i: (i, 0)),
        out_shape=jax.ShapeDtypeStruct((m, n), BF16),
        compiler_params=_params(1), name="norm_xq_proj")(x, g_cross.reshape(1, d), w_xq, g_xq.reshape(1, X_DIM))


def _out_norm_body(a_ref, w_ref, res_ref, g_ref, x_ref, h_ref):
    x = res_ref[...] + jnp.dot(a_ref[...], w_ref[...].astype(BF16), preferred_element_type=F32)
    x_ref[...] = x
    h_ref[...] = (_rms(x, x.shape[-1]) * g_ref[...]).astype(h_ref.dtype)


def _matmul_residual_norm(a, w, layer, res, g, name):
    m, k = a.shape
    n = w.shape[2]
    tm = _tile(m, 256, 8)
    return pl.pallas_call(
        _out_norm_body, grid=(m // tm,),
        in_specs=[pl.BlockSpec((tm, k), lambda i: (i, 0)), pl.BlockSpec((None, k, n), lambda i: (layer, 0, 0)),
                  pl.BlockSpec((tm, n), lambda i: (i, 0)), pl.BlockSpec((1, n), lambda i: (0, 0))],
        out_specs=[pl.BlockSpec((tm, n), lambda i: (i, 0))] * 2,
        out_shape=[jax.ShapeDtypeStruct((m, n), F32), jax.ShapeDtypeStruct((m, n), BF16)],
        compiler_params=_params(1), name=name)(a, w, res, g.reshape(1, n))


def _xkv_epilogue(y, extras, outs, *, heads):
    g_ref = extras[0]
    k_ref, v_ref = outs
    for h in range(heads):
        sl = slice(h * X_DIM, (h + 1) * X_DIM)
        k = _rms(y[:, 2 * h * X_DIM:(2 * h + 1) * X_DIM], X_DIM) * g_ref[...]
        k_ref[:, sl] = k.astype(k_ref.dtype)
        v_ref[:, sl] = y[:, (2 * h + 1) * X_DIM:(2 * h + 2) * X_DIM].astype(v_ref.dtype)


def _xkv_proj(a, w_xkv, layer, g_xk):
    m, n = a.shape[0], w_xkv.shape[2]
    tm, width = _tile(m, 512, 8), n // 2
    return _matmul(
        a, w_xkv, layer, tm=tm, tn=n, epilogue=functools.partial(_xkv_epilogue, heads=width // X_DIM),
        extras=(g_xk.reshape(1, X_DIM),), extra_specs=(_const_spec(X_DIM),),
        out_shape=[jax.ShapeDtypeStruct((m, width), BF16)] * 2,
        out_specs=[_row_spec(tm, width)] * 2, name="xkv_proj")


def _swiglu_body(a_ref, wg_ref, wu_ref, o_ref):
    a = a_ref[...]
    g = jnp.dot(a, wg_ref[...].astype(BF16), preferred_element_type=F32)
    u = jnp.dot(a, wu_ref[...].astype(BF16), preferred_element_type=F32)
    o_ref[...] = (g / (1.0 + jnp.exp(-g)) * u).astype(o_ref.dtype)


def _swiglu(a, w_gate, w_up, layer):
    m, k = a.shape
    n = w_gate.shape[2]
    tm, tn = _tile(m, 1024, 8), _tile(n, 256, LANE)
    return pl.pallas_call(
        _swiglu_body, grid=(m // tm, n // tn),
        in_specs=[pl.BlockSpec((tm, k), lambda i, j: (i, 0)),
                  _weight_spec(layer, k, tn), _weight_spec(layer, k, tn)],
        out_specs=_tile_spec(tm, tn), out_shape=jax.ShapeDtypeStruct((m, n), BF16),
        compiler_params=_params(2), name="swiglu")(a, w_gate, w_up)


def _dot_nt(a, b):
    return lax.dot_general(a, b, (((1,), (1,)), ((), ())), preferred_element_type=F32)


def _cast_block_rows(rows, steps):
    block = -(-rows // steps)
    while rows % block or block % 16:
        block += 1
    return block


def _mla_attn_body(q_ref, k_ref, v_ref, w_ref, o_ref, w_out_ref, *, tile, cast_blocks):
    step = pl.program_id(0) * pl.num_programs(1) + pl.program_id(1)

    @pl.when(step < cast_blocks)
    def _():
        w_out_ref[...] = w_ref[...].astype(w_out_ref.dtype)

    n_tiles = q_ref.shape[0] // tile
    row = lax.broadcasted_iota(jnp.int32, (tile, tile), 0)
    col = lax.broadcasted_iota(jnp.int32, (tile, tile), 1)

    def scores(qi):
        lo, hi = qi * tile, (qi + 1) * tile
        s = _dot_nt(q_ref[lo:hi, :], k_ref[0:hi, :])
        last = jnp.where(col <= row, s[:, lo:], -jnp.inf)
        return jnp.concatenate([s[:, :lo], last], axis=1) if qi else last

    def probs(s):
        p = jnp.exp2(s - jnp.max(s, axis=1, keepdims=True))
        return p.astype(BF16), jnp.sum(p, axis=1, keepdims=True)

    def values(qi, p, l):
        lo, hi = qi * tile, (qi + 1) * tile
        o_ref[lo:hi, :] = jnp.dot(p, v_ref[0:hi, :], preferred_element_type=F32) / l

    nxt, prev = scores(0), None
    for qi in range(n_tiles):
        now = nxt
        if qi + 1 < n_tiles:
            nxt = scores(qi + 1)
        if qi:
            values(qi - 1, *prev)
        prev = probs(now)
    values(n_tiles - 1, *prev)


def _mla_attention(q, k, v, batch, w, layer):
    t = q.shape[0]
    seq = t // batch
    heads = q.shape[1] // QK_PAD
    tile = _tile(seq, ATTN_TILE, 8)
    rows, cols = w.shape[1:]
    block = _cast_block_rows(rows, batch * heads)
    last = rows // block - 1
    w_block = lambda b, h: jnp.minimum(b * heads + h, last)
    return pl.pallas_call(
        functools.partial(_mla_attn_body, tile=tile, cast_blocks=last + 1), grid=(batch, heads),
        in_specs=[pl.BlockSpec((seq, QK_PAD), lambda b, h: (b, h)),
                  pl.BlockSpec((seq, QK_PAD), lambda b, h: (b, h)),
                  pl.BlockSpec((seq, MLA_V), lambda b, h: (b, h)),
                  pl.BlockSpec((None, block, cols), lambda b, h: (layer, w_block(b, h), 0))],
        out_specs=[pl.BlockSpec((seq, MLA_V), lambda b, h: (b, h)),
                   pl.BlockSpec((None, block, cols), lambda b, h: (0, w_block(b, h), 0))],
        out_shape=[jax.ShapeDtypeStruct((t, heads * MLA_V), F32),
                   jax.ShapeDtypeStruct((1, rows, cols), BF16)],
        compiler_params=_params(2), name="mla_attention")(q, k, v, w)


def _sb_attn_body(q_ref, k_ref, v_ref, o_ref, *, tile):
    n_tiles = q_ref.shape[0] // tile
    row = lax.broadcasted_iota(jnp.int32, (tile, tile), 0)
    col = lax.broadcasted_iota(jnp.int32, (tile, tile), 1)
    strict = col < row
    ones_below = jnp.where(row > col, 1.0, 0.0).astype(BF16)

    def logs(z):
        sign = jnp.uint32(1 << 31)
        neg_abs = lax.bitcast_convert_type(lax.bitcast_convert_type(z, jnp.uint32) | sign, F32)
        log_beta = jnp.minimum(z, 0.0) - jnp.log2(1.0 + jnp.exp2(neg_abs))
        return log_beta, log_beta - z

    def suffix_sums(log_keep):
        return jnp.dot(log_keep.astype(BF16), ones_below, preferred_element_type=F32)

    def logits(qi):
        q = q_ref[qi * tile:(qi + 1) * tile, :]
        return jnp.concatenate(
            [_dot_nt(q, k_ref[c * tile:(c + 1) * tile, :]) for c in range(qi + 1)], axis=0)

    def masked_diagonal(x, qi, fill):
        last = jnp.where(strict, x[qi * tile:], fill)
        return jnp.concatenate([x[:qi * tile], last], axis=0) if qi else last

    def weighted_values(qi, z):
        log_beta, keep = logs(z)
        keep = masked_diagonal(keep, qi, 0.0)
        totals = jnp.sum(keep, axis=1, keepdims=True)
        run = jnp.zeros((tile, 1), F32)
        carries = [None] * (qi + 1)
        for c in reversed(range(qi + 1)):
            carries[c] = run
            run = run + totals[c * tile:(c + 1) * tile]
        a = jnp.exp2(log_beta + suffix_sums(keep) + jnp.concatenate(carries, axis=0))
        a = masked_diagonal(a, qi, 0.0).astype(BF16)
        a = jnp.concatenate([a[c * tile:(c + 1) * tile] for c in range(qi + 1)], axis=1)
        o_ref[qi * tile:(qi + 1) * tile, :] = jnp.dot(a, v_ref[0:(qi + 1) * tile, :],
                                                      preferred_element_type=F32)

    z_next = logits(0)
    for qi in range(n_tiles):
        z_now = z_next
        if qi + 1 < n_tiles:
            z_next = logits(qi + 1)
        weighted_values(qi, z_now)


def _sb_attention(qkv, batch):
    t = qkv.shape[0]
    seq = t // batch
    heads = qkv.shape[1] // (3 * SB_DIM)
    tile = _tile(seq, ATTN_TILE, 8)
    return pl.pallas_call(
        functools.partial(_sb_attn_body, tile=tile), grid=(batch, heads),
        in_specs=[pl.BlockSpec((seq, SB_DIM), lambda b, h: (b, h)),
                  pl.BlockSpec((seq, SB_DIM), lambda b, h: (b, heads + h)),
                  pl.BlockSpec((seq, SB_DIM), lambda b, h: (b, 2 * heads + h))],
        out_specs=pl.BlockSpec((seq, SB_DIM), lambda b, h: (b, h)),
        out_shape=jax.ShapeDtypeStruct((t, heads * SB_DIM), F32),
        compiler_params=_params(2), name="sb_attention")(qkv, qkv, qkv)


def _cross_attn_body(q_ref, k_ref, v_ref, o_ref, *, heads):
    cols = [slice(h * X_DIM, (h + 1) * X_DIM) for h in range(heads)]
    scores = [_dot_nt(q_ref[:, sl], k_ref[:, sl]) for sl in cols]
    for sl, s in zip(cols, scores):
        p = jnp.exp(s - jnp.max(s, axis=1, keepdims=True))
        o = jnp.dot(p.astype(BF16), v_ref[:, sl], preferred_element_type=F32)
        o_ref[:, sl] = (o / jnp.sum(p, axis=1, keepdims=True)).astype(o_ref.dtype)


def _cross_attention(q, k, v, batch):
    t, width = q.shape
    seq, mem_len = t // batch, k.shape[0] // batch
    tq = _tile(seq, 512, 8)
    n_q = seq // tq
    return pl.pallas_call(
        functools.partial(_cross_attn_body, heads=width // X_DIM), grid=(batch, n_q),
        in_specs=[pl.BlockSpec((tq, width), lambda b, i: (b * n_q + i, 0)),
                  pl.BlockSpec((mem_len, width), lambda b, i: (b, 0)),
                  pl.BlockSpec((mem_len, width), lambda b, i: (b, 0))],
        out_specs=pl.BlockSpec((tq, width), lambda b, i: (b * n_q + i, 0)),
        out_shape=jax.ShapeDtypeStruct((t, width), BF16),
        compiler_params=_params(2), name="cross_attention")(q, k, v)


def _twice(g):
    return jnp.tile(g, 2).reshape(1, 2 * g.shape[0])


def _relaid_weights(w_in_t, w_q_up, q_rank, kv_rank):
    assert 2 * MLA_ROPE == LANE and (q_rank + kv_rank) % LANE == 0
    layers = w_in_t.shape[0]
    w_q = w_q_up.reshape(layers, q_rank, MLA_HEADS, MLA_NOPE + MLA_ROPE)
    w_q = jnp.concatenate([w_q, w_q[..., MLA_NOPE:]], axis=-1)
    ws = dict(lat_t=w_in_t[:, :q_rank + kv_rank + LANE],
              q=w_q.reshape(layers, q_rank, MLA_HEADS * QK_PAD))
    return {name: w.astype(BF16) for name, w in ws.items()}


def kernel(x, mem, positions, g_attn, w_in, g_q_lat, g_kv_lat, w_q_up, w_kv_up, g_mla_q, g_mla_k,
           g_mla_out, g_sb_out, w_out, g_cross, g_mem, w_xq, w_xkv, g_xq, g_xk, w_xo, g_ffn,
           w_gate, w_up, w_down):
    batch, seq, d = x.shape
    depth = w_in.shape[0]
    q_rank, kv_rank = g_q_lat.shape[1], g_kv_lat.shape[1]
    x = x.reshape(batch * seq, d)
    mem2 = mem.reshape(-1, d)
    cos, sin = _rope_tables(positions)
    w_in_t = jnp.swapaxes(w_in, 1, 2)
    w = _relaid_weights(w_in_t, w_q_up, q_rank, kv_rank)
    for l in range(depth):
        n, cq, ckv, k_rope = _norm_latent_proj(x, g_attn[l], w["lat_t"], l, g_q_lat[l], g_kv_lat[l],
                                               _twice(g_mla_k[l, MLA_NOPE:]), cos, sin)
        q = _q_up(cq, w["q"], l, g_mla_q[l, :MLA_NOPE].reshape(1, LANE),
                  _twice(g_mla_q[l, MLA_NOPE:]), cos, sin)
        k, v = _kv_up(ckv, w_kv_up, l, g_mla_k[l, :MLA_NOPE].reshape(1, LANE), k_rope)
        o_mla, w_down_bf16 = _mla_attention(q, k, v, batch, w_down, l)
        o_sb = _sb_attention(_sb_proj(n, w_in_t, l, q_rank + kv_rank + MLA_ROPE), batch)
        mixed = _mixnorm(o_mla, o_sb, g_mla_out[l], g_sb_out[l])
        x = _matmul_residual(mixed, w_out, l, x, "out_proj")
        xq = _norm_xq_proj(x, g_cross[l], w_xq, l, g_xq[l])
        xk, xv = _xkv_proj(_rmsnorm(mem2, g_mem[l], "norm_mem"), w_xkv, l, g_xk[l])
        x, h = _matmul_residual_norm(_cross_attention(xq, xk, xv, batch), w_xo, l, x, g_ffn[l], "cross_out")
        x = _matmul_residual(_swiglu(h, w_gate, w_up, l), w_down_bf16, 0, x, "ffn_down", tm=512)
    return x.reshape(batch, seq, d)
```

```python
import functools
import math

import jax
import jax.numpy as jnp
from jax import lax
from jax.experimental import pallas as pl
from jax.experimental.pallas import tpu as pltpu

MLA_NOPE = 128
MLA_ROPE = 64
MLA_V = 128
SB_DIM = 128
X_DIM = 128
ROPE_THETA = 10000.0
EPS = 1e-6

LANE = 128
QK_PAD = 2 * LANE
VMEM_LIMIT_BYTES = 56 * 1024 * 1024
ATTN_TILE = 256
LOG2_E = math.log2(math.e)
BF16 = jnp.bfloat16
F32 = jnp.float32


def _tile(dim, pref, align):
    if dim <= pref:
        return dim
    t = (pref // align) * align
    while t >= align:
        if dim % t == 0:
            return t
        t -= align
    return dim


def _params(ndims):
    return pltpu.CompilerParams(dimension_semantics=("arbitrary",) * ndims,
                                vmem_limit_bytes=VMEM_LIMIT_BYTES)


def _rms(y, width):
    ms = jnp.sum(y * y, axis=-1, keepdims=True) * (1.0 / width)
    return y * lax.rsqrt(ms + EPS)


def _rope(r, cos, sin):
    return r * cos + pltpu.roll(r, MLA_ROPE // 2, 1) * sin


def _rmsnorm_body(x_ref, g_ref, o_ref):
    x = x_ref[...]
    o_ref[...] = (_rms(x, x.shape[-1]) * g_ref[...]).astype(o_ref.dtype)


def _rmsnorm(x, g, name):
    m, d = x.shape
    tm = _tile(m, 256, 8)
    return pl.pallas_call(
        _rmsnorm_body, grid=(m // tm,),
        in_specs=[pl.BlockSpec((tm, d), lambda i: (i, 0)), pl.BlockSpec((1, d), lambda i: (0, 0))],
        out_specs=pl.BlockSpec((tm, d), lambda i: (i, 0)),
        out_shape=jax.ShapeDtypeStruct((m, d), BF16),
        compiler_params=_params(1), name=name)(x, g.reshape(1, d))


def _mixnorm_body(a_ref, b_ref, ga_ref, gb_ref, o_ref):
    wa = a_ref.shape[-1]
    a, b = a_ref[...], b_ref[...]
    o_ref[:, :wa] = (_rms(a, wa) * ga_ref[...]).astype(o_ref.dtype)
    o_ref[:, wa:] = (_rms(b, b.shape[-1]) * gb_ref[...]).astype(o_ref.dtype)


def _mixnorm(a, b, ga, gb):
    m, wa = a.shape
    wb = b.shape[1]
    tm = _tile(m, 256, 8)
    return pl.pallas_call(
        _mixnorm_body, grid=(m // tm,),
        in_specs=[pl.BlockSpec((tm, wa), lambda i: (i, 0)), pl.BlockSpec((tm, wb), lambda i: (i, 0)),
                  pl.BlockSpec((1, wa), lambda i: (0, 0)), pl.BlockSpec((1, wb), lambda i: (0, 0))],
        out_specs=pl.BlockSpec((tm, wa + wb), lambda i: (i, 0)),
        out_shape=jax.ShapeDtypeStruct((m, wa + wb), BF16),
        compiler_params=_params(1), name="mixnorm")(a, b, ga.reshape(1, wa), gb.reshape(1, wb))


def _rope_table_body(pos_ref, freq_ref, cos_ref, sin_ref):
    ang = pos_ref[...] * freq_ref[...]
    lane = lax.broadcasted_iota(jnp.int32, ang.shape, 1)
    sin = jnp.sin(ang)
    cos_ref[...] = jnp.where(lane < MLA_ROPE, jnp.cos(ang), 0.0)
    sin_ref[...] = jnp.where(lane < MLA_ROPE // 2, -sin, jnp.where(lane < MLA_ROPE, sin, 0.0))


def _rope_tables(positions):
    t = positions.size
    half = MLA_ROPE // 2
    inv_freq = ROPE_THETA ** (-jnp.arange(half, dtype=F32) / half)
    freq = jnp.concatenate([inv_freq, inv_freq, jnp.zeros((LANE - MLA_ROPE,), F32)]).reshape(1, LANE)
    pos = positions.astype(F32).reshape(t, 1)
    tm = _tile(t, 512, 8)
    return pl.pallas_call(
        _rope_table_body, grid=(t // tm,),
        in_specs=[pl.BlockSpec((tm, 1), lambda i: (i, 0)), pl.BlockSpec((1, LANE), lambda i: (0, 0))],
        out_specs=[pl.BlockSpec((tm, LANE), lambda i: (i, 0))] * 2,
        out_shape=[jax.ShapeDtypeStruct((t, LANE), F32)] * 2,
        compiler_params=_params(1), name="rope_tables")(pos, freq)


def _weight_spec(layer, k, tn):
    return pl.BlockSpec((None, k, tn), lambda i, j: (layer, 0, j))


def _lhs_spec(tm, k, single_buffered):
    mode = dict(pipeline_mode=pl.Buffered(1)) if single_buffered else {}
    return pl.BlockSpec((tm, k), lambda i, j: (i, 0), **mode)


def _mm_body(a_ref, w_ref, *rest, n_extra, epilogue):
    extras, outs = rest[:n_extra], rest[n_extra:]
    y = jnp.dot(a_ref[...], w_ref[...].astype(BF16), preferred_element_type=F32)
    epilogue(y, extras, outs)


def _matmul(a, w, layer, *, tm, tn, epilogue, out_shape, out_specs, extras=(), extra_specs=(), name,
            single_buffered_lhs=False):
    m, k = a.shape
    n = w.shape[2]
    body = functools.partial(_mm_body, n_extra=len(extras), epilogue=epilogue)
    return pl.pallas_call(
        body, grid=(m // tm, n // tn),
        in_specs=[_lhs_spec(tm, k, single_buffered_lhs), _weight_spec(layer, k, tn), *extra_specs],
        out_specs=out_specs, out_shape=out_shape,
        compiler_params=_params(2), name=name)(a, w, *extras)


def _row_spec(tm, width):
    return pl.BlockSpec((tm, width), lambda i, j: (i, 0))


def _const_spec(width):
    return pl.BlockSpec((1, width), lambda i, j: (0, 0))


def _tile_spec(tm, tn):
    return pl.BlockSpec((tm, tn), lambda i, j: (i, j))


def _residual_epilogue(y, extras, outs):
    outs[0][...] = extras[0][...] + y


def _matmul_residual(a, w, layer, res, name, tm=1024, tn=512, single_buffered_lhs=False):
    m, n = a.shape[0], w.shape[2]
    tm, tn = _tile(m, tm, 8), _tile(n, tn, LANE)
    return _matmul(a, w, layer, tm=tm, tn=tn, epilogue=_residual_epilogue, single_buffered_lhs=single_buffered_lhs,
                   extras=(res,), extra_specs=(_tile_spec(tm, tn),),
                   out_shape=jax.ShapeDtypeStruct((m, n), F32),
                   out_specs=_tile_spec(tm, tn), name=name)


def _latent_epilogue(y, extras, outs, *, q_rank, kv_rank):
    gq_ref, gkv_ref, gkr_ref, cos_ref, sin_ref = extras
    cq_ref, ckv_ref, kr_ref = outs
    cq_ref[...] = (_rms(y[:, :q_rank], q_rank) * gq_ref[...]).astype(cq_ref.dtype)
    ckv = y[:, q_rank:q_rank + kv_rank]
    ckv_ref[...] = (_rms(ckv, kv_rank) * gkv_ref[...]).astype(ckv_ref.dtype)
    group = y[:, q_rank + kv_rank:]
    lane = lax.broadcasted_iota(jnp.int32, group.shape, 1)
    twice = jnp.where(lane < MLA_ROPE, group, pltpu.roll(group, MLA_ROPE, 1))
    kr = _rms(twice, 2 * MLA_ROPE) * gkr_ref[...]
    kr_ref[...] = _rope(kr, cos_ref[...], sin_ref[...]).astype(kr_ref.dtype)


def _norm_latent_body(x_ref, g_ref, w_ref, gq_ref, gkv_ref, gkr_ref, cos_ref, sin_ref,
                      n_ref, cq_ref, ckv_ref, kr_ref, *, q_rank, kv_rank):
    half = x_ref.shape[0] // 2
    halves = [slice(0, half), slice(half, 2 * half)]
    ys = []
    for rows in halves:
        x = x_ref[rows, :]
        n = (_rms(x, x.shape[-1]) * g_ref[...]).astype(BF16)
        n_ref[rows, :] = n
        ys.append(_dot_nt(n, w_ref[...]))
    for rows, y in zip(halves, ys):
        _latent_epilogue(y, (gq_ref, gkv_ref, gkr_ref, cos_ref.at[rows, :], sin_ref.at[rows, :]),
                         (cq_ref.at[rows, :], ckv_ref.at[rows, :], kr_ref.at[rows, :]),
                         q_rank=q_rank, kv_rank=kv_rank)


def _norm_latent_proj(x, g_attn, w_lat_t, layer, g_q_lat, g_kv_lat, g_k_rope, cos, sin):
    m, d = x.shape
    q_rank, kv_rank = g_q_lat.shape[0], g_kv_lat.shape[0]
    n = w_lat_t.shape[1]
    tm = _tile(m, 512, 16)
    row = lambda width: pl.BlockSpec((tm, width), lambda i: (i, 0))
    const = lambda width: pl.BlockSpec((1, width), lambda i: (0, 0))
    weight = pl.BlockSpec((None, n, d), lambda i: (layer, 0, 0), pipeline_mode=pl.Buffered(1))
    return pl.pallas_call(
        functools.partial(_norm_latent_body, q_rank=q_rank, kv_rank=kv_rank), grid=(m // tm,),
        in_specs=[row(d), const(d), weight,
                  const(q_rank), const(kv_rank), const(LANE), row(LANE), row(LANE)],
        out_specs=[row(d), row(q_rank), row(kv_rank), row(LANE)],
        out_shape=[jax.ShapeDtypeStruct((m, d), BF16), jax.ShapeDtypeStruct((m, q_rank), BF16),
                   jax.ShapeDtypeStruct((m, kv_rank), BF16), jax.ShapeDtypeStruct((m, LANE), BF16)],
        compiler_params=_params(1), name="norm_latent_proj")(
            x, g_attn.reshape(1, d), w_lat_t, g_q_lat.reshape(1, -1), g_kv_lat.reshape(1, -1), g_k_rope, cos, sin)


def _sb_proj_body(a_ref, w_ref, o_ref, *, q_tiles, scale):
    j = pl.program_id(1)
    y = _dot_nt(a_ref[...], w_ref[0].astype(BF16))
    o_ref[...] = (y * jnp.where(j < q_tiles, scale, 1.0)).astype(o_ref.dtype)


def _sb_proj(a, w_in_t, layer, first):
    m, k = a.shape
    n = w_in_t.shape[1] - first
    tm, tn = _tile(m, 2048, 8), _tile(n // 3, 512, LANE)
    body = functools.partial(_sb_proj_body, q_tiles=(n // 3) // tn, scale=LOG2_E / math.sqrt(SB_DIM))
    return pl.pallas_call(
        body, grid=(m // tm, n // tn),
        in_specs=[_lhs_spec(tm, k, True),
                  pl.BlockSpec((pl.Element(1), pl.Element(tn), pl.Element(k)),
                               lambda i, j: (layer, pl.multiple_of(first + j * tn, 8), 0))],
        out_specs=_tile_spec(tm, tn), out_shape=jax.ShapeDtypeStruct((m, n), BF16),
        compiler_params=_params(2), name="sb_proj")(a, w_in_t)


def _q_up_body(cq_ref, w_ref, gn_ref, gr_ref, cos_ref, sin_ref, o_ref, *, heads, scale):
    cos, sin = cos_ref[...] * scale, sin_ref[...] * scale
    gn = gn_ref[...] * scale
    row = lax.broadcasted_iota(jnp.int32, (2 * QK_PAD, QK_PAD), 0)
    col = lax.broadcasted_iota(jnp.int32, (2 * QK_PAD, QK_PAD), 1)
    group_mean = jnp.where((row % QK_PAD) // LANE == col // LANE, 1.0 / LANE, 0.0).astype(BF16)
    y = jnp.dot(cq_ref[...], w_ref[...].astype(BF16), preferred_element_type=F32)
    low = lax.broadcasted_iota(jnp.int32, (y.shape[0], LANE), 1) < MLA_ROPE
    swap = lambda g: pltpu.roll(g, MLA_ROPE, 1)
    ys = []
    for p in range(heads // 2):
        a, b, c = (y[:, (3 * p + i) * LANE:(3 * p + i + 1) * LANE] for i in range(3))
        b_swapped, c_swapped = swap(b), swap(c)
        ys.append(jnp.concatenate([a, jnp.where(low, b, b_swapped)], axis=1))
        ys.append(jnp.concatenate([jnp.where(low, b_swapped, c_swapped), jnp.where(low, c_swapped, c)], axis=1))
    parts = []
    for yh in ys:
        sq = yh * yh
        hi = sq.astype(BF16)
        parts.append(jnp.concatenate([hi, (sq - hi.astype(F32)).astype(BF16)], axis=1))
    means = [jnp.dot(p, group_mean, preferred_element_type=F32) for p in parts]
    for h in range(heads):
        lo = h * QK_PAD
        yn = ys[h] * lax.rsqrt(means[h] + EPS)
        o_ref[:, lo:lo + MLA_NOPE] = (yn[:, :MLA_NOPE] * gn).astype(o_ref.dtype)
        r = yn[:, MLA_NOPE:] * gr_ref[...]
        o_ref[:, lo + MLA_NOPE:lo + QK_PAD] = _rope(r, cos, sin).astype(o_ref.dtype)


def _q_up(cq, w_q_up, layer, g_nope, g_rope, cos, sin):
    assert 2 * MLA_ROPE == LANE and MLA_NOPE == LANE
    m, k = cq.shape
    all_heads = w_q_up.shape[2] // (MLA_NOPE + MLA_ROPE)
    tm, heads = _tile(m, 1024, 8), 4
    body = functools.partial(_q_up_body, heads=heads, scale=LOG2_E / math.sqrt(MLA_NOPE + MLA_ROPE))
    return pl.pallas_call(
        body, grid=(m // tm, all_heads // heads),
        in_specs=[pl.BlockSpec((tm, k), lambda i, j: (i, 0)),
                  _weight_spec(layer, k, heads * (MLA_NOPE + MLA_ROPE)),
                  _const_spec(LANE), _const_spec(LANE), _row_spec(tm, LANE), _row_spec(tm, LANE)],
        out_specs=_tile_spec(tm, heads * QK_PAD),
        out_shape=jax.ShapeDtypeStruct((m, all_heads * QK_PAD), BF16),
        compiler_params=_params(2), name="q_up")(cq, w_q_up, g_nope, g_rope, cos, sin)


def _kv_up_epilogue(y, extras, outs, *, heads):
    gn_ref, kr_ref = extras
    k_ref, v_ref = outs
    pair = MLA_NOPE + MLA_V
    for h in range(heads):
        k = _rms(y[:, h * pair:h * pair + MLA_NOPE], MLA_NOPE) * gn_ref[...]
        k_ref[:, h * QK_PAD:h * QK_PAD + MLA_NOPE] = k.astype(k_ref.dtype)
        k_ref[:, h * QK_PAD + MLA_NOPE:(h + 1) * QK_PAD] = kr_ref[...]
        v_ref[:, h * MLA_V:(h + 1) * MLA_V] = y[:, h * pair + MLA_NOPE:(h + 1) * pair].astype(v_ref.dtype)


def _kv_up(ckv, w_kv_up, layer, g_nope, k_rope):
    assert MLA_NOPE == LANE and MLA_V == LANE
    m, n = ckv.shape[0], w_kv_up.shape[2]
    all_heads = n // (MLA_NOPE + MLA_V)
    tm, heads = _tile(m, 1024, 8), 4
    return _matmul(
        ckv, w_kv_up, layer, tm=tm, tn=heads * (MLA_NOPE + MLA_V),
        epilogue=functools.partial(_kv_up_epilogue, heads=heads),
        extras=(g_nope, k_rope), extra_specs=(_const_spec(LANE), _row_spec(tm, LANE)),
        out_shape=[jax.ShapeDtypeStruct((m, all_heads * QK_PAD), BF16),
                   jax.ShapeDtypeStruct((m, all_heads * MLA_V), BF16)],
        out_specs=[_tile_spec(tm, heads * QK_PAD), _tile_spec(tm, heads * MLA_V)], name="kv_up")


def _headnorm_epilogue(y, extras, outs, *, heads, scale):
    g_ref = extras[0]
    for h in range(heads):
        sl = slice(h * X_DIM, (h + 1) * X_DIM)
        outs[0][:, sl] = (_rms(y[:, sl], X_DIM) * g_ref[...] * scale).astype(outs[0].dtype)


def _norm_xq_body(x_ref, g_ref, w_ref, gq_ref, o_ref, *, heads, scale):
    x = x_ref[...]
    xn = (_rms(x, x.shape[-1]) * g_ref[...]).astype(BF16)
    y = jnp.dot(xn, w_ref[...].astype(BF16), preferred_element_type=F32)
    _headnorm_epilogue(y, (gq_ref,), (o_ref,), heads=heads, scale=scale)


def _norm_xq_proj(x, g_cross, w_xq, layer, g_xq):
    m, d = x.shape
    n = w_xq.shape[2]
    tm = _tile(m, 512, 8)
    body = functools.partial(_norm_xq_body, heads=n // X_DIM, scale=1.0 / math.sqrt(X_DIM))
    return pl.pallas_call(
        body, grid=(m // tm,),
        in_specs=[pl.BlockSpec((tm, d), lambda i: (i, 0)), pl.BlockSpec((1, d), lambda i: (0, 0)),
                  pl.BlockSpec((None, d, n), lambda i: (layer, 0, 0)),
                  pl.BlockSpec((1, X_DIM), lambda i: (0, 0))],
        out_specs=pl.BlockSpec((tm, n), lambda i: (i, 0)),
        out_shape=jax.ShapeDtypeStruct((m, n), BF16),
        compiler_params=_params(1), name="norm_xq_proj")(x, g_cross.reshape(1, d), w_xq, g_xq.reshape(1, X_DIM))


def _out_norm_body(a_ref, w_ref, res_ref, g_ref, x_ref, h_ref):
    x = res_ref[...] + jnp.dot(a_ref[...], w_ref[...].astype(BF16), preferred_element_type=F32)
    x_ref[...] = x
    h_ref[...] = (_rms(x, x.shape[-1]) * g_ref[...]).astype(h_ref.dtype)


def _matmul_residual_norm(a, w, layer, res, g, name):
    m, k = a.shape
    n = w.shape[2]
    tm = _tile(m, 256, 8)
    return pl.pallas_call(
        _out_norm_body, grid=(m // tm,),
        in_specs=[pl.BlockSpec((tm, k), lambda i: (i, 0)), pl.BlockSpec((None, k, n), lambda i: (layer, 0, 0)),
                  pl.BlockSpec((tm, n), lambda i: (i, 0)), pl.BlockSpec((1, n), lambda i: (0, 0))],
        out_specs=[pl.BlockSpec((tm, n), lambda i: (i, 0))] * 2,
        out_shape=[jax.ShapeDtypeStruct((m, n), F32), jax.ShapeDtypeStruct((m, n), BF16)],
        compiler_params=_params(1), name=name)(a, w, res, g.reshape(1, n))


def _xkv_epilogue(y, extras, outs, *, heads):
    g_ref = extras[0]
    k_ref, v_ref = outs
    for h in range(heads):
        sl = slice(h * X_DIM, (h + 1) * X_DIM)
        k = _rms(y[:, 2 * h * X_DIM:(2 * h + 1) * X_DIM], X_DIM) * g_ref[...]
        k_ref[:, sl] = k.astype(k_ref.dtype)
        v_ref[:, sl] = y[:, (2 * h + 1) * X_DIM:(2 * h + 2) * X_DIM].astype(v_ref.dtype)


def _xkv_proj(a, w_xkv, layer, g_xk):
    m, n = a.shape[0], w_xkv.shape[2]
    tm, width = _tile(m, 512, 8), n // 2
    return _matmul(
        a, w_xkv, layer, tm=tm, tn=n, epilogue=functools.partial(_xkv_epilogue, heads=width // X_DIM),
        extras=(g_xk.reshape(1, X_DIM),), extra_specs=(_const_spec(X_DIM),),
        out_shape=[jax.ShapeDtypeStruct((m, width), BF16)] * 2,
        out_specs=[_row_spec(tm, width)] * 2, name="xkv_proj")


def _swiglu_body(a_ref, wg_ref, wu_ref, o_ref):
    a = a_ref[...]
    g = jnp.dot(a, wg_ref[...].astype(BF16), preferred_element_type=F32)
    u = jnp.dot(a, wu_ref[...].astype(BF16), preferred_element_type=F32)
    o_ref[...] = (g / (1.0 + jnp.exp(-g)) * u).astype(o_ref.dtype)


def _swiglu(a, w_gate, w_up, layer):
    m, k = a.shape
    n = w_gate.shape[2]
    tm, tn = _tile(m, 1024, 8), _tile(n, 256, LANE)
    return pl.pallas_call(
        _swiglu_body, grid=(m // tm, n // tn),
        in_specs=[pl.BlockSpec((tm, k), lambda i, j: (i, 0)),
                  _weight_spec(layer, k, tn), _weight_spec(layer, k, tn)],
        out_specs=_tile_spec(tm, tn), out_shape=jax.ShapeDtypeStruct((m, n), BF16),
        compiler_params=_params(2), name="swiglu")(a, w_gate, w_up)


def _dot_nt(a, b):
    return lax.dot_general(a, b, (((1,), (1,)), ((), ())), preferred_element_type=F32)


def _cast_block_rows(rows, steps):
    block = -(-rows // steps)
    while rows % block or block % 16:
        block += 1
    return block


def _mla_attn_body(q_ref, k_ref, v_ref, w_ref, o_ref, w_out_ref, *, tile, cast_blocks):
    step = pl.program_id(0) * pl.num_programs(1) + pl.program_id(1)

    @pl.when(step < cast_blocks)
    def _():
        w_out_ref[...] = w_ref[...].astype(w_out_ref.dtype)

    n_tiles = q_ref.shape[0] // tile
    row = lax.broadcasted_iota(jnp.int32, (tile, tile), 0)
    col = lax.broadcasted_iota(jnp.int32, (tile, tile), 1)

    def scores(qi):
        lo, hi = qi * tile, (qi + 1) * tile
        s = _dot_nt(q_ref[lo:hi, :], k_ref[0:hi, :])
        last = jnp.where(col <= row, s[:, lo:], -jnp.inf)
        return jnp.concatenate([s[:, :lo], last], axis=1) if qi else last

    def probs(s):
        p = jnp.exp2(s - jnp.max(s, axis=1, keepdims=True))
        return p.astype(BF16), jnp.sum(p, axis=1, keepdims=True)

    def values(qi, p, l):
        lo, hi = qi * tile, (qi + 1) * tile
        o_ref[lo:hi, :] = jnp.dot(p, v_ref[0:hi, :], preferred_element_type=F32) / l

    nxt, prev = scores(0), None
    for qi in range(n_tiles):
        now = nxt
        if qi + 1 < n_tiles:
            nxt = scores(qi + 1)
        if qi:
            values(qi - 1, *prev)
        prev = probs(now)
    values(n_tiles - 1, *prev)


def _mla_attention(q, k, v, batch, w, layer):
    t = q.shape[0]
    seq = t // batch
    heads = q.shape[1] // QK_PAD
    tile = _tile(seq, ATTN_TILE, 8)
    rows, cols = w.shape[1:]
    block = _cast_block_rows(rows, batch * heads)
    last = rows // block - 1
    w_block = lambda b, h: jnp.minimum(b * heads + h, last)
    return pl.pallas_call(
        functools.partial(_mla_attn_body, tile=tile, cast_blocks=last + 1), grid=(batch, heads),
        in_specs=[pl.BlockSpec((seq, QK_PAD), lambda b, h: (b, h)),
                  pl.BlockSpec((seq, QK_PAD), lambda b, h: (b, h)),
                  pl.BlockSpec((seq, MLA_V), lambda b, h: (b, h)),
                  pl.BlockSpec((None, block, cols), lambda b, h: (layer, w_block(b, h), 0))],
        out_specs=[pl.BlockSpec((seq, MLA_V), lambda b, h: (b, h)),
                   pl.BlockSpec((None, block, cols), lambda b, h: (0, w_block(b, h), 0))],
        out_shape=[jax.ShapeDtypeStruct((t, heads * MLA_V), F32),
                   jax.ShapeDtypeStruct((1, rows, cols), BF16)],
        compiler_params=_params(2), name="mla_attention")(q, k, v, w)


def _sb_attn_body(q_ref, k_ref, v_ref, o_ref, *, tile):
    n_tiles = q_ref.shape[0] // tile
    row = lax.broadcasted_iota(jnp.int32, (tile, tile), 0)
    col = lax.broadcasted_iota(jnp.int32, (tile, tile), 1)
    strict = col < row
    ones_below = jnp.where(row > col, 1.0, 0.0).astype(BF16)

    def logs(z):
        sign = jnp.uint32(1 << 31)
        neg_abs = lax.bitcast_convert_type(lax.bitcast_convert_type(z, jnp.uint32) | sign, F32)
        log_beta = jnp.minimum(z, 0.0) - jnp.log2(1.0 + jnp.exp2(neg_abs))
        return log_beta, log_beta - z

    def suffix_sums(log_keep):
        return jnp.dot(log_keep.astype(BF16), ones_below, preferred_element_type=F32)

    def logits(qi):
        q = q_ref[qi * tile:(qi + 1) * tile, :]
        return jnp.concatenate(
            [_dot_nt(q, k_ref[c * tile:(c + 1) * tile, :]) for c in range(qi + 1)], axis=0)

    def masked_diagonal(x, qi, fill):
        last = jnp.where(strict, x[qi * tile:], fill)
        return jnp.concatenate([x[:qi * tile], last], axis=0) if qi else last

    def weighted_values(qi, z):
        log_beta, keep = logs(z)
        keep = masked_diagonal(keep, qi, 0.0)
        totals = jnp.sum(keep, axis=1, keepdims=True)
        run = jnp.zeros((tile, 1), F32)
        carries = [None] * (qi + 1)
        for c in reversed(range(qi + 1)):
            carries[c] = run
            run = run + totals[c * tile:(c + 1) * tile]
        a = jnp.exp2(log_beta + suffix_sums(keep) + jnp.concatenate(carries, axis=0))
        a = masked_diagonal(a, qi, 0.0).astype(BF16)
        a = jnp.concatenate([a[c * tile:(c + 1) * tile] for c in range(qi + 1)], axis=1)
        o_ref[qi * tile:(qi + 1) * tile, :] = jnp.dot(a, v_ref[0:(qi + 1) * tile, :],
                                                      preferred_element_type=F32)

    z_next = logits(0)
    for qi in range(n_tiles):
        z_now = z_next
        if qi + 1 < n_tiles:
            z_next = logits(qi + 1)
        weighted_values(qi, z_now)


def _sb_attention(qkv, batch):
    t = qkv.shape[0]
    seq = t // batch
    heads = qkv.shape[1] // (3 * SB_DIM)
    tile = _tile(seq, ATTN_TILE, 8)
    return pl.pallas_call(
        functools.partial(_sb_attn_body, tile=tile), grid=(batch, heads),
        in_specs=[pl.BlockSpec((seq, SB_DIM), lambda b, h: (b, h)),
                  pl.BlockSpec((seq, SB_DIM), lambda b, h: (b, heads + h)),
                  pl.BlockSpec((seq, SB_DIM), lambda b, h: (b, 2 * heads + h))],
        out_specs=pl.BlockSpec((seq, SB_DIM), lambda b, h: (b, h)),
        out_shape=jax.ShapeDtypeStruct((t, heads * SB_DIM), F32),
        compiler_params=_params(2), name="sb_attention")(qkv, qkv, qkv)


def _cross_attn_body(q_ref, k_ref, v_ref, o_ref, *, heads):
    for h in range(heads):
        sl = slice(h * X_DIM, (h + 1) * X_DIM)
        s = _dot_nt(q_ref[:, sl], k_ref[:, sl])
        p = jnp.exp(s - jnp.max(s, axis=1, keepdims=True))
        o = jnp.dot(p.astype(BF16), v_ref[:, sl], preferred_element_type=F32)
        o_ref[:, sl] = (o / jnp.sum(p, axis=1, keepdims=True)).astype(o_ref.dtype)


def _cross_attention(q, k, v, batch):
    t, width = q.shape
    seq, mem_len = t // batch, k.shape[0] // batch
    tq = _tile(seq, 512, 8)
    n_q = seq // tq
    return pl.pallas_call(
        functools.partial(_cross_attn_body, heads=width // X_DIM), grid=(batch, n_q),
        in_specs=[pl.BlockSpec((tq, width), lambda b, i: (b * n_q + i, 0)),
                  pl.BlockSpec((mem_len, width), lambda b, i: (b, 0)),
                  pl.BlockSpec((mem_len, width), lambda b, i: (b, 0))],
        out_specs=pl.BlockSpec((tq, width), lambda b, i: (b * n_q + i, 0)),
        out_shape=jax.ShapeDtypeStruct((t, width), BF16),
        compiler_params=_params(2), name="cross_attention")(q, k, v)


def _twice(g):
    return jnp.tile(g, 2).reshape(1, 2 * g.shape[0])


def _latent_rows(w_in_t, q_rank, kv_rank):
    assert 2 * MLA_ROPE == LANE and (q_rank + kv_rank) % LANE == 0
    return w_in_t[:, :q_rank + kv_rank + LANE].astype(BF16)


def kernel(x, mem, positions, g_attn, w_in, g_q_lat, g_kv_lat, w_q_up, w_kv_up, g_mla_q, g_mla_k,
           g_mla_out, g_sb_out, w_out, g_cross, g_mem, w_xq, w_xkv, g_xq, g_xk, w_xo, g_ffn,
           w_gate, w_up, w_down):
    batch, seq, d = x.shape
    depth = w_in.shape[0]
    q_rank, kv_rank = g_q_lat.shape[1], g_kv_lat.shape[1]
    x = x.reshape(batch * seq, d)
    mem2 = mem.reshape(-1, d)
    cos, sin = _rope_tables(positions)
    w_in_t = jnp.swapaxes(w_in, 1, 2)
    w_lat_t = _latent_rows(w_in_t, q_rank, kv_rank)
    for l in range(depth):
        n, cq, ckv, k_rope = _norm_latent_proj(x, g_attn[l], w_lat_t, l, g_q_lat[l], g_kv_lat[l],
                                               _twice(g_mla_k[l, MLA_NOPE:]), cos, sin)
        q = _q_up(cq, w_q_up, l, g_mla_q[l, :MLA_NOPE].reshape(1, LANE),
                  _twice(g_mla_q[l, MLA_NOPE:]), cos, sin)
        k, v = _kv_up(ckv, w_kv_up, l, g_mla_k[l, :MLA_NOPE].reshape(1, LANE), k_rope)
        o_mla, w_down_bf16 = _mla_attention(q, k, v, batch, w_down, l)
        o_sb = _sb_attention(_sb_proj(n, w_in_t, l, q_rank + kv_rank + MLA_ROPE), batch)
        mixed = _mixnorm(o_mla, o_sb, g_mla_out[l], g_sb_out[l])
        x = _matmul_residual(mixed, w_out, l, x, "out_proj", tm=2048, single_buffered_lhs=True)
        xq = _norm_xq_proj(x, g_cross[l], w_xq, l, g_xq[l])
        xk, xv = _xkv_proj(_rmsnorm(mem2, g_mem[l], "norm_mem"), w_xkv, l, g_xk[l])
        x, h = _matmul_residual_norm(_cross_attention(xq, xk, xv, batch), w_xo, l, x, g_ffn[l], "cross_out")
        x = _matmul_residual(_swiglu(h, w_gate, w_up, l), w_down_bf16, 0, x, "ffn_down", tm=512)
    return x.reshape(batch, seq, d)
```

```python
import functools
import math

import jax
import jax.numpy as jnp
from jax import lax
from jax.experimental import pallas as pl
from jax.experimental.pallas import tpu as pltpu

MLA_NOPE = 128
MLA_ROPE = 64
MLA_V = 128
SB_DIM = 128
X_DIM = 128
ROPE_THETA = 10000.0
EPS = 1e-6

LANE = 128
QK_PAD = 2 * LANE
VMEM_LIMIT_BYTES = 56 * 1024 * 1024
ATTN_TILE = 256
LOG2_E = math.log2(math.e)
BF16 = jnp.bfloat16
F32 = jnp.float32


def _tile(dim, pref, align):
    if dim <= pref:
        return dim
    t = (pref // align) * align
    while t >= align:
        if dim % t == 0:
            return t
        t -= align
    return dim


def _params(ndims):
    return pltpu.CompilerParams(dimension_semantics=("arbitrary",) * ndims,
                                vmem_limit_bytes=VMEM_LIMIT_BYTES)


def _rms(y, width):
    ms = jnp.sum(y * y, axis=-1, keepdims=True) * (1.0 / width)
    return y * lax.rsqrt(ms + EPS)


def _rope(r, cos, sin):
    return r * cos + pltpu.roll(r, MLA_ROPE // 2, 1) * sin


def _rmsnorm_body(x_ref, g_ref, o_ref):
    x = x_ref[...]
    o_ref[...] = (_rms(x, x.shape[-1]) * g_ref[...]).astype(o_ref.dtype)


def _rmsnorm(x, g, name):
    m, d = x.shape
    tm = _tile(m, 256, 8)
    return pl.pallas_call(
        _rmsnorm_body, grid=(m // tm,),
        in_specs=[pl.BlockSpec((tm, d), lambda i: (i, 0)), pl.BlockSpec((1, d), lambda i: (0, 0))],
        out_specs=pl.BlockSpec((tm, d), lambda i: (i, 0)),
        out_shape=jax.ShapeDtypeStruct((m, d), BF16),
        compiler_params=_params(1), name=name)(x, g.reshape(1, d))


def _mixnorm_body(a_ref, b_ref, ga_ref, gb_ref, o_ref):
    wa = a_ref.shape[-1]
    a, b = a_ref[...], b_ref[...]
    o_ref[:, :wa] = (_rms(a, wa) * ga_ref[...]).astype(o_ref.dtype)
    o_ref[:, wa:] = (_rms(b, b.shape[-1]) * gb_ref[...]).astype(o_ref.dtype)


def _mixnorm(a, b, ga, gb):
    m, wa = a.shape
    wb = b.shape[1]
    tm = _tile(m, 256, 8)
    return pl.pallas_call(
        _mixnorm_body, grid=(m // tm,),
        in_specs=[pl.BlockSpec((tm, wa), lambda i: (i, 0)), pl.BlockSpec((tm, wb), lambda i: (i, 0)),
                  pl.BlockSpec((1, wa), lambda i: (0, 0)), pl.BlockSpec((1, wb), lambda i: (0, 0))],
        out_specs=pl.BlockSpec((tm, wa + wb), lambda i: (i, 0)),
        out_shape=jax.ShapeDtypeStruct((m, wa + wb), BF16),
        compiler_params=_params(1), name="mixnorm")(a, b, ga.reshape(1, wa), gb.reshape(1, wb))


def _rope_table_body(pos_ref, freq_ref, cos_ref, sin_ref):
    ang = pos_ref[...] * freq_ref[...]
    lane = lax.broadcasted_iota(jnp.int32, ang.shape, 1)
    sin = jnp.sin(ang)
    cos_ref[...] = jnp.where(lane < MLA_ROPE, jnp.cos(ang), 0.0)
    sin_ref[...] = jnp.where(lane < MLA_ROPE // 2, -sin, jnp.where(lane < MLA_ROPE, sin, 0.0))


def _rope_tables(positions):
    t = positions.size
    half = MLA_ROPE // 2
    inv_freq = ROPE_THETA ** (-jnp.arange(half, dtype=F32) / half)
    freq = jnp.concatenate([inv_freq, inv_freq, jnp.zeros((LANE - MLA_ROPE,), F32)]).reshape(1, LANE)
    pos = positions.astype(F32).reshape(t, 1)
    tm = _tile(t, 512, 8)
    return pl.pallas_call(
        _rope_table_body, grid=(t // tm,),
        in_specs=[pl.BlockSpec((tm, 1), lambda i: (i, 0)), pl.BlockSpec((1, LANE), lambda i: (0, 0))],
        out_specs=[pl.BlockSpec((tm, LANE), lambda i: (i, 0))] * 2,
        out_shape=[jax.ShapeDtypeStruct((t, LANE), F32)] * 2,
        compiler_params=_params(1), name="rope_tables")(pos, freq)


def _weight_spec(layer, k, tn):
    return pl.BlockSpec((None, k, tn), lambda i, j: (layer, 0, j))


def _mm_body(a_ref, w_ref, *rest, n_extra, epilogue):
    extras, outs = rest[:n_extra], rest[n_extra:]
    y = jnp.dot(a_ref[...], w_ref[...].astype(BF16), preferred_element_type=F32)
    epilogue(y, extras, outs)


def _matmul(a, w, layer, *, tm, tn, epilogue, out_shape, out_specs, extras=(), extra_specs=(), name):
    m, k = a.shape
    n = w.shape[2]
    body = functools.partial(_mm_body, n_extra=len(extras), epilogue=epilogue)
    return pl.pallas_call(
        body, grid=(m // tm, n // tn),
        in_specs=[pl.BlockSpec((tm, k), lambda i, j: (i, 0)), _weight_spec(layer, k, tn), *extra_specs],
        out_specs=out_specs, out_shape=out_shape,
        compiler_params=_params(2), name=name)(a, w, *extras)


def _row_spec(tm, width):
    return pl.BlockSpec((tm, width), lambda i, j: (i, 0))


def _const_spec(width):
    return pl.BlockSpec((1, width), lambda i, j: (0, 0))


def _tile_spec(tm, tn):
    return pl.BlockSpec((tm, tn), lambda i, j: (i, j))


def _residual_epilogue(y, extras, outs):
    outs[0][...] = extras[0][...] + y


def _matmul_residual(a, w, layer, res, name, tm=1024, tn=512):
    m, n = a.shape[0], w.shape[2]
    tm, tn = _tile(m, tm, 8), _tile(n, tn, LANE)
    return _matmul(a, w, layer, tm=tm, tn=tn, epilogue=_residual_epilogue,
                   extras=(res,), extra_specs=(_tile_spec(tm, tn),),
                   out_shape=jax.ShapeDtypeStruct((m, n), F32),
                   out_specs=_tile_spec(tm, tn), name=name)


def _latent_epilogue(y, extras, outs, *, q_rank, kv_rank):
    gq_ref, gkv_ref, gkr_ref, cos_ref, sin_ref = extras
    cq_ref, ckv_ref, kr_ref = outs
    cq_ref[...] = (_rms(y[:, :q_rank], q_rank) * gq_ref[...]).astype(cq_ref.dtype)
    ckv = y[:, q_rank:q_rank + kv_rank]
    ckv_ref[...] = (_rms(ckv, kv_rank) * gkv_ref[...]).astype(ckv_ref.dtype)
    group = y[:, q_rank + kv_rank:]
    lane = lax.broadcasted_iota(jnp.int32, group.shape, 1)
    twice = jnp.where(lane < MLA_ROPE, group, pltpu.roll(group, MLA_ROPE, 1))
    kr = _rms(twice, 2 * MLA_ROPE) * gkr_ref[...]
    kr_ref[...] = _rope(kr, cos_ref[...], sin_ref[...]).astype(kr_ref.dtype)


def _norm_latent_body(x_ref, g_ref, w_ref, gq_ref, gkv_ref, gkr_ref, cos_ref, sin_ref,
                      n_ref, cq_ref, ckv_ref, kr_ref, *, q_rank, kv_rank):
    half = x_ref.shape[0] // 2
    halves = [slice(0, half), slice(half, 2 * half)]
    ys = []
    for rows in halves:
        x = x_ref[rows, :]
        n = (_rms(x, x.shape[-1]) * g_ref[...]).astype(BF16)
        n_ref[rows, :] = n
        ys.append(_dot_nt(n, w_ref[...]))
    for rows, y in zip(halves, ys):
        _latent_epilogue(y, (gq_ref, gkv_ref, gkr_ref, cos_ref.at[rows, :], sin_ref.at[rows, :]),
                         (cq_ref.at[rows, :], ckv_ref.at[rows, :], kr_ref.at[rows, :]),
                         q_rank=q_rank, kv_rank=kv_rank)


def _norm_latent_proj(x, g_attn, w_lat_t, layer, g_q_lat, g_kv_lat, g_k_rope, cos, sin):
    m, d = x.shape
    q_rank, kv_rank = g_q_lat.shape[0], g_kv_lat.shape[0]
    n = w_lat_t.shape[1]
    tm = _tile(m, 512, 16)
    row = lambda width: pl.BlockSpec((tm, width), lambda i: (i, 0))
    const = lambda width: pl.BlockSpec((1, width), lambda i: (0, 0))
    weight = pl.BlockSpec((None, n, d), lambda i: (layer, 0, 0), pipeline_mode=pl.Buffered(1))
    return pl.pallas_call(
        functools.partial(_norm_latent_body, q_rank=q_rank, kv_rank=kv_rank), grid=(m // tm,),
        in_specs=[row(d), const(d), weight,
                  const(q_rank), const(kv_rank), const(LANE), row(LANE), row(LANE)],
        out_specs=[row(d), row(q_rank), row(kv_rank), row(LANE)],
        out_shape=[jax.ShapeDtypeStruct((m, d), BF16), jax.ShapeDtypeStruct((m, q_rank), BF16),
                   jax.ShapeDtypeStruct((m, kv_rank), BF16), jax.ShapeDtypeStruct((m, LANE), BF16)],
        compiler_params=_params(1), name="norm_latent_proj")(
            x, g_attn.reshape(1, d), w_lat_t, g_q_lat.reshape(1, -1), g_kv_lat.reshape(1, -1), g_k_rope, cos, sin)


def _sb_proj_body(a_ref, w_ref, o_ref, *, q_tiles, scale):
    j = pl.program_id(1)
    y = _dot_nt(a_ref[...], w_ref[0].astype(BF16))
    o_ref[...] = (y * jnp.where(j < q_tiles, scale, 1.0)).astype(o_ref.dtype)


def _sb_proj(a, w_in_t, layer, first):
    m, k = a.shape
    n = w_in_t.shape[1] - first
    tm, tn = _tile(m, 1024, 8), _tile(n // 3, 512, LANE)
    body = functools.partial(_sb_proj_body, q_tiles=(n // 3) // tn, scale=LOG2_E / math.sqrt(SB_DIM))
    return pl.pallas_call(
        body, grid=(m // tm, n // tn),
        in_specs=[pl.BlockSpec((tm, k), lambda i, j: (i, 0)),
                  pl.BlockSpec((pl.Element(1), pl.Element(tn), pl.Element(k)),
                               lambda i, j: (layer, pl.multiple_of(first + j * tn, 8), 0))],
        out_specs=_tile_spec(tm, tn), out_shape=jax.ShapeDtypeStruct((m, n), BF16),
        compiler_params=_params(2), name="sb_proj")(a, w_in_t)


def _q_up_body(cq_ref, w_ref, gn_ref, gr_ref, cos_ref, sin_ref, o_ref, *, heads, scale):
    cos, sin = cos_ref[...] * scale, sin_ref[...] * scale
    gn = gn_ref[...] * scale
    row = lax.broadcasted_iota(jnp.int32, (2 * QK_PAD, QK_PAD), 0)
    col = lax.broadcasted_iota(jnp.int32, (2 * QK_PAD, QK_PAD), 1)
    group_mean = jnp.where((row % QK_PAD) // LANE == col // LANE, 1.0 / LANE, 0.0).astype(BF16)
    y = jnp.dot(cq_ref[...], w_ref[...].astype(BF16), preferred_element_type=F32)
    low = lax.broadcasted_iota(jnp.int32, (y.shape[0], LANE), 1) < MLA_ROPE
    swap = lambda g: pltpu.roll(g, MLA_ROPE, 1)
    ys = []
    for p in range(heads // 2):
        a, b, c = (y[:, (3 * p + i) * LANE:(3 * p + i + 1) * LANE] for i in range(3))
        b_swapped, c_swapped = swap(b), swap(c)
        ys.append(jnp.concatenate([a, jnp.where(low, b, b_swapped)], axis=1))
        ys.append(jnp.concatenate([jnp.where(low, b_swapped, c_swapped), jnp.where(low, c_swapped, c)], axis=1))
    parts = []
    for yh in ys:
        sq = yh * yh
        hi = sq.astype(BF16)
        parts.append(jnp.concatenate([hi, (sq - hi.astype(F32)).astype(BF16)], axis=1))
    means = [jnp.dot(p, group_mean, preferred_element_type=F32) for p in parts]
    for h in range(heads):
        lo = h * QK_PAD
        yn = ys[h] * lax.rsqrt(means[h] + EPS)
        o_ref[:, lo:lo + MLA_NOPE] = (yn[:, :MLA_NOPE] * gn).astype(o_ref.dtype)
        r = yn[:, MLA_NOPE:] * gr_ref[...]
        o_ref[:, lo + MLA_NOPE:lo + QK_PAD] = _rope(r, cos, sin).astype(o_ref.dtype)


def _q_up(cq, w_q_up, layer, g_nope, g_rope, cos, sin):
    assert 2 * MLA_ROPE == LANE and MLA_NOPE == LANE
    m, k = cq.shape
    all_heads = w_q_up.shape[2] // (MLA_NOPE + MLA_ROPE)
    tm, heads = _tile(m, 1024, 8), 4
    body = functools.partial(_q_up_body, heads=heads, scale=LOG2_E / math.sqrt(MLA_NOPE + MLA_ROPE))
    return pl.pallas_call(
        body, grid=(m // tm, all_heads // heads),
        in_specs=[pl.BlockSpec((tm, k), lambda i, j: (i, 0)),
                  _weight_spec(layer, k, heads * (MLA_NOPE + MLA_ROPE)),
                  _const_spec(LANE), _const_spec(LANE), _row_spec(tm, LANE), _row_spec(tm, LANE)],
        out_specs=_tile_spec(tm, heads * QK_PAD),
        out_shape=jax.ShapeDtypeStruct((m, all_heads * QK_PAD), BF16),
        compiler_params=_params(2), name="q_up")(cq, w_q_up, g_nope, g_rope, cos, sin)


def _kv_up_epilogue(y, extras, outs, *, heads):
    gn_ref, kr_ref = extras
    k_ref, v_ref = outs
    pair = MLA_NOPE + MLA_V
    for h in range(heads):
        k = _rms(y[:, h * pair:h * pair + MLA_NOPE], MLA_NOPE) * gn_ref[...]
        k_ref[:, h * QK_PAD:h * QK_PAD + MLA_NOPE] = k.astype(k_ref.dtype)
        k_ref[:, h * QK_PAD + MLA_NOPE:(h + 1) * QK_PAD] = kr_ref[...]
        v_ref[:, h * MLA_V:(h + 1) * MLA_V] = y[:, h * pair + MLA_NOPE:(h + 1) * pair].astype(v_ref.dtype)


def _kv_up(ckv, w_kv_up, layer, g_nope, k_rope):
    assert MLA_NOPE == LANE and MLA_V == LANE
    m, n = ckv.shape[0], w_kv_up.shape[2]
    all_heads = n // (MLA_NOPE + MLA_V)
    tm, heads = _tile(m, 1024, 8), 4
    return _matmul(
        ckv, w_kv_up, layer, tm=tm, tn=heads * (MLA_NOPE + MLA_V),
        epilogue=functools.partial(_kv_up_epilogue, heads=heads),
        extras=(g_nope, k_rope), extra_specs=(_const_spec(LANE), _row_spec(tm, LANE)),
        out_shape=[jax.ShapeDtypeStruct((m, all_heads * QK_PAD), BF16),
                   jax.ShapeDtypeStruct((m, all_heads * MLA_V), BF16)],
        out_specs=[_tile_spec(tm, heads * QK_PAD), _tile_spec(tm, heads * MLA_V)], name="kv_up")


def _headnorm_epilogue(y, extras, outs, *, heads, scale):
    g_ref = extras[0]
    for h in range(heads):
        sl = slice(h * X_DIM, (h + 1) * X_DIM)
        outs[0][:, sl] = (_rms(y[:, sl], X_DIM) * g_ref[...] * scale).astype(outs[0].dtype)


def _norm_xq_body(x_ref, g_ref, w_ref, gq_ref, o_ref, *, heads, scale):
    x = x_ref[...]
    xn = (_rms(x, x.shape[-1]) * g_ref[...]).astype(BF16)
    y = jnp.dot(xn, w_ref[...].astype(BF16), preferred_element_type=F32)
    _headnorm_epilogue(y, (gq_ref,), (o_ref,), heads=heads, scale=scale)


def _norm_xq_proj(x, g_cross, w_xq, layer, g_xq):
    m, d = x.shape
    n = w_xq.shape[2]
    tm = _tile(m, 512, 8)
    body = functools.partial(_norm_xq_body, heads=n // X_DIM, scale=1.0 / math.sqrt(X_DIM))
    return pl.pallas_call(
        body, grid=(m // tm,),
        in_specs=[pl.BlockSpec((tm, d), lambda i: (i, 0)), pl.BlockSpec((1, d), lambda i: (0, 0)),
                  pl.BlockSpec((None, d, n), lambda i: (layer, 0, 0)),
                  pl.BlockSpec((1, X_DIM), lambda i: (0, 0))],
        out_specs=pl.BlockSpec((tm, n), lambda i: (i, 0)),
        out_shape=jax.ShapeDtypeStruct((m, n), BF16),
        compiler_params=_params(1), name="norm_xq_proj")(x, g_cross.reshape(1, d), w_xq, g_xq.reshape(1, X_DIM))


def _out_norm_body(a_ref, w_ref, res_ref, g_ref, x_ref, h_ref):
    x = res_ref[...] + jnp.dot(a_ref[...], w_ref[...].astype(BF16), preferred_element_type=F32)
    x_ref[...] = x
    h_ref[...] = (_rms(x, x.shape[-1]) * g_ref[...]).astype(h_ref.dtype)


def _matmul_residual_norm(a, w, layer, res, g, name):
    m, k = a.shape
    n = w.shape[2]
    tm = _tile(m, 256, 8)
    return pl.pallas_call(
        _out_norm_body, grid=(m // tm,),
        in_specs=[pl.BlockSpec((tm, k), lambda i: (i, 0)), pl.BlockSpec((None, k, n), lambda i: (layer, 0, 0)),
                  pl.BlockSpec((tm, n), lambda i: (i, 0)), pl.BlockSpec((1, n), lambda i: (0, 0))],
        out_specs=[pl.BlockSpec((tm, n), lambda i: (i, 0))] * 2,
        out_shape=[jax.ShapeDtypeStruct((m, n), F32), jax.ShapeDtypeStruct((m, n), BF16)],
        compiler_params=_params(1), name=name)(a, w, res, g.reshape(1, n))


def _xkv_epilogue(y, extras, outs, *, heads):
    g_ref = extras[0]
    k_ref, v_ref = outs
    for h in range(heads):
        sl = slice(h * X_DIM, (h + 1) * X_DIM)
        k = _rms(y[:, 2 * h * X_DIM:(2 * h + 1) * X_DIM], X_DIM) * g_ref[...]
        k_ref[:, sl] = k.astype(k_ref.dtype)
        v_ref[:, sl] = y[:, (2 * h + 1) * X_DIM:(2 * h + 2) * X_DIM].astype(v_ref.dtype)


def _xkv_proj(a, w_xkv, layer, g_xk):
    m, n = a.shape[0], w_xkv.shape[2]
    tm, width = _tile(m, 512, 8), n // 2
    return _matmul(
        a, w_xkv, layer, tm=tm, tn=n, epilogue=functools.partial(_xkv_epilogue, heads=width // X_DIM),
        extras=(g_xk.reshape(1, X_DIM),), extra_specs=(_const_spec(X_DIM),),
        out_shape=[jax.ShapeDtypeStruct((m, width), BF16)] * 2,
        out_specs=[_row_spec(tm, width)] * 2, name="xkv_proj")


def _swiglu_body(a_ref, wg_ref, wu_ref, o_ref):
    a = a_ref[...]
    g = jnp.dot(a, wg_ref[...].astype(BF16), preferred_element_type=F32)
    u = jnp.dot(a, wu_ref[...].astype(BF16), preferred_element_type=F32)
    o_ref[...] = (g / (1.0 + jnp.exp(-g)) * u).astype(o_ref.dtype)


def _swiglu(a, w_gate, w_up, layer):
    m, k = a.shape
    n = w_gate.shape[2]
    tm, tn = _tile(m, 1024, 8), _tile(n, 256, LANE)
    return pl.pallas_call(
        _swiglu_body, grid=(m // tm, n // tn),
        in_specs=[pl.BlockSpec((tm, k), lambda i, j: (i, 0)),
                  _weight_spec(layer, k, tn), _weight_spec(layer, k, tn)],
        out_specs=_tile_spec(tm, tn), out_shape=jax.ShapeDtypeStruct((m, n), BF16),
        compiler_params=_params(2), name="swiglu")(a, w_gate, w_up)


def _dot_nt(a, b):
    return lax.dot_general(a, b, (((1,), (1,)), ((), ())), preferred_element_type=F32)


def _cast_block_rows(rows, steps):
    block = -(-rows // steps)
    while rows % block or block % 16:
        block += 1
    return block


def _mla_attn_body(q_ref, k_ref, v_ref, w_ref, o_ref, w_out_ref, *, tile, cast_blocks):
    step = pl.program_id(0) * pl.num_programs(1) + pl.program_id(1)

    @pl.when(step < cast_blocks)
    def _():
        w_out_ref[...] = w_ref[...].astype(w_out_ref.dtype)

    n_tiles = q_ref.shape[0] // tile
    row = lax.broadcasted_iota(jnp.int32, (tile, tile), 0)
    col = lax.broadcasted_iota(jnp.int32, (tile, tile), 1)

    def scores(qi):
        lo, hi = qi * tile, (qi + 1) * tile
        s = _dot_nt(q_ref[lo:hi, :], k_ref[0:hi, :])
        last = jnp.where(col <= row, s[:, lo:], -jnp.inf)
        return jnp.concatenate([s[:, :lo], last], axis=1) if qi else last

    def probs(s):
        p = jnp.exp2(s - jnp.max(s, axis=1, keepdims=True))
        return p.astype(BF16), jnp.sum(p, axis=1, keepdims=True)

    def values(qi, p, l):
        lo, hi = qi * tile, (qi + 1) * tile
        o_ref[lo:hi, :] = jnp.dot(p, v_ref[0:hi, :], preferred_element_type=F32) / l

    nxt, prev = scores(0), None
    for qi in range(n_tiles):
        now = nxt
        if qi + 1 < n_tiles:
            nxt = scores(qi + 1)
        if qi:
            values(qi - 1, *prev)
        prev = probs(now)
    values(n_tiles - 1, *prev)


def _mla_attention(q, k, v, batch, w, layer):
    t = q.shape[0]
    seq = t // batch
    heads = q.shape[1] // QK_PAD
    tile = _tile(seq, ATTN_TILE, 8)
    rows, cols = w.shape[1:]
    block = _cast_block_rows(rows, batch * heads)
    last = rows // block - 1
    w_block = lambda b, h: jnp.minimum(b * heads + h, last)
    return pl.pallas_call(
        functools.partial(_mla_attn_body, tile=tile, cast_blocks=last + 1), grid=(batch, heads),
        in_specs=[pl.BlockSpec((seq, QK_PAD), lambda b, h: (b, h)),
                  pl.BlockSpec((seq, QK_PAD), lambda b, h: (b, h)),
                  pl.BlockSpec((seq, MLA_V), lambda b, h: (b, h)),
                  pl.BlockSpec((None, block, cols), lambda b, h: (layer, w_block(b, h), 0))],
        out_specs=[pl.BlockSpec((seq, MLA_V), lambda b, h: (b, h)),
                   pl.BlockSpec((None, block, cols), lambda b, h: (0, w_block(b, h), 0))],
        out_shape=[jax.ShapeDtypeStruct((t, heads * MLA_V), F32),
                   jax.ShapeDtypeStruct((1, rows, cols), BF16)],
        compiler_params=_params(2), name="mla_attention")(q, k, v, w)


def _sb_attn_body(q_ref, k_ref, v_ref, o_ref, *, tile):
    n_tiles = q_ref.shape[0] // tile
    row = lax.broadcasted_iota(jnp.int32, (tile, tile), 0)
    col = lax.broadcasted_iota(jnp.int32, (tile, tile), 1)
    strict = col < row
    ones_below = jnp.where(row > col, 1.0, 0.0).astype(BF16)

    def logs(z):
        sign = jnp.uint32(1 << 31)
        neg_abs = lax.bitcast_convert_type(lax.bitcast_convert_type(z, jnp.uint32) | sign, F32)
        log_beta = jnp.minimum(z, 0.0) - jnp.log2(1.0 + jnp.exp2(neg_abs))
        return log_beta, log_beta - z

    def suffix_sums(log_keep):
        return jnp.dot(log_keep.astype(BF16), ones_below, preferred_element_type=F32)

    def logits(qi):
        q = q_ref[qi * tile:(qi + 1) * tile, :]
        return jnp.concatenate(
            [_dot_nt(q, k_ref[c * tile:(c + 1) * tile, :]) for c in range(qi + 1)], axis=0)

    def masked_diagonal(x, qi, fill):
        last = jnp.where(strict, x[qi * tile:], fill)
        return jnp.concatenate([x[:qi * tile], last], axis=0) if qi else last

    def weighted_values(qi, z):
        log_beta, keep = logs(z)
        keep = masked_diagonal(keep, qi, 0.0)
        totals = jnp.sum(keep, axis=1, keepdims=True)
        run = jnp.zeros((tile, 1), F32)
        carries = [None] * (qi + 1)
        for c in reversed(range(qi + 1)):
            carries[c] = run
            run = run + totals[c * tile:(c + 1) * tile]
        a = jnp.exp2(log_beta + suffix_sums(keep) + jnp.concatenate(carries, axis=0))
        a = masked_diagonal(a, qi, 0.0).astype(BF16)
        a = jnp.concatenate([a[c * tile:(c + 1) * tile] for c in range(qi + 1)], axis=1)
        o_ref[qi * tile:(qi + 1) * tile, :] = jnp.dot(a, v_ref[0:(qi + 1) * tile, :],
                                                      preferred_element_type=F32)

    z_next = logits(0)
    for qi in range(n_tiles):
        z_now = z_next
        if qi + 1 < n_tiles:
            z_next = logits(qi + 1)
        weighted_values(qi, z_now)


def _sb_attention(qkv, batch):
    t = qkv.shape[0]
    seq = t // batch
    heads = qkv.shape[1] // (3 * SB_DIM)
    tile = _tile(seq, ATTN_TILE, 8)
    return pl.pallas_call(
        functools.partial(_sb_attn_body, tile=tile), grid=(batch, heads),
        in_specs=[pl.BlockSpec((seq, SB_DIM), lambda b, h: (b, h)),
                  pl.BlockSpec((seq, SB_DIM), lambda b, h: (b, heads + h)),
                  pl.BlockSpec((seq, SB_DIM), lambda b, h: (b, 2 * heads + h))],
        out_specs=pl.BlockSpec((seq, SB_DIM), lambda b, h: (b, h)),
        out_shape=jax.ShapeDtypeStruct((t, heads * SB_DIM), F32),
        compiler_params=_params(2), name="sb_attention")(qkv, qkv, qkv)


def _cross_attn_body(q_ref, k_ref, v_ref, o_ref, *, heads):
    for h in range(heads):
        sl = slice(h * X_DIM, (h + 1) * X_DIM)
        s = _dot_nt(q_ref[:, sl], k_ref[:, sl])
        p = jnp.exp(s - jnp.max(s, axis=1, keepdims=True))
        o = jnp.dot(p.astype(BF16), v_ref[:, sl], preferred_element_type=F32)
        o_ref[:, sl] = (o / jnp.sum(p, axis=1, keepdims=True)).astype(o_ref.dtype)


def _cross_attention(q, k, v, batch):
    t, width = q.shape
    seq, mem_len = t // batch, k.shape[0] // batch
    tq = _tile(seq, 512, 8)
    n_q = seq // tq
    return pl.pallas_call(
        functools.partial(_cross_attn_body, heads=width // X_DIM), grid=(batch, n_q),
        in_specs=[pl.BlockSpec((tq, width), lambda b, i: (b * n_q + i, 0)),
                  pl.BlockSpec((mem_len, width), lambda b, i: (b, 0)),
                  pl.BlockSpec((mem_len, width), lambda b, i: (b, 0))],
        out_specs=pl.BlockSpec((tq, width), lambda b, i: (b * n_q + i, 0)),
        out_shape=jax.ShapeDtypeStruct((t, width), BF16),
        compiler_params=_params(2), name="cross_attention")(q, k, v)


def _twice(g):
    return jnp.tile(g, 2).reshape(1, 2 * g.shape[0])


def _latent_rows(w_in_t, q_rank, kv_rank):
    assert 2 * MLA_ROPE == LANE and (q_rank + kv_rank) % LANE == 0
    return w_in_t[:, :q_rank + kv_rank + LANE].astype(BF16)


def kernel(x, mem, positions, g_attn, w_in, g_q_lat, g_kv_lat, w_q_up, w_kv_up, g_mla_q, g_mla_k,
           g_mla_out, g_sb_out, w_out, g_cross, g_mem, w_xq, w_xkv, g_xq, g_xk, w_xo, g_ffn,
           w_gate, w_up, w_down):
    batch, seq, d = x.shape
    depth = w_in.shape[0]
    q_rank, kv_rank = g_q_lat.shape[1], g_kv_lat.shape[1]
    x = x.reshape(batch * seq, d)
    mem2 = mem.reshape(-1, d)
    cos, sin = _rope_tables(positions)
    w_in_t = jnp.swapaxes(w_in, 1, 2)
    w_lat_t = _latent_rows(w_in_t, q_rank, kv_rank)
    for l in range(depth):
        n, cq, ckv, k_rope = _norm_latent_proj(x, g_attn[l], w_lat_t, l, g_q_lat[l], g_kv_lat[l],
                                               _twice(g_mla_k[l, MLA_NOPE:]), cos, sin)
        q = _q_up(cq, w_q_up, l, g_mla_q[l, :MLA_NOPE].reshape(1, LANE),
                  _twice(g_mla_q[l, MLA_NOPE:]), cos, sin)
        k, v = _kv_up(ckv, w_kv_up, l, g_mla_k[l, :MLA_NOPE].reshape(1, LANE), k_rope)
        o_mla, w_down_bf16 = _mla_attention(q, k, v, batch, w_down, l)
        o_sb = _sb_attention(_sb_proj(n, w_in_t, l, q_rank + kv_rank + MLA_ROPE), batch)
        mixed = _mixnorm(o_mla, o_sb, g_mla_out[l], g_sb_out[l])
        x = _matmul_residual(mixed, w_out, l, x, "out_proj")
        xq = _norm_xq_proj(x, g_cross[l], w_xq, l, g_xq[l])
        xk, xv = _xkv_proj(_rmsnorm(mem2, g_mem[l], "norm_mem"), w_xkv, l, g_xk[l])
        x, h = _matmul_residual_norm(_cross_attention(xq, xk, xv, batch), w_xo, l, x, g_ffn[l], "cross_out")
        x = _matmul_residual(_swiglu(h, w_gate, w_up, l), w_down_bf16, 0, x, "ffn_down", tm=512)
    return x.reshape(batch, seq, d)
```
